```python
import math
import jax
import jax.numpy as jnp
from jax import lax
import numpy as np

D_MODEL = 2048
BATCH = 2
SEQ = 4096
DEPTH = 4
DEC_BATCH = 8
DEC_SEQ = 8
PAST_LEN = 16384
PAGE_SIZE = 128

N_MIXERS = 3
N_LAYERS_A = (DEPTH + 2) // 3
N_LAYERS_B = (DEPTH + 1) // 3
N_LAYERS_C = DEPTH // 3
ALPHA = (2.0 * DEPTH) ** 0.25
BETA_INIT = (8.0 * DEPTH) ** -0.25
LN_EPS = 1e-5
NORM_EPS = 1e-6
NEG = -1e30
FORCE = 1e9
HEAD_DIM = 128
ATTN_SCALE = HEAD_DIM ** -0.5

A_HEADS = D_MODEL // HEAD_DIM
A_KV = 4
A_REP = A_HEADS // A_KV
A_IDX_HEADS = 16
A_IDX_DIM = 64
A_TOPK = 256
A_QBLK = 128
A_COLS = A_HEADS * HEAD_DIM + 2 * A_KV * HEAD_DIM + A_IDX_HEADS * A_IDX_DIM + A_IDX_DIM + A_IDX_HEADS

B_QK_HEADS = 16
B_V_HEADS = 32
B_DK = 128
B_DV = 128
B_CONV = 4
B_CHUNK = 64
B_CONV_CH = 2 * B_QK_HEADS * B_DK + B_V_HEADS * B_DV
B_COLS = B_CONV_CH + B_V_HEADS * B_DV + 2 * B_V_HEADS

C_HEADS = D_MODEL // HEAD_DIM
C_KV = 4
C_REP = C_HEADS // C_KV
CMP_BLK = 32
CMP_STRIDE = 16
SLC_BLK = 64
N_SEL = 16
WINDOW = 512
C_QBLK = 32
WIN_QBLK = 128
C_COLS = C_HEADS * HEAD_DIM + 6 * C_KV * HEAD_DIM + 3 * C_HEADS

N_EXPERTS = 32
TOP_K = 4
D_FF = 2048
SWIGLU_LIMIT = 7.0
SWIGLU_ALPHA = 1.702
MOE_BLK = 128

kernel_name = "hybrid_dsa_gdn_nsa_moe_step"


def split_cols(z, sizes):
    return jnp.split(z, np.cumsum(sizes)[:-1].tolist(), axis=-1)


def layer_norm(x, g, b):
    xf = x.astype(jnp.float32)
    mu = jnp.mean(xf, axis=-1, keepdims=True)
    var = jnp.mean(jnp.square(xf - mu), axis=-1, keepdims=True)
    return ((xf - mu) * lax.rsqrt(var + LN_EPS) * g + b).astype(x.dtype)


def rms_norm(x, g):
    xf = x.astype(jnp.float32)
    return xf * lax.rsqrt(jnp.mean(jnp.square(xf), axis=-1, keepdims=True) + NORM_EPS) * g


def l2_normalize(x):
    xf = x.astype(jnp.float32)
    return xf * lax.rsqrt(jnp.sum(jnp.square(xf), axis=-1, keepdims=True) + NORM_EPS)


def masked_softmax(logits, mask):
    return jax.nn.softmax(jnp.where(mask, logits, NEG), axis=-1)


def modulation(c, w, b):
    return (jax.nn.silu(c) @ w + b).reshape(c.shape[0], 6, D_MODEL)


def modulate(x, shift, scale):
    return x * (1.0 + scale[:, None, :]) + shift[:, None, :]


def post_norm(x, y, gate, g, b):
    return layer_norm(ALPHA * x + gate[:, None, :] * y, g, b)


def gather_pages(pool, page_table, layer):
    rows = pool[page_table, :, layer]
    return rows.reshape(rows.shape[0], -1, *rows.shape[3:])


def gather_rows(pool, page_table, layer, pos):
    b = jnp.arange(pos.shape[0]).reshape((-1,) + (1,) * (pos.ndim - 1))
    phys = page_table[b, pos // PAGE_SIZE]
    return pool[phys, pos % PAGE_SIZE, layer]


def gather_group_rows(pool, page_table, layer, pos, kv_j):
    b = jnp.arange(pos.shape[0])[:, None, None, None, None]
    g = jnp.arange(C_KV)[None, None, :, None, None]
    phys = page_table[b, pos // PAGE_SIZE]
    return pool[phys, pos % PAGE_SIZE, layer, kv_j, g]


def dsa_project(h, w_in):
    B, T, _ = h.shape
    q, k, v, qi, ki, wi = split_cols(h @ w_in, [A_HEADS * HEAD_DIM, A_KV * HEAD_DIM, A_KV * HEAD_DIM,
                                                A_IDX_HEADS * A_IDX_DIM, A_IDX_DIM, A_IDX_HEADS])
    q = q.reshape(B, T, A_KV, A_REP, HEAD_DIM)
    kv = jnp.stack([k.reshape(B, T, A_KV, HEAD_DIM), v.reshape(B, T, A_KV, HEAD_DIM)], axis=2)
    qi = qi.reshape(B, T, A_IDX_HEADS, A_IDX_DIM)
    return q, kv, qi, ki, wi * A_IDX_HEADS ** -0.5


def indexer_scores(qi, wi, ki, q_pos, k_pos):
    s = jax.nn.relu(jnp.einsum("bthd,bsd->bths", qi, ki).astype(jnp.float32) * A_IDX_DIM ** -0.5)
    score = jnp.einsum("bths,bth->bts", s, wi.astype(jnp.float32))
    return jnp.where(k_pos[None, None, :] <= q_pos[None, :, None], score, NEG)


def gathered_attend(q, kv_sel, ok):
    logits = jnp.einsum("btgrd,btkgd->btgrk", q, kv_sel[:, :, :, 0]).astype(jnp.float32) * ATTN_SCALE
    p = masked_softmax(logits, ok[:, :, None, None, :])
    return jnp.einsum("btgrk,btkgd->btgrd", p.astype(q.dtype), kv_sel[:, :, :, 1])


def dsa_prompt(h, w_in, w_out):
    B, T, _ = h.shape
    q, kv, qi, ki, wi = dsa_project(h, w_in)
    topk = min(A_TOPK, T // 4)
    k_pos = jnp.arange(T)
    b_ix = jnp.arange(B)[:, None, None]

    def block(start):
        sl = lambda a: lax.dynamic_slice_in_dim(a, start, A_QBLK, axis=1)
        q_pos = start + jnp.arange(A_QBLK)
        _, idx = lax.top_k(indexer_scores(sl(qi), sl(wi), ki, q_pos, k_pos), topk)
        return gathered_attend(sl(q), kv[b_ix, idx], idx <= q_pos[None, :, None])

    o = lax.map(block, jnp.arange(0, T, A_QBLK))
    o = jnp.moveaxis(o, 0, 1).reshape(B, T, A_HEADS * HEAD_DIM)
    return o @ w_out, kv, ki


def dsa_sample(h, w_in, w_out, pool_kv, pool_idx, page_table, layer):
    B, T, _ = h.shape
    q, kv, qi, ki, wi = dsa_project(h, w_in)
    L = PAST_LEN + T
    topk = min(A_TOPK, L // 4)
    q_pos = PAST_LEN + jnp.arange(T)
    ki_all = jnp.concatenate([gather_pages(pool_idx, page_table, layer).astype(ki.dtype), ki], axis=1)
    _, idx = lax.top_k(indexer_scores(qi, wi, ki_all, q_pos, jnp.arange(L)), topk)
    b_ix = jnp.arange(B)[:, None, None]
    kv_past = gather_rows(pool_kv, page_table, layer, jnp.minimum(idx, PAST_LEN - 1)).astype(kv.dtype)
    kv_cur = kv[b_ix, jnp.clip(idx - PAST_LEN, 0, T - 1)]
    kv_sel = jnp.where((idx < PAST_LEN)[..., None, None, None], kv_past, kv_cur)
    o = gathered_attend(q, kv_sel, idx <= q_pos[None, :, None])
    return o.reshape(B, T, A_HEADS * HEAD_DIM) @ w_out, kv, ki


def chunk_gated_delta(q, k, v, g, beta, s0):
    B, T, H, DK = q.shape
    C = B_CHUNK
    n = -(-T // C)
    pad = n * C - T

    def chunks(a):
        a = jnp.pad(a, [(0, 0), (0, pad)] + [(0, 0)] * (a.ndim - 2))
        a = a.reshape(B, n, C, *a.shape[2:])
        return jnp.moveaxis(a, (1, 3), (0, 2))

    qc, kc, vc, bc = chunks(q) * DK ** -0.5, chunks(k), chunks(v), chunks(beta)
    gc = jnp.cumsum(chunks(g), axis=-1)
    incl = jnp.tril(jnp.ones((C, C), bool))
    strict = jnp.tril(jnp.ones((C, C), bool), -1)
    decay = jnp.exp(jnp.where(incl, gc[..., :, None] - gc[..., None, :], NEG))
    kb = kc * bc[..., None]
    a_mat = jnp.where(strict, jnp.einsum("...id,...jd->...ij", kb, kc) * decay, 0.0)
    eye = jnp.eye(C, dtype=jnp.float32)
    t_mat = lax.linalg.triangular_solve(eye + a_mat, jnp.broadcast_to(eye, a_mat.shape), left_side=True, lower=True)
    u = t_mat @ (vc * bc[..., None])
    w = t_mat @ (kb * jnp.exp(gc)[..., None])
    attn = jnp.einsum("...id,...jd->...ij", qc, kc) * decay

    def step(s, inp):
        q_i, k_i, u_i, w_i, g_i, attn_i = inp
        v_new = u_i - w_i @ s
        o = (q_i * jnp.exp(g_i)[..., None]) @ s + attn_i @ v_new
        g_last = g_i[..., -1]
        s = s * jnp.exp(g_last)[..., None, None] + jnp.einsum(
            "bhcd,bhce->bhde", k_i * jnp.exp(g_last[..., None] - g_i)[..., None], v_new)
        return s, o

    s_final, o = lax.scan(step, s0, (qc, kc, u, w, gc, attn))
    o = jnp.moveaxis(o, (0, 2), (1, 3)).reshape(B, n * C, H, -1)[:, :T]
    return o, s_final


def gdn_mixer(h, w_in, conv_w, a_log, dt_bias, norm_g, w_out, conv_state, rec_state):
    B, T, _ = h.shape
    qkv, z, b_raw, a_raw = split_cols(h @ w_in, [B_CONV_CH, B_V_HEADS * B_DV, B_V_HEADS, B_V_HEADS])
    x_ext = jnp.concatenate([conv_state.astype(qkv.dtype), qkv], axis=1)
    conv = lax.conv_general_dilated(x_ext, conv_w[:, None, :].astype(qkv.dtype), window_strides=(1,), padding="VALID",
                                    dimension_numbers=("NWC", "WIO", "NWC"), feature_group_count=B_CONV_CH)
    q, k, v = split_cols(jax.nn.silu(conv), [B_QK_HEADS * B_DK, B_QK_HEADS * B_DK, B_V_HEADS * B_DV])
    rep = B_V_HEADS // B_QK_HEADS
    q = jnp.repeat(l2_normalize(q.reshape(B, T, B_QK_HEADS, B_DK)), rep, axis=2)
    k = jnp.repeat(l2_normalize(k.reshape(B, T, B_QK_HEADS, B_DK)), rep, axis=2)
    v = v.reshape(B, T, B_V_HEADS, B_DV).astype(jnp.float32)
    beta = jax.nn.sigmoid(b_raw.astype(jnp.float32))
    g = -jnp.exp(a_log.astype(jnp.float32)) * jax.nn.softplus(a_raw.astype(jnp.float32) + dt_bias.astype(jnp.float32))
    o, s_new = chunk_gated_delta(q, k, v, g, beta, rec_state.astype(jnp.float32))
    o = rms_norm(o, norm_g.astype(jnp.float32)) * jax.nn.silu(z.reshape(B, T, B_V_HEADS, B_DV).astype(jnp.float32))
    y = o.reshape(B, T, B_V_HEADS * B_DV).astype(h.dtype) @ w_out
    return y, x_ext[:, -(B_CONV - 1):], s_new.astype(rec_state.dtype)


def nsa_project(h, w_in):
    B, T, _ = h.shape
    parts = split_cols(h @ w_in, [C_HEADS * HEAD_DIM] + [C_KV * HEAD_DIM] * 6 + [3 * C_HEADS])
    q = parts[0].reshape(B, T, C_KV, C_REP, HEAD_DIM)
    kvs = [p.reshape(B, T, C_KV, HEAD_DIM) for p in parts[1:7]]
    cmp_kv = jnp.stack(kvs[0:2], axis=2)
    slc_kv = jnp.stack(kvs[2:4], axis=2)
    win_kv = jnp.stack(kvs[4:6], axis=2)
    gate = jax.nn.sigmoid(parts[7].astype(jnp.float32)).reshape(B, T, C_KV, C_REP, 3)
    return q, cmp_kv, slc_kv, win_kv, gate


def compress(kv, pe, w1, w2):
    B, L = kv.shape[:2]
    n_cmp = (L - CMP_BLK) // CMP_STRIDE + 1
    idx = jnp.arange(n_cmp)[:, None] * CMP_STRIDE + jnp.arange(CMP_BLK)[None, :]
    blk = kv[:, idx] + pe[None, None, :, :, None, :]
    blk = jnp.moveaxis(blk, 2, 4).reshape(B, n_cmp, 2, C_KV, CMP_BLK * HEAD_DIM)
    hid = jax.nn.gelu(jnp.einsum("bnegi,eio->bnego", blk, w1))
    return jnp.einsum("bnegi,eio->bnego", hid, w2)


def cmp_attend(q, kv_cmp, q_pos):
    n_cmp = kv_cmp.shape[1]
    logits = jnp.einsum("btgrd,bngd->btgrn", q, kv_cmp[:, :, 0]).astype(jnp.float32) * ATTN_SCALE
    vis = ((jnp.arange(n_cmp) * CMP_STRIDE + CMP_BLK - 1)[None, :] <= q_pos[:, None])[None, :, None, None, :]
    p = jnp.where(vis, masked_softmax(logits, vis), 0.0)
    return jnp.einsum("btgrn,bngd->btgrd", p.astype(q.dtype), kv_cmp[:, :, 1]), p


def select_blocks(p_cmp, q_pos, L):
    n_cmp = p_cmp.shape[-1]
    n_slc = -(-L // SLC_BLK)
    cmp_start = jnp.arange(n_cmp)[:, None] * CMP_STRIDE
    slc_start = jnp.arange(n_slc)[None, :] * SLC_BLK
    overlap = ((cmp_start < slc_start + SLC_BLK) & (cmp_start + CMP_BLK > slc_start)).astype(jnp.float32)
    imp = jnp.einsum("btgrn,nm->btgm", p_cmp, overlap)
    cur = (q_pos // SLC_BLK)[None, :, None, None]
    blk = jnp.arange(n_slc)
    imp = jnp.where((blk == 0) | (blk == cur), FORCE, imp)
    imp = jnp.where(blk <= cur, imp, NEG)
    _, idx = lax.top_k(imp, min(N_SEL, n_slc))
    return idx, idx <= cur


def slc_attend(q, kb, vb, blk_idx, blk_ok, q_pos):
    B, T, G, NS, SB, dh = kb.shape
    key_pos = blk_idx[..., None] * SLC_BLK + jnp.arange(SLC_BLK)
    ok = blk_ok[..., None] & (key_pos <= q_pos[None, :, None, None, None])
    logits = jnp.einsum("btgrd,btgnsd->btgrns", q, kb).astype(jnp.float32) * ATTN_SCALE
    p = masked_softmax(logits.reshape(B, T, G, C_REP, NS * SB), ok.reshape(B, T, G, 1, NS * SB))
    return jnp.einsum("btgrk,btgkd->btgrd", p.astype(q.dtype), vb.reshape(B, T, G, NS * SB, dh))


def win_attend(q, kv, q_pos, k_pos):
    diff = q_pos[:, :, None] - k_pos[:, None, :]
    ok = (k_pos[:, None, :] >= 0) & (diff >= 0) & (diff < WINDOW)
    logits = jnp.einsum("bnqgrd,bnkgd->bnqgrk", q, kv[:, :, :, 0]).astype(jnp.float32) * ATTN_SCALE
    p = masked_softmax(logits, ok[None, :, :, None, None, :])
    return jnp.einsum("bnqgrk,bnkgd->bnqgrd", p.astype(q.dtype), kv[:, :, :, 1])


def to_blocks(k):
    B, L = k.shape[:2]
    n = -(-L // SLC_BLK)
    k = jnp.pad(k, ((0, 0), (0, n * SLC_BLK - L), (0, 0), (0, 0)))
    return k.reshape(B, n, SLC_BLK, C_KV, HEAD_DIM).transpose(0, 3, 1, 2, 4)


def nsa_out(gate, o_cmp, o_slc, o_win, w_out, dtype):
    o = gate[..., 0:1] * o_cmp + gate[..., 1:2] * o_slc + gate[..., 2:3] * o_win
    B, T = o.shape[:2]
    return o.reshape(B, T, C_HEADS * HEAD_DIM).astype(dtype) @ w_out


def nsa_prompt(h, w_in, pe, w1, w2, w_out):
    B, T, _ = h.shape
    q, cmp_kv, slc_kv, win_kv, gate = nsa_project(h, w_in)
    q_pos = jnp.arange(T)
    o_cmp, p_cmp = cmp_attend(q, compress(cmp_kv, pe, w1, w2), q_pos)
    idx, ok = select_blocks(p_cmp, q_pos, T)
    kb, vb = to_blocks(slc_kv[:, :, 0]), to_blocks(slc_kv[:, :, 1])
    b_ix = jnp.arange(B)[:, None, None, None]
    g_ix = jnp.arange(C_KV)[None, None, :, None]

    def block(start):
        sl = lambda a: lax.dynamic_slice_in_dim(a, start, C_QBLK, axis=1)
        ib = sl(idx)
        return slc_attend(sl(q), kb[b_ix, g_ix, ib], vb[b_ix, g_ix, ib], ib, sl(ok), start + jnp.arange(C_QBLK))

    o_slc = jnp.moveaxis(lax.map(block, jnp.arange(0, T, C_QBLK)), 0, 1).reshape(q.shape)
    nb = T // WIN_QBLK
    kv_pad = jnp.pad(win_kv, ((0, 0), (WINDOW, 0), (0, 0), (0, 0), (0, 0)))
    kidx = jnp.arange(nb)[:, None] * WIN_QBLK + jnp.arange(WINDOW + WIN_QBLK)[None, :]
    o_win = win_attend(q.reshape(B, nb, WIN_QBLK, C_KV, C_REP, HEAD_DIM), kv_pad[:, kidx],
                       q_pos.reshape(nb, WIN_QBLK), kidx - WINDOW).reshape(q.shape)
    y = nsa_out(gate, o_cmp, o_slc, o_win, w_out, h.dtype)
    return y, cmp_kv, slc_kv, kv_pad[:, -WINDOW:]


def nsa_sample(h, w_in, pe, w1, w2, w_out, pool_cmp, pool_slc, win_buf, page_table, layer):
    B, T, _ = h.shape
    q, cmp_kv, slc_kv, win_kv, gate = nsa_project(h, w_in)
    L = PAST_LEN + T
    q_pos = PAST_LEN + jnp.arange(T)
    cmp_all = jnp.concatenate([gather_pages(pool_cmp, page_table, layer).astype(cmp_kv.dtype), cmp_kv], axis=1)
    o_cmp, p_cmp = cmp_attend(q, compress(cmp_all, pe, w1, w2), q_pos)
    idx, ok = select_blocks(p_cmp, q_pos, L)
    past_blk = PAST_LEN // SLC_BLK
    is_past = (idx < past_blk)[..., None, None]
    pos = jnp.minimum(idx, past_blk - 1)[..., None] * SLC_BLK + jnp.arange(SLC_BLK)
    new_idx = jnp.clip(idx - past_blk, 0, -(-T // SLC_BLK) - 1)
    b_ix = jnp.arange(B)[:, None, None, None]
    g_ix = jnp.arange(C_KV)[None, None, :, None]
    kb = jnp.where(is_past, gather_group_rows(pool_slc, page_table, layer, pos, 0).astype(slc_kv.dtype),
                   to_blocks(slc_kv[:, :, 0])[b_ix, g_ix, new_idx])
    vb = jnp.where(is_past, gather_group_rows(pool_slc, page_table, layer, pos, 1).astype(slc_kv.dtype),
                   to_blocks(slc_kv[:, :, 1])[b_ix, g_ix, new_idx])
    o_slc = slc_attend(q, kb, vb, idx, ok, q_pos)
    kv_all = jnp.concatenate([win_buf.astype(win_kv.dtype), win_kv], axis=1)
    k_pos = PAST_LEN - WINDOW + jnp.arange(WINDOW + T)
    o_win = win_attend(q[:, None], kv_all[:, None], q_pos[None], k_pos[None])[:, 0]
    y = nsa_out(gate, o_cmp, o_slc, o_win, w_out, h.dtype)
    return y, cmp_kv, slc_kv, kv_all[:, -WINDOW:]


def moe(h, layer, w_router, b_router, w_in, b_in, w_out, b_out):
    x = h.reshape(-1, D_MODEL)
    n_tok = x.shape[0]
    n_asg = n_tok * TOP_K
    logits = (x @ w_router[layer] + b_router[layer]).astype(jnp.float32)
    top_val, top_idx = lax.top_k(logits, TOP_K)
    gate = jax.nn.softmax(top_val, axis=-1).reshape(-1)
    e_flat = top_idx.reshape(-1)
    order = jnp.argsort(e_flat)
    e_sorted = e_flat[order]
    tok_sorted = order // TOP_K
    counts = jnp.bincount(e_flat, length=N_EXPERTS)
    blocks = (counts + MOE_BLK - 1) // MOE_BLK
    blk_end = jnp.cumsum(blocks)
    dest = (blk_end - blocks)[e_sorted] * MOE_BLK + jnp.arange(n_asg) - (jnp.cumsum(counts) - counts)[e_sorted]
    n_blk = n_asg // MOE_BLK + N_EXPERTS
    xs = jnp.zeros((n_blk * MOE_BLK, D_MODEL), x.dtype).at[dest].set(x[tok_sorted])
    blk_expert = jnp.minimum(jnp.searchsorted(blk_end, jnp.arange(n_blk), side="right"), N_EXPERTS - 1)

    def expert_block(args):
        xb, e = args
        hg = xb @ w_in[layer, e] + b_in[layer, e]
        glu = jnp.minimum(hg[:, :D_FF], SWIGLU_LIMIT)
        lin = jnp.clip(hg[:, D_FF:], -SWIGLU_LIMIT, SWIGLU_LIMIT)
        return (glu * jax.nn.sigmoid(SWIGLU_ALPHA * glu) * (lin + 1.0)) @ w_out[layer, e] + b_out[layer, e]

    ys = lax.map(expert_block, (xs.reshape(n_blk, MOE_BLK, D_MODEL), blk_expert)).reshape(-1, D_MODEL)
    y = jax.ops.segment_sum(ys[dest] * gate[order][:, None].astype(ys.dtype), tok_sorted, num_segments=n_tok)
    return y.reshape(h.shape)


def setup_inputs(seed: int = 0) -> dict:
    key = jax.random.key(seed)
    keys = iter(jax.random.split(key, 40))

    def nrm(shape, scale):
        return jax.random.normal(next(keys), shape, jnp.float32) * scale

    n_pages = PAST_LEN // PAGE_SIZE
    used = DEC_BATCH * n_pages
    n_pool = used + max(1, used // 4)
    page_table = jax.random.permutation(next(keys), n_pool)[:used].reshape(DEC_BATCH, n_pages).astype(jnp.int32)
    dt = jnp.exp(jax.random.uniform(next(keys), (N_LAYERS_B, B_V_HEADS), jnp.float32, math.log(1e-3), math.log(1e-1)))
    a_log = jnp.log(jax.random.uniform(next(keys), (N_LAYERS_B, B_V_HEADS), jnp.float32, 1.0, 16.0))
    return {
        "x_prompt": nrm((BATCH, SEQ, D_MODEL), 1.0),
        "x_sample": nrm((DEC_BATCH, DEC_SEQ, D_MODEL), 1.0),
        "cache_dsa_kv": nrm((n_pool, PAGE_SIZE, N_LAYERS_A, 2, A_KV, HEAD_DIM), 1.0),
        "cache_dsa_idx": nrm((n_pool, PAGE_SIZE, N_LAYERS_A, A_IDX_DIM), 1.0),
        "state_gdn_conv": nrm((DEC_BATCH, N_LAYERS_B, B_CONV - 1, B_CONV_CH), 1.0),
        "state_gdn_rec": nrm((DEC_BATCH, N_LAYERS_B, B_V_HEADS, B_DK, B_DV), 0.05),
        "cache_nsa_cmp": nrm((n_pool, PAGE_SIZE, N_LAYERS_C, 2, C_KV, HEAD_DIM), 1.0),
        "cache_nsa_slc": nrm((n_pool, PAGE_SIZE, N_LAYERS_C, 2, C_KV, HEAD_DIM), 1.0),
        "cache_nsa_win": nrm((DEC_BATCH, N_LAYERS_C, WINDOW, 2, C_KV, HEAD_DIM), 1.0),
        "page_table": page_table,
        "c_prompt": nrm((BATCH, D_MODEL), 1.0),
        "c_sample": nrm((DEC_BATCH, D_MODEL), 1.0),
        "ada_w": nrm((DEPTH, D_MODEL, 6 * D_MODEL), 0.01),
        "ada_b": nrm((DEPTH, 6 * D_MODEL), 0.02),
        "ln_g": 1.0 + nrm((DEPTH, 2, D_MODEL), 0.01),
        "ln_b": nrm((DEPTH, 2, D_MODEL), 0.01),
        "dsa_w_in": nrm((N_LAYERS_A, D_MODEL, A_COLS), D_MODEL ** -0.5),
        "dsa_w_out": nrm((N_LAYERS_A, A_HEADS * HEAD_DIM, D_MODEL), BETA_INIT * (A_HEADS * HEAD_DIM) ** -0.5),
        "gdn_w_in": nrm((N_LAYERS_B, D_MODEL, B_COLS), D_MODEL ** -0.5),
        "gdn_conv_w": nrm((N_LAYERS_B, B_CONV, B_CONV_CH), B_CONV ** -0.5),
        "gdn_a_log": a_log,
        "gdn_dt_bias": dt + jnp.log(-jnp.expm1(-dt)),
        "gdn_norm_g": 1.0 + nrm((N_LAYERS_B, B_DV), 0.01),
        "gdn_w_out": nrm((N_LAYERS_B, B_V_HEADS * B_DV, D_MODEL), BETA_INIT * (B_V_HEADS * B_DV) ** -0.5),
        "nsa_w_in": nrm((N_LAYERS_C, D_MODEL, C_COLS), D_MODEL ** -0.5),
        "nsa_cmp_pe": nrm((N_LAYERS_C, CMP_BLK, 2, HEAD_DIM), 0.1),
        "nsa_cmp_w1": nrm((N_LAYERS_C, 2, CMP_BLK * HEAD_DIM, HEAD_DIM), (CMP_BLK * HEAD_DIM) ** -0.5),
        "nsa_cmp_w2": nrm((N_LAYERS_C, 2, HEAD_DIM, HEAD_DIM), HEAD_DIM ** -0.5),
        "nsa_w_out": nrm((N_LAYERS_C, C_HEADS * HEAD_DIM, D_MODEL), BETA_INIT * (C_HEADS * HEAD_DIM) ** -0.5),
        "moe_w_router": nrm((DEPTH, D_MODEL, N_EXPERTS), D_MODEL ** -0.5),
        "moe_b_router": nrm((DEPTH, N_EXPERTS), 0.01),
        "moe_w_in": nrm((DEPTH, N_EXPERTS, D_MODEL, 2 * D_FF), D_MODEL ** -0.5),
        "moe_b_in": nrm((DEPTH, N_EXPERTS, 2 * D_FF), 0.01),
        "moe_w_out": nrm((DEPTH, N_EXPERTS, D_FF, D_MODEL), BETA_INIT * D_FF ** -0.5),
        "moe_b_out": nrm((DEPTH, N_EXPERTS, D_MODEL), 0.01),
    }


def reference(x_prompt, x_sample, cache_dsa_kv, cache_dsa_idx, state_gdn_conv, state_gdn_rec,
              cache_nsa_cmp, cache_nsa_slc, cache_nsa_win, page_table, c_prompt, c_sample,
              ada_w, ada_b, ln_g, ln_b, dsa_w_in, dsa_w_out, gdn_w_in, gdn_conv_w, gdn_a_log, gdn_dt_bias,
              gdn_norm_g, gdn_w_out, nsa_w_in, nsa_cmp_pe, nsa_cmp_w1, nsa_cmp_w2, nsa_w_out,
              moe_w_router, moe_b_router, moe_w_in, moe_b_in, moe_w_out, moe_b_out):
    xp, xs = x_prompt, x_sample
    bp = xp.shape[0]
    dsa_kv_p, dsa_kv_s, dsa_idx_p, dsa_idx_s = [], [], [], []
    gdn_conv_p, gdn_conv_s, gdn_rec_p, gdn_rec_s = [], [], [], []
    nsa_cmp_p, nsa_cmp_s, nsa_slc_p, nsa_slc_s, nsa_win_p, nsa_win_s = [], [], [], [], [], []
    moe_w = (moe_w_router, moe_b_router, moe_w_in, moe_b_in, moe_w_out, moe_b_out)
    for i in range(DEPTH):
        kind, j = i % N_MIXERS, i // N_MIXERS
        mp = modulation(c_prompt, ada_w[i], ada_b[i])
        ms = modulation(c_sample, ada_w[i], ada_b[i])
        hp, hs = modulate(xp, mp[:, 0], mp[:, 1]), modulate(xs, ms[:, 0], ms[:, 1])
        if kind == 0:
            yp, kv_p, ki_p = dsa_prompt(hp, dsa_w_in[j], dsa_w_out[j])
            ys, kv_s, ki_s = dsa_sample(hs, dsa_w_in[j], dsa_w_out[j], cache_dsa_kv, cache_dsa_idx, page_table, j)
            dsa_kv_p.append(kv_p)
            dsa_kv_s.append(kv_s)
            dsa_idx_p.append(ki_p)
            dsa_idx_s.append(ki_s)
        elif kind == 1:
            gdn_args = (gdn_w_in[j], gdn_conv_w[j], gdn_a_log[j], gdn_dt_bias[j], gdn_norm_g[j], gdn_w_out[j])
            zero_conv = jnp.zeros((bp, B_CONV - 1, B_CONV_CH), xp.dtype)
            zero_rec = jnp.zeros((bp, B_V_HEADS, B_DK, B_DV), xp.dtype)
            yp, conv_p, rec_p = gdn_mixer(hp, *gdn_args, zero_conv, zero_rec)
            ys, conv_s, rec_s = gdn_mixer(hs, *gdn_args, state_gdn_conv[:, j], state_gdn_rec[:, j])
            gdn_conv_p.append(conv_p)
            gdn_conv_s.append(conv_s)
            gdn_rec_p.append(rec_p)
            gdn_rec_s.append(rec_s)
        else:
            nsa_args = (nsa_w_in[j], nsa_cmp_pe[j], nsa_cmp_w1[j], nsa_cmp_w2[j], nsa_w_out[j])
            yp, cmp_p, slc_p, win_p = nsa_prompt(hp, *nsa_args)
            ys, cmp_s, slc_s, win_s = nsa_sample(hs, *nsa_args, cache_nsa_cmp, cache_nsa_slc, cache_nsa_win[:, j],
                                                 page_table, j)
            nsa_cmp_p.append(cmp_p)
            nsa_cmp_s.append(cmp_s)
            nsa_slc_p.append(slc_p)
            nsa_slc_s.append(slc_s)
            nsa_win_p.append(win_p)
            nsa_win_s.append(win_s)
        xp = post_norm(xp, yp, mp[:, 2], ln_g[i, 0], ln_b[i, 0])
        xs = post_norm(xs, ys, ms[:, 2], ln_g[i, 0], ln_b[i, 0])
        hp, hs = modulate(xp, mp[:, 3], mp[:, 4]), modulate(xs, ms[:, 3], ms[:, 4])
        xp = post_norm(xp, moe(hp, i, *moe_w), mp[:, 5], ln_g[i, 1], ln_b[i, 1])
        xs = post_norm(xs, moe(hs, i, *moe_w), ms[:, 5], ln_g[i, 1], ln_b[i, 1])
    return (xp, xs,
            jnp.stack(dsa_kv_p, axis=2), jnp.stack(dsa_kv_s, axis=2),
            jnp.stack(dsa_idx_p, axis=2), jnp.stack(dsa_idx_s, axis=2),
            jnp.stack(gdn_conv_p, axis=1), jnp.stack(gdn_conv_s, axis=1),
            jnp.stack(gdn_rec_p, axis=1), jnp.stack(gdn_rec_s, axis=1),
            jnp.stack(nsa_cmp_p, axis=2), jnp.stack(nsa_cmp_s, axis=2),
            jnp.stack(nsa_slc_p, axis=2), jnp.stack(nsa_slc_s, axis=2),
            jnp.stack(nsa_win_p, axis=1), jnp.stack(nsa_win_s, axis=1))
```

```python
import functools
import math

import jax
import jax.numpy as jnp
import numpy as np
from jax import lax
from jax.experimental import pallas as pl
from jax.experimental.pallas import tpu as pltpu

D_MODEL = 2048
DEPTH = 4
PAST_LEN = 16384
PAGE_SIZE = 128
N_MIXERS = 3
ALPHA = (2.0 * DEPTH) ** 0.25
LN_EPS = 1e-5
NORM_EPS = 1e-6
NEG = -1e30
FORCE = 1e9
HEAD_DIM = 128
ATTN_SCALE = HEAD_DIM ** -0.5

A_HEADS = D_MODEL // HEAD_DIM
A_KV = 4
A_REP = A_HEADS // A_KV
A_IDX_HEADS = 16
A_IDX_DIM = 64
A_TOPK = 256
A_QBLK = 128

B_QK_HEADS = 16
B_V_HEADS = 32
B_DK = 128
B_DV = 128
B_CONV = 4
B_CHUNK = 64
B_CONV_CH = 2 * B_QK_HEADS * B_DK + B_V_HEADS * B_DV

C_HEADS = D_MODEL // HEAD_DIM
C_KV = 4
C_REP = C_HEADS // C_KV
CMP_BLK = 32
CMP_STRIDE = 16
SLC_BLK = 64
N_SEL = 16
WINDOW = 512
C_QBLK = 32
WIN_QBLK = 128

N_EXPERTS = 32
TOP_K = 4
D_FF = 2048
SWIGLU_LIMIT = 7.0
SWIGLU_ALPHA = 1.702

VMEM_LIMIT_BYTES = 56 * 1024 * 1024
MOE_ROW_BLK = 256
MOE_FF_TILE = 512
MOE_OUT_TILE = 512


def _params(*sem):
    return pltpu.CompilerParams(dimension_semantics=sem, vmem_limit_bytes=VMEM_LIMIT_BYTES)


def _bf16(x):
    return x.astype(jnp.bfloat16)


def _mm_kernel(x_ref, w_ref, o_ref):
    o_ref[...] = jnp.dot(_bf16(x_ref[...]), _bf16(w_ref[...]), preferred_element_type=jnp.float32)


def _mm_bias_kernel(x_ref, w_ref, b_ref, o_ref):
    o_ref[...] = jnp.dot(_bf16(x_ref[...]), _bf16(w_ref[...]), preferred_element_type=jnp.float32) + b_ref[...]


def _mm_mod_kernel(x_ref, sc_ref, sh_ref, w_ref, o_ref):
    h = x_ref[0] * (1.0 + sc_ref[0]) + sh_ref[0]
    o_ref[0] = jnp.dot(_bf16(h), _bf16(w_ref[...]), preferred_element_type=jnp.float32)


def matmul(x, w, layer, b=None, tm=512, tn=512):
    m, k = x.shape
    n = w.shape[2]
    tm = min(tm, m)
    tn = min(tn, n)
    grid = (pl.cdiv(m, tm), pl.cdiv(n, tn))
    in_specs = [pl.BlockSpec((tm, k), lambda i, j: (i, 0)), pl.BlockSpec((None, k, tn), lambda i, j: (layer, 0, j))]
    args = [x, w]
    body = _mm_kernel
    if b is not None:
        in_specs.append(pl.BlockSpec((None, 1, tn), lambda i, j: (layer, 0, j)))
        args.append(b.reshape(b.shape[0], 1, n))
        body = _mm_bias_kernel
    return pl.pallas_call(
        body, grid=grid, in_specs=in_specs,
        out_specs=pl.BlockSpec((tm, tn), lambda i, j: (i, j)),
        out_shape=jax.ShapeDtypeStruct((m, n), jnp.float32),
        compiler_params=_params("parallel", "parallel"))(*args)


def matmul_modulated(x, scale, shift, w, layer, tm=512, tn=512):
    bsz, t, k = x.shape
    n = w.shape[2]
    tm = min(tm, t)
    tn = min(tn, n)
    grid = (bsz, pl.cdiv(t, tm), pl.cdiv(n, tn))
    mod_spec = pl.BlockSpec((1, 1, k), lambda b, i, j: (b, 0, 0))
    return pl.pallas_call(
        _mm_mod_kernel, grid=grid,
        in_specs=[pl.BlockSpec((1, tm, k), lambda b, i, j: (b, i, 0)), mod_spec, mod_spec,
                  pl.BlockSpec((None, k, tn), lambda b, i, j: (layer, 0, j))],
        out_specs=pl.BlockSpec((1, tm, tn), lambda b, i, j: (b, i, j)),
        out_shape=jax.ShapeDtypeStruct((bsz, t, n), jnp.float32),
        compiler_params=_params("parallel", "parallel", "parallel"))(
            x, scale.reshape(bsz, 1, k), shift.reshape(bsz, 1, k), w)


def _postnorm_kernel(x_ref, y_ref, gate_ref, g_ref, b_ref, sc_ref, sh_ref, xo_ref, ho_ref):
    z = ALPHA * x_ref[0] + gate_ref[0] * y_ref[0]
    mu = jnp.mean(z, axis=-1, keepdims=True)
    zc = z - mu
    var = jnp.mean(zc * zc, axis=-1, keepdims=True)
    xn = zc * lax.rsqrt(var + LN_EPS) * g_ref[...] + b_ref[...]
    xo_ref[0] = xn
    ho_ref[0] = xn * (1.0 + sc_ref[0]) + sh_ref[0]


def post_norm_modulate(x, y, gate, g, b, scale, shift, tm=256):
    bsz, t, d = x.shape
    tm = min(tm, t)
    row = pl.BlockSpec((1, tm, d), lambda bi, i: (bi, i, 0))
    per_seq = pl.BlockSpec((1, 1, d), lambda bi, i: (bi, 0, 0))
    shared = pl.BlockSpec((1, d), lambda bi, i: (0, 0))
    return pl.pallas_call(
        _postnorm_kernel, grid=(bsz, t // tm),
        in_specs=[row, row, per_seq, shared, shared, per_seq, per_seq],
        out_specs=[row, row],
        out_shape=[jax.ShapeDtypeStruct(x.shape, jnp.float32)] * 2,
        compiler_params=_params("parallel", "parallel"))(
            x, y, gate.reshape(bsz, 1, d), g.reshape(1, d), b.reshape(1, d),
            scale.reshape(bsz, 1, d), shift.reshape(bsz, 1, d))


def _moe_in_kernel(blk_e_ref, x_ref, wg_ref, wl_ref, bg_ref, bl_ref, o_ref):
    x = x_ref[...]
    glu = jnp.dot(x, _bf16(wg_ref[...]), preferred_element_type=jnp.float32) + bg_ref[...]
    lin = jnp.dot(x, _bf16(wl_ref[...]), preferred_element_type=jnp.float32) + bl_ref[...]
    glu = jnp.minimum(glu, SWIGLU_LIMIT)
    lin = jnp.clip(lin, -SWIGLU_LIMIT, SWIGLU_LIMIT)
    o_ref[...] = (glu * jax.nn.sigmoid(SWIGLU_ALPHA * glu) * (lin + 1.0)).astype(o_ref.dtype)


def _moe_out_kernel(blk_e_ref, h_ref, w_ref, b_ref, o_ref):
    o_ref[...] = jnp.dot(h_ref[...], _bf16(w_ref[...]), preferred_element_type=jnp.float32) + b_ref[...]


def moe_ffn(xs, blk_expert, layer, w_in, b_in, w_out, b_out):
    r = xs.shape[0]
    n_blk = r // MOE_ROW_BLK
    n_ff = D_FF // MOE_FF_TILE
    tm, tf, tn = MOE_ROW_BLK, MOE_FF_TILE, MOE_OUT_TILE
    b_in4 = b_in.reshape(DEPTH, N_EXPERTS, 1, 2 * D_FF)
    b_out4 = b_out.reshape(DEPTH, N_EXPERTS, 1, D_MODEL)
    act = pl.pallas_call(
        _moe_in_kernel,
        grid_spec=pltpu.PrefetchScalarGridSpec(
            num_scalar_prefetch=1, grid=(n_ff, n_blk),
            in_specs=[
                pl.BlockSpec((tm, D_MODEL), lambda j, m, be: (m, 0)),
                pl.BlockSpec((None, None, D_MODEL, tf), lambda j, m, be: (layer, be[m], 0, j)),
                pl.BlockSpec((None, None, D_MODEL, tf), lambda j, m, be: (layer, be[m], 0, n_ff + j)),
                pl.BlockSpec((None, None, 1, tf), lambda j, m, be: (layer, be[m], 0, j)),
                pl.BlockSpec((None, None, 1, tf), lambda j, m, be: (layer, be[m], 0, n_ff + j)),
            ],
            out_specs=pl.BlockSpec((tm, tf), lambda j, m, be: (m, j))),
        out_shape=jax.ShapeDtypeStruct((r, D_FF), jnp.bfloat16),
        compiler_params=_params("arbitrary", "arbitrary"))(blk_expert, xs, w_in, w_in, b_in4, b_in4)
    return pl.pallas_call(
        _moe_out_kernel,
        grid_spec=pltpu.PrefetchScalarGridSpec(
            num_scalar_prefetch=1, grid=(D_MODEL // tn, n_blk),
            in_specs=[
                pl.BlockSpec((tm, D_FF), lambda j, m, be: (m, 0)),
                pl.BlockSpec((None, None, D_FF, tn), lambda j, m, be: (layer, be[m], 0, j)),
                pl.BlockSpec((None, None, 1, tn), lambda j, m, be: (layer, be[m], 0, j)),
            ],
            out_specs=pl.BlockSpec((tm, tn), lambda j, m, be: (m, j))),
        out_shape=jax.ShapeDtypeStruct((r, D_MODEL), jnp.float32),
        compiler_params=_params("arbitrary", "arbitrary"))(blk_expert, act, w_out, b_out4)


def moe(x, layer, w_router, b_router, w_in, b_in, w_out, b_out):
    n_tok = x.shape[0]
    n_asg = n_tok * TOP_K
    logits = jnp.dot(x, w_router[layer], precision=lax.Precision.HIGHEST) + b_router[layer]
    top_val, top_idx = lax.top_k(logits, TOP_K)
    gate = jax.nn.softmax(top_val, axis=-1)
    e_flat = top_idx.reshape(-1)
    order = jnp.argsort(e_flat)
    e_sorted = e_flat[order]
    counts = jnp.bincount(e_flat, length=N_EXPERTS)
    blocks = (counts + MOE_ROW_BLK - 1) // MOE_ROW_BLK
    blk_end = jnp.cumsum(blocks)
    dest = (blk_end - blocks)[e_sorted] * MOE_ROW_BLK + jnp.arange(n_asg) - (jnp.cumsum(counts) - counts)[e_sorted]
    n_blk = -(-n_asg // MOE_ROW_BLK) + N_EXPERTS
    row_tok = jnp.zeros((n_blk * MOE_ROW_BLK,), jnp.int32).at[dest].set((order // TOP_K).astype(jnp.int32))
    blk_expert = jnp.minimum(jnp.searchsorted(blk_end, jnp.arange(n_blk), side="right"), N_EXPERTS - 1).astype(jnp.int32)
    xs = _bf16(x)[row_tok]
    ys = moe_ffn(xs, blk_expert, layer, w_in, b_in, w_out, b_out)
    dest_of_asg = jnp.zeros((n_asg,), jnp.int32).at[order].set(dest.astype(jnp.int32))
    return jnp.sum(ys[dest_of_asg].reshape(n_tok, TOP_K, D_MODEL) * gate[..., None], axis=1)


def split_cols(z, sizes):
    return jnp.split(z, np.cumsum(sizes)[:-1].tolist(), axis=-1)


def rms_norm(x, g):
    return x * lax.rsqrt(jnp.mean(jnp.square(x), axis=-1, keepdims=True) + NORM_EPS) * g


def l2_normalize(x):
    return x * lax.rsqrt(jnp.sum(jnp.square(x), axis=-1, keepdims=True) + NORM_EPS)


def masked_softmax(logits, mask):
    return jax.nn.softmax(jnp.where(mask, logits, NEG), axis=-1)


def gather_pages(pool, page_table, layer):
    rows = pool[page_table, :, layer]
    return rows.reshape(rows.shape[0], -1, *rows.shape[3:])


def gather_rows(pool, page_table, layer, pos):
    b = jnp.arange(pos.shape[0]).reshape((-1,) + (1,) * (pos.ndim - 1))
    phys = page_table[b, pos // PAGE_SIZE]
    return pool[phys, pos % PAGE_SIZE, layer]


def gather_group_rows(pool, page_table, layer, pos, kv_j):
    b = jnp.arange(pos.shape[0])[:, None, None, None, None]
    g = jnp.arange(C_KV)[None, None, :, None, None]
    phys = page_table[b, pos // PAGE_SIZE]
    return pool[phys, pos % PAGE_SIZE, layer, kv_j, g]


def dsa_split(z):
    B, T, _ = z.shape
    q, k, v, qi, ki, wi = split_cols(z, [A_HEADS * HEAD_DIM, A_KV * HEAD_DIM, A_KV * HEAD_DIM,
                                         A_IDX_HEADS * A_IDX_DIM, A_IDX_DIM, A_IDX_HEADS])
    q = q.reshape(B, T, A_KV, A_REP, HEAD_DIM)
    kv = jnp.stack([k.reshape(B, T, A_KV, HEAD_DIM), v.reshape(B, T, A_KV, HEAD_DIM)], axis=2)
    qi = qi.reshape(B, T, A_IDX_HEADS, A_IDX_DIM)
    return q, kv, qi, ki, wi * A_IDX_HEADS ** -0.5


def indexer_scores(qi, wi, ki, q_pos, k_pos):
    s = jax.nn.relu(jnp.einsum("bthd,bsd->bths", qi, ki).astype(jnp.float32) * A_IDX_DIM ** -0.5)
    score = jnp.einsum("bths,bth->bts", s, wi.astype(jnp.float32))
    return jnp.where(k_pos[None, None, :] <= q_pos[None, :, None], score, NEG)


def gathered_attend(q, kv_sel, ok):
    logits = jnp.einsum("btgrd,btkgd->btgrk", q, kv_sel[:, :, :, 0]).astype(jnp.float32) * ATTN_SCALE
    p = masked_softmax(logits, ok[:, :, None, None, :])
    return jnp.einsum("btgrk,btkgd->btgrd", p.astype(q.dtype), kv_sel[:, :, :, 1])


def dsa_prompt(z):
    B, T, _ = z.shape
    q, kv, qi, ki, wi = dsa_split(z)
    topk = min(A_TOPK, T // 4)
    k_pos = jnp.arange(T)
    b_ix = jnp.arange(B)[:, None, None]

    def block(start):
        sl = lambda a: lax.dynamic_slice_in_dim(a, start, A_QBLK, axis=1)
        q_pos = start + jnp.arange(A_QBLK)
        _, idx = lax.top_k(indexer_scores(sl(qi), sl(wi), ki, q_pos, k_pos), topk)
        return gathered_attend(sl(q), kv[b_ix, idx], idx <= q_pos[None, :, None])

    o = lax.map(block, jnp.arange(0, T, A_QBLK))
    o = jnp.moveaxis(o, 0, 1).reshape(B, T, A_HEADS * HEAD_DIM)
    return o, kv, ki


def dsa_sample(z, pool_kv, pool_idx, page_table, layer):
    B, T, _ = z.shape
    q, kv, qi, ki, wi = dsa_split(z)
    L = PAST_LEN + T
    topk = min(A_TOPK, L // 4)
    q_pos = PAST_LEN + jnp.arange(T)
    ki_all = jnp.concatenate([gather_pages(pool_idx, page_table, layer).astype(ki.dtype), ki], axis=1)
    _, idx = lax.top_k(indexer_scores(qi, wi, ki_all, q_pos, jnp.arange(L)), topk)
    b_ix = jnp.arange(B)[:, None, None]
    kv_past = gather_rows(pool_kv, page_table, layer, jnp.minimum(idx, PAST_LEN - 1)).astype(kv.dtype)
    kv_cur = kv[b_ix, jnp.clip(idx - PAST_LEN, 0, T - 1)]
    kv_sel = jnp.where((idx < PAST_LEN)[..., None, None, None], kv_past, kv_cur)
    o = gathered_attend(q, kv_sel, idx <= q_pos[None, :, None])
    return o.reshape(B, T, A_HEADS * HEAD_DIM), kv, ki


def chunk_gated_delta(q, k, v, g, beta, s0):
    B, T, H, DK = q.shape
    C = B_CHUNK
    n = -(-T // C)
    pad = n * C - T

    def chunks(a):
        a = jnp.pad(a, [(0, 0), (0, pad)] + [(0, 0)] * (a.ndim - 2))
        a = a.reshape(B, n, C, *a.shape[2:])
        return jnp.moveaxis(a, (1, 3), (0, 2))

    qc, kc, vc, bc = chunks(q) * DK ** -0.5, chunks(k), chunks(v), chunks(beta)
    gc = jnp.cumsum(chunks(g), axis=-1)
    incl = jnp.tril(jnp.ones((C, C), bool))
    strict = jnp.tril(jnp.ones((C, C), bool), -1)
    decay = jnp.exp(jnp.where(incl, gc[..., :, None] - gc[..., None, :], NEG))
    kb = kc * bc[..., None]
    a_mat = jnp.where(strict, jnp.einsum("...id,...jd->...ij", kb, kc) * decay, 0.0)
    eye = jnp.eye(C, dtype=jnp.float32)
    t_mat = lax.linalg.triangular_solve(eye + a_mat, jnp.broadcast_to(eye, a_mat.shape), left_side=True, lower=True)
    u = t_mat @ (vc * bc[..., None])
    w = t_mat @ (kb * jnp.exp(gc)[..., None])
    attn = jnp.einsum("...id,...jd->...ij", qc, kc) * decay

    def step(s, inp):
        q_i, k_i, u_i, w_i, g_i, attn_i = inp
        v_new = u_i - w_i @ s
        o = (q_i * jnp.exp(g_i)[..., None]) @ s + attn_i @ v_new
        g_last = g_i[..., -1]
        s = s * jnp.exp(g_last)[..., None, None] + jnp.einsum(
            "bhcd,bhce->bhde", k_i * jnp.exp(g_last[..., None] - g_i)[..., None], v_new)
        return s, o

    s_final, o = lax.scan(step, s0, (qc, kc, u, w, gc, attn))
    o = jnp.moveaxis(o, (0, 2), (1, 3)).reshape(B, n * C, H, -1)[:, :T]
    return o, s_final


def gdn_mixer(zin, conv_w, a_log, dt_bias, norm_g, conv_state, rec_state):
    B, T, _ = zin.shape
    qkv, z, b_raw, a_raw = split_cols(zin, [B_CONV_CH, B_V_HEADS * B_DV, B_V_HEADS, B_V_HEADS])
    x_ext = jnp.concatenate([conv_state, qkv], axis=1)
    conv = sum(x_ext[:, i:i + T] * conv_w[i][None, None, :] for i in range(B_CONV))
    q, k, v = split_cols(jax.nn.silu(conv), [B_QK_HEADS * B_DK, B_QK_HEADS * B_DK, B_V_HEADS * B_DV])
    rep = B_V_HEADS // B_QK_HEADS
    q = jnp.repeat(l2_normalize(q.reshape(B, T, B_QK_HEADS, B_DK)), rep, axis=2)
    k = jnp.repeat(l2_normalize(k.reshape(B, T, B_QK_HEADS, B_DK)), rep, axis=2)
    v = v.reshape(B, T, B_V_HEADS, B_DV)
    beta = jax.nn.sigmoid(b_raw)
    g = -jnp.exp(a_log) * jax.nn.softplus(a_raw + dt_bias)
    o, s_new = chunk_gated_delta(q, k, v, g, beta, rec_state)
    o = rms_norm(o, norm_g) * jax.nn.silu(z.reshape(B, T, B_V_HEADS, B_DV))
    return o.reshape(B, T, B_V_HEADS * B_DV), x_ext[:, -(B_CONV - 1):], s_new


def nsa_split(z):
    B, T, _ = z.shape
    parts = split_cols(z, [C_HEADS * HEAD_DIM] + [C_KV * HEAD_DIM] * 6 + [3 * C_HEADS])
    q = parts[0].reshape(B, T, C_KV, C_REP, HEAD_DIM)
    kvs = [p.reshape(B, T, C_KV, HEAD_DIM) for p in parts[1:7]]
    cmp_kv = jnp.stack(kvs[0:2], axis=2)
    slc_kv = jnp.stack(kvs[2:4], axis=2)
    win_kv = jnp.stack(kvs[4:6], axis=2)
    gate = jax.nn.sigmoid(parts[7]).reshape(B, T, C_KV, C_REP, 3)
    return q, cmp_kv, slc_kv, win_kv, gate


def compress(kv, pe, w1, w2):
    B, L = kv.shape[:2]
    n_cmp = (L - CMP_BLK) // CMP_STRIDE + 1
    idx = jnp.arange(n_cmp)[:, None] * CMP_STRIDE + jnp.arange(CMP_BLK)[None, :]
    blk = kv[:, idx] + pe[None, None, :, :, None, :]
    blk = jnp.moveaxis(blk, 2, 4).reshape(B, n_cmp, 2, C_KV, CMP_BLK * HEAD_DIM)
    hid = jax.nn.gelu(jnp.einsum("bnegi,eio->bnego", blk, w1))
    return jnp.einsum("bnegi,eio->bnego", hid, w2)


def cmp_attend(q, kv_cmp, q_pos):
    n_cmp = kv_cmp.shape[1]
    logits = jnp.einsum("btgrd,bngd->btgrn", q, kv_cmp[:, :, 0]).astype(jnp.float32) * ATTN_SCALE
    vis = ((jnp.arange(n_cmp) * CMP_STRIDE + CMP_BLK - 1)[None, :] <= q_pos[:, None])[None, :, None, None, :]
    p = jnp.where(vis, masked_softmax(logits, vis), 0.0)
    return jnp.einsum("btgrn,bngd->btgrd", p.astype(q.dtype), kv_cmp[:, :, 1]), p


def select_blocks(p_cmp, q_pos, L):
    n_cmp = p_cmp.shape[-1]
    n_slc = -(-L // SLC_BLK)
    cmp_start = jnp.arange(n_cmp)[:, None] * CMP_STRIDE
    slc_start = jnp.arange(n_slc)[None, :] * SLC_BLK
    overlap = ((cmp_start < slc_start + SLC_BLK) & (cmp_start + CMP_BLK > slc_start)).astype(jnp.float32)
    imp = jnp.einsum("btgrn,nm->btgm", p_cmp, overlap, precision=lax.Precision.HIGHEST)
    cur = (q_pos // SLC_BLK)[None, :, None, None]
    blk = jnp.arange(n_slc)
    imp = jnp.where((blk == 0) | (blk == cur), FORCE, imp)
    imp = jnp.where(blk <= cur, imp, NEG)
    _, idx = lax.top_k(imp, min(N_SEL, n_slc))
    return idx, idx <= cur


def slc_attend(q, kb, vb, blk_idx, blk_ok, q_pos):
    B, T, G, NS, SB, dh = kb.shape
    key_pos = blk_idx[..., None] * SLC_BLK + jnp.arange(SLC_BLK)
    ok = blk_ok[..., None] & (key_pos <= q_pos[None, :, None, None, None])
    logits = jnp.einsum("btgrd,btgnsd->btgrns", q, kb).astype(jnp.float32) * ATTN_SCALE
    p = masked_softmax(logits.reshape(B, T, G, C_REP, NS * SB), ok.reshape(B, T, G, 1, NS * SB))
    return jnp.einsum("btgrk,btgkd->btgrd", p.astype(q.dtype), vb.reshape(B, T, G, NS * SB, dh))


def win_attend(q, kv, q_pos, k_pos):
    diff = q_pos[:, :, None] - k_pos[:, None, :]
    ok = (k_pos[:, None, :] >= 0) & (diff >= 0) & (diff < WINDOW)
    logits = jnp.einsum("bnqgrd,bnkgd->bnqgrk", q, kv[:, :, :, 0]).astype(jnp.float32) * ATTN_SCALE
    p = masked_softmax(logits, ok[None, :, :, None, None, :])
    return jnp.einsum("bnqgrk,bnkgd->bnqgrd", p.astype(q.dtype), kv[:, :, :, 1])


def to_blocks(k):
    B, L = k.shape[:2]
    n = -(-L // SLC_BLK)
    k = jnp.pad(k, ((0, 0), (0, n * SLC_BLK - L), (0, 0), (0, 0)))
    return k.reshape(B, n, SLC_BLK, C_KV, HEAD_DIM).transpose(0, 3, 1, 2, 4)


def nsa_combine(gate, o_cmp, o_slc, o_win):
    o = gate[..., 0:1] * o_cmp + gate[..., 1:2] * o_slc + gate[..., 2:3] * o_win
    B, T = o.shape[:2]
    return o.reshape(B, T, C_HEADS * HEAD_DIM)


def nsa_prompt(z, pe, w1, w2):
    B, T, _ = z.shape
    q, cmp_kv, slc_kv, win_kv, gate = nsa_split(z)
    q_pos = jnp.arange(T)
    o_cmp, p_cmp = cmp_attend(q, compress(cmp_kv, pe, w1, w2), q_pos)
    idx, ok = select_blocks(p_cmp, q_pos, T)
    kb, vb = to_blocks(slc_kv[:, :, 0]), to_blocks(slc_kv[:, :, 1])
    b_ix = jnp.arange(B)[:, None, None, None]
    g_ix = jnp.arange(C_KV)[None, None, :, None]

    def block(start):
        sl = lambda a: lax.dynamic_slice_in_dim(a, start, C_QBLK, axis=1)
        ib = sl(idx)
        return slc_attend(sl(q), kb[b_ix, g_ix, ib], vb[b_ix, g_ix, ib], ib, sl(ok), start + jnp.arange(C_QBLK))

    o_slc = jnp.moveaxis(lax.map(block, jnp.arange(0, T, C_QBLK)), 0, 1).reshape(q.shape)
    nb = T // WIN_QBLK
    kv_pad = jnp.pad(win_kv, ((0, 0), (WINDOW, 0), (0, 0), (0, 0), (0, 0)))
    kidx = jnp.arange(nb)[:, None] * WIN_QBLK + jnp.arange(WINDOW + WIN_QBLK)[None, :]
    o_win = win_attend(q.reshape(B, nb, WIN_QBLK, C_KV, C_REP, HEAD_DIM), kv_pad[:, kidx],
                       q_pos.reshape(nb, WIN_QBLK), kidx - WINDOW).reshape(q.shape)
    return nsa_combine(gate, o_cmp, o_slc, o_win), cmp_kv, slc_kv, kv_pad[:, -WINDOW:]


def nsa_sample(z, pe, w1, w2, pool_cmp, pool_slc, win_buf, page_table, layer):
    B, T, _ = z.shape
    q, cmp_kv, slc_kv, win_kv, gate = nsa_split(z)
    L = PAST_LEN + T
    q_pos = PAST_LEN + jnp.arange(T)
    cmp_all = jnp.concatenate([gather_pages(pool_cmp, page_table, layer), cmp_kv], axis=1)
    o_cmp, p_cmp = cmp_attend(q, compress(cmp_all, pe, w1, w2), q_pos)
    idx, ok = select_blocks(p_cmp, q_pos, L)
    past_blk = PAST_LEN // SLC_BLK
    is_past = (idx < past_blk)[..., None, None]
    pos = jnp.minimum(idx, past_blk - 1)[..., None] * SLC_BLK + jnp.arange(SLC_BLK)
    new_idx = jnp.clip(idx - past_blk, 0, -(-T // SLC_BLK) - 1)
    b_ix = jnp.arange(B)[:, None, None, None]
    g_ix = jnp.arange(C_KV)[None, None, :, None]
    kb = jnp.where(is_past, gather_group_rows(pool_slc, page_table, layer, pos, 0),
                   to_blocks(slc_kv[:, :, 0])[b_ix, g_ix, new_idx])
    vb = jnp.where(is_past, gather_group_rows(pool_slc, page_table, layer, pos, 1),
                   to_blocks(slc_kv[:, :, 1])[b_ix, g_ix, new_idx])
    o_slc = slc_attend(q, kb, vb, idx, ok, q_pos)
    kv_all = jnp.concatenate([win_buf, win_kv], axis=1)
    k_pos = PAST_LEN - WINDOW + jnp.arange(WINDOW + T)
    o_win = win_attend(q[:, None], kv_all[:, None], q_pos[None], k_pos[None])[:, 0]
    return nsa_combine(gate, o_cmp, o_slc, o_win), cmp_kv, slc_kv, kv_all[:, -WINDOW:]


def kernel(x_prompt, x_sample, cache_dsa_kv, cache_dsa_idx, state_gdn_conv, state_gdn_rec, cache_nsa_cmp, cache_nsa_slc, cache_nsa_win, page_table, c_prompt, c_sample, ada_w, ada_b, ln_g, ln_b, dsa_w_in, dsa_w_out, gdn_w_in, gdn_conv_w, gdn_a_log, gdn_dt_bias, gdn_norm_g, gdn_w_out, nsa_w_in, nsa_cmp_pe, nsa_cmp_w1, nsa_cmp_w2, nsa_w_out, moe_w_router, moe_b_router, moe_w_in, moe_b_in, moe_w_out, moe_b_out):
    xp, xs = x_prompt, x_sample
    bp, tp, _ = xp.shape
    bs, ts, _ = xs.shape
    n_p, n_s = bp * tp, bs * ts
    outs = {k: [] for k in ("dsa_kv_p", "dsa_kv_s", "dsa_idx_p", "dsa_idx_s", "gdn_conv_p", "gdn_conv_s", "gdn_rec_p",
                            "gdn_rec_s", "nsa_cmp_p", "nsa_cmp_s", "nsa_slc_p", "nsa_slc_s", "nsa_win_p", "nsa_win_s")}
    moe_w = (moe_w_router, moe_b_router, moe_w_in, moe_b_in, moe_w_out, moe_b_out)
    c_all = jax.nn.silu(jnp.concatenate([c_prompt, c_sample], axis=0))
    for i in range(DEPTH):
        kind, j = i % N_MIXERS, i // N_MIXERS
        mod = matmul(c_all, ada_w, i, ada_b).reshape(bp + bs, 6, D_MODEL)
        mp, ms = mod[:bp], mod[bp:]
        if kind == 0:
            w_in, w_out = dsa_w_in, dsa_w_out
        elif kind == 1:
            w_in, w_out = gdn_w_in, gdn_w_out
        else:
            w_in, w_out = nsa_w_in, nsa_w_out
        zp = matmul_modulated(xp, mp[:, 1], mp[:, 0], w_in, j)
        hs = xs * (1.0 + ms[:, 1][:, None, :]) + ms[:, 0][:, None, :]
        zs = matmul(hs.reshape(n_s, D_MODEL), w_in, j).reshape(bs, ts, -1)
        if kind == 0:
            op, kv_p, ki_p = dsa_prompt(zp)
            os_, kv_s, ki_s = dsa_sample(zs, cache_dsa_kv, cache_dsa_idx, page_table, j)
            outs["dsa_kv_p"].append(kv_p)
            outs["dsa_kv_s"].append(kv_s)
            outs["dsa_idx_p"].append(ki_p)
            outs["dsa_idx_s"].append(ki_s)
        elif kind == 1:
            gdn_args = (gdn_conv_w[j], gdn_a_log[j], gdn_dt_bias[j], gdn_norm_g[j])
            zero_conv = jnp.zeros((bp, B_CONV - 1, B_CONV_CH), xp.dtype)
            zero_rec = jnp.zeros((bp, B_V_HEADS, B_DK, B_DV), xp.dtype)
            op, conv_p, rec_p = gdn_mixer(zp, *gdn_args, zero_conv, zero_rec)
            os_, conv_s, rec_s = gdn_mixer(zs, *gdn_args, state_gdn_conv[:, j], state_gdn_rec[:, j])
            outs["gdn_conv_p"].append(conv_p)
            outs["gdn_conv_s"].append(conv_s)
            outs["gdn_rec_p"].append(rec_p)
            outs["gdn_rec_s"].append(rec_s)
        else:
            nsa_args = (nsa_cmp_pe[j], nsa_cmp_w1[j], nsa_cmp_w2[j])
            op, cmp_p, slc_p, win_p = nsa_prompt(zp, *nsa_args)
            os_, cmp_s, slc_s, win_s = nsa_sample(zs, *nsa_args, cache_nsa_cmp, cache_nsa_slc, cache_nsa_win[:, j],
                                                  page_table, j)
            outs["nsa_cmp_p"].append(cmp_p)
            outs["nsa_cmp_s"].append(cmp_s)
            outs["nsa_slc_p"].append(slc_p)
            outs["nsa_slc_s"].append(slc_s)
            outs["nsa_win_p"].append(win_p)
            outs["nsa_win_s"].append(win_s)
        yp = matmul(op.reshape(n_p, -1), w_out, j).reshape(bp, tp, D_MODEL)
        ys = matmul(os_.reshape(n_s, -1), w_out, j).reshape(bs, ts, D_MODEL)
        xp, hp = post_norm_modulate(xp, yp, mp[:, 2], ln_g[i, 0], ln_b[i, 0], mp[:, 4], mp[:, 3])
        xs, hs = post_norm_modulate(xs, ys, ms[:, 2], ln_g[i, 0], ln_b[i, 0], ms[:, 4], ms[:, 3])
        h_all = jnp.concatenate([hp.reshape(n_p, D_MODEL), hs.reshape(n_s, D_MODEL)], axis=0)
        y_all = moe(h_all, i, *moe_w)
        zero = jnp.zeros((bp + bs, D_MODEL), jnp.float32)
        xp, _ = post_norm_modulate(xp, y_all[:n_p].reshape(bp, tp, D_MODEL), mp[:, 5], ln_g[i, 1], ln_b[i, 1],
                                   zero[:bp], zero[:bp])
        xs, _ = post_norm_modulate(xs, y_all[n_p:].reshape(bs, ts, D_MODEL), ms[:, 5], ln_g[i, 1], ln_b[i, 1],
                                   zero[bp:], zero[bp:])
    st = lambda k, ax: jnp.stack(outs[k], axis=ax)
    return (xp, xs,
            st("dsa_kv_p", 2), st("dsa_kv_s", 2), st("dsa_idx_p", 2), st("dsa_idx_s", 2),
            st("gdn_conv_p", 1), st("gdn_conv_s", 1), st("gdn_rec_p", 1), st("gdn_rec_s", 1),
            st("nsa_cmp_p", 2), st("nsa_cmp_s", 2), st("nsa_slc_p", 2), st("nsa_slc_s", 2),
            st("nsa_win_p", 1), st("nsa_win_s", 1))
```

```python
import functools
import math

import jax
import jax.numpy as jnp
import numpy as np
from jax import lax
from jax.experimental import pallas as pl
from jax.experimental.pallas import tpu as pltpu

D_MODEL = 2048
DEPTH = 4
PAST_LEN = 16384
PAGE_SIZE = 128
N_MIXERS = 3
ALPHA = (2.0 * DEPTH) ** 0.25
LN_EPS = 1e-5
NORM_EPS = 1e-6
NEG = -1e30
FORCE = 1e9
HEAD_DIM = 128
ATTN_SCALE = HEAD_DIM ** -0.5

A_HEADS = D_MODEL // HEAD_DIM
A_KV = 4
A_REP = A_HEADS // A_KV
A_IDX_HEADS = 16
A_IDX_DIM = 64
A_TOPK = 256
A_QBLK = 128

B_QK_HEADS = 16
B_V_HEADS = 32
B_DK = 128
B_DV = 128
B_CONV = 4
B_CHUNK = 64
B_CONV_CH = 2 * B_QK_HEADS * B_DK + B_V_HEADS * B_DV

C_HEADS = D_MODEL // HEAD_DIM
C_KV = 4
C_REP = C_HEADS // C_KV
CMP_BLK = 32
CMP_STRIDE = 16
SLC_BLK = 64
N_SEL = 16
WINDOW = 512
C_QBLK = 32
WIN_QBLK = 128

N_EXPERTS = 32
TOP_K = 4
D_FF = 2048
SWIGLU_LIMIT = 7.0
SWIGLU_ALPHA = 1.702

VMEM_LIMIT_BYTES = 56 * 1024 * 1024
MOE_ROW_BLK = 256
MOE_FF_TILE = 512
MOE_OUT_TILE = 512


def _params(*sem):
    return pltpu.CompilerParams(dimension_semantics=sem, vmem_limit_bytes=VMEM_LIMIT_BYTES)


def _bf16(x):
    return x.astype(jnp.bfloat16)


def _mm_kernel(x_ref, w_ref, o_ref):
    o_ref[...] = jnp.dot(_bf16(x_ref[...]), _bf16(w_ref[...]), preferred_element_type=jnp.float32)


def _mm_bias_kernel(x_ref, w_ref, b_ref, o_ref):
    o_ref[...] = jnp.dot(_bf16(x_ref[...]), _bf16(w_ref[...]), preferred_element_type=jnp.float32) + b_ref[...]


def _mm_mod_kernel(x_ref, sc_ref, sh_ref, w_ref, o_ref):
    h = x_ref[0] * (1.0 + sc_ref[0]) + sh_ref[0]
    o_ref[0] = jnp.dot(_bf16(h), _bf16(w_ref[...]), preferred_element_type=jnp.float32)


def matmul(x, w, layer, b=None, tm=512, tn=512):
    m, k = x.shape
    n = w.shape[2]
    tm = min(tm, m)
    tn = min(tn, n)
    grid = (pl.cdiv(m, tm), pl.cdiv(n, tn))
    in_specs = [pl.BlockSpec((tm, k), lambda i, j: (i, 0)), pl.BlockSpec((None, k, tn), lambda i, j: (layer, 0, j))]
    args = [x, w]
    body = _mm_kernel
    if b is not None:
        in_specs.append(pl.BlockSpec((None, 1, tn), lambda i, j: (layer, 0, j)))
        args.append(b.reshape(b.shape[0], 1, n))
        body = _mm_bias_kernel
    return pl.pallas_call(
        body, grid=grid, in_specs=in_specs,
        out_specs=pl.BlockSpec((tm, tn), lambda i, j: (i, j)),
        out_shape=jax.ShapeDtypeStruct((m, n), jnp.float32),
        compiler_params=_params("parallel", "parallel"))(*args)


def matmul_modulated(x, scale, shift, w, layer, tm=512, tn=512):
    bsz, t, k = x.shape
    n = w.shape[2]
    tm = min(tm, t)
    tn = min(tn, n)
    grid = (bsz, pl.cdiv(t, tm), pl.cdiv(n, tn))
    mod_spec = pl.BlockSpec((1, 1, k), lambda b, i, j: (b, 0, 0))
    return pl.pallas_call(
        _mm_mod_kernel, grid=grid,
        in_specs=[pl.BlockSpec((1, tm, k), lambda b, i, j: (b, i, 0)), mod_spec, mod_spec,
                  pl.BlockSpec((None, k, tn), lambda b, i, j: (layer, 0, j))],
        out_specs=pl.BlockSpec((1, tm, tn), lambda b, i, j: (b, i, j)),
        out_shape=jax.ShapeDtypeStruct((bsz, t, n), jnp.float32),
        compiler_params=_params("parallel", "parallel", "parallel"))(
            x, scale.reshape(bsz, 1, k), shift.reshape(bsz, 1, k), w)


def _postnorm_kernel(x_ref, y_ref, gate_ref, g_ref, b_ref, sc_ref, sh_ref, xo_ref, ho_ref):
    z = ALPHA * x_ref[0] + gate_ref[0] * y_ref[0]
    mu = jnp.mean(z, axis=-1, keepdims=True)
    zc = z - mu
    var = jnp.mean(zc * zc, axis=-1, keepdims=True)
    xn = zc * lax.rsqrt(var + LN_EPS) * g_ref[...] + b_ref[...]
    xo_ref[0] = xn
    ho_ref[0] = xn * (1.0 + sc_ref[0]) + sh_ref[0]


def post_norm_modulate(x, y, gate, g, b, scale, shift, tm=256):
    bsz, t, d = x.shape
    tm = min(tm, t)
    row = pl.BlockSpec((1, tm, d), lambda bi, i: (bi, i, 0))
    per_seq = pl.BlockSpec((1, 1, d), lambda bi, i: (bi, 0, 0))
    shared = pl.BlockSpec((1, d), lambda bi, i: (0, 0))
    return pl.pallas_call(
        _postnorm_kernel, grid=(bsz, t // tm),
        in_specs=[row, row, per_seq, shared, shared, per_seq, per_seq],
        out_specs=[row, row],
        out_shape=[jax.ShapeDtypeStruct(x.shape, jnp.float32)] * 2,
        compiler_params=_params("parallel", "parallel"))(
            x, y, gate.reshape(bsz, 1, d), g.reshape(1, d), b.reshape(1, d),
            scale.reshape(bsz, 1, d), shift.reshape(bsz, 1, d))


def _moe_in_kernel(blk_e_ref, x_ref, wg_ref, wl_ref, bg_ref, bl_ref, o_ref):
    x = x_ref[...]
    glu = jnp.dot(x, _bf16(wg_ref[...]), preferred_element_type=jnp.float32) + bg_ref[...]
    lin = jnp.dot(x, _bf16(wl_ref[...]), preferred_element_type=jnp.float32) + bl_ref[...]
    glu = jnp.minimum(glu, SWIGLU_LIMIT)
    lin = jnp.clip(lin, -SWIGLU_LIMIT, SWIGLU_LIMIT)
    o_ref[...] = (glu * jax.nn.sigmoid(SWIGLU_ALPHA * glu) * (lin + 1.0)).astype(o_ref.dtype)


def _moe_out_kernel(blk_e_ref, h_ref, w_ref, b_ref, o_ref):
    o_ref[...] = jnp.dot(h_ref[...], _bf16(w_ref[...]), preferred_element_type=jnp.float32) + b_ref[...]


def moe_ffn(xs, blk_expert, layer, w_in, b_in, w_out, b_out):
    r = xs.shape[0]
    n_blk = r // MOE_ROW_BLK
    n_ff = D_FF // MOE_FF_TILE
    tm, tf, tn = MOE_ROW_BLK, MOE_FF_TILE, MOE_OUT_TILE
    b_in4 = b_in.reshape(DEPTH, N_EXPERTS, 1, 2 * D_FF)
    b_out4 = b_out.reshape(DEPTH, N_EXPERTS, 1, D_MODEL)
    act = pl.pallas_call(
        _moe_in_kernel,
        grid_spec=pltpu.PrefetchScalarGridSpec(
            num_scalar_prefetch=1, grid=(n_ff, n_blk),
            in_specs=[
                pl.BlockSpec((tm, D_MODEL), lambda j, m, be: (m, 0)),
                pl.BlockSpec((None, None, D_MODEL, tf), lambda j, m, be: (layer, be[m], 0, j)),
                pl.BlockSpec((None, None, D_MODEL, tf), lambda j, m, be: (layer, be[m], 0, n_ff + j)),
                pl.BlockSpec((None, None, 1, tf), lambda j, m, be: (layer, be[m], 0, j)),
                pl.BlockSpec((None, None, 1, tf), lambda j, m, be: (layer, be[m], 0, n_ff + j)),
            ],
            out_specs=pl.BlockSpec((tm, tf), lambda j, m, be: (m, j))),
        out_shape=jax.ShapeDtypeStruct((r, D_FF), jnp.bfloat16),
        compiler_params=_params("arbitrary", "arbitrary"))(blk_expert, xs, w_in, w_in, b_in4, b_in4)
    return pl.pallas_call(
        _moe_out_kernel,
        grid_spec=pltpu.PrefetchScalarGridSpec(
            num_scalar_prefetch=1, grid=(D_MODEL // tn, n_blk),
            in_specs=[
                pl.BlockSpec((tm, D_FF), lambda j, m, be: (m, 0)),
                pl.BlockSpec((None, None, D_FF, tn), lambda j, m, be: (layer, be[m], 0, j)),
                pl.BlockSpec((None, None, 1, tn), lambda j, m, be: (layer, be[m], 0, j)),
            ],
            out_specs=pl.BlockSpec((tm, tn), lambda j, m, be: (m, j))),
        out_shape=jax.ShapeDtypeStruct((r, D_MODEL), jnp.float32),
        compiler_params=_params("arbitrary", "arbitrary"))(blk_expert, act, w_out, b_out4)


def moe(x, layer, w_router, b_router, w_in, b_in, w_out, b_out):
    n_tok = x.shape[0]
    n_asg = n_tok * TOP_K
    logits = jnp.dot(x, w_router[layer], precision=lax.Precision.HIGHEST) + b_router[layer]
    top_val, top_idx = lax.top_k(logits, TOP_K)
    gate = jax.nn.softmax(top_val, axis=-1)
    e_flat = top_idx.reshape(-1)
    order = jnp.argsort(e_flat)
    e_sorted = e_flat[order]
    counts = jnp.bincount(e_flat, length=N_EXPERTS)
    blocks = (counts + MOE_ROW_BLK - 1) // MOE_ROW_BLK
    blk_end = jnp.cumsum(blocks)
    dest = (blk_end - blocks)[e_sorted] * MOE_ROW_BLK + jnp.arange(n_asg) - (jnp.cumsum(counts) - counts)[e_sorted]
    n_blk = -(-n_asg // MOE_ROW_BLK) + N_EXPERTS
    row_tok = jnp.zeros((n_blk * MOE_ROW_BLK,), jnp.int32).at[dest].set((order // TOP_K).astype(jnp.int32))
    blk_expert = jnp.minimum(jnp.searchsorted(blk_end, jnp.arange(n_blk), side="right"), N_EXPERTS - 1).astype(jnp.int32)
    xs = _bf16(x)[row_tok]
    ys = moe_ffn(xs, blk_expert, layer, w_in, b_in, w_out, b_out)
    dest_of_asg = jnp.zeros((n_asg,), jnp.int32).at[order].set(dest.astype(jnp.int32))
    return jnp.sum(ys[dest_of_asg].reshape(n_tok, TOP_K, D_MODEL) * gate[..., None], axis=1)


def split_cols(z, sizes):
    return jnp.split(z, np.cumsum(sizes)[:-1].tolist(), axis=-1)


def rms_norm(x, g):
    return x * lax.rsqrt(jnp.mean(jnp.square(x), axis=-1, keepdims=True) + NORM_EPS) * g


def l2_normalize(x):
    return x * lax.rsqrt(jnp.sum(jnp.square(x), axis=-1, keepdims=True) + NORM_EPS)


def masked_softmax(logits, mask):
    return jax.nn.softmax(jnp.where(mask, logits, NEG), axis=-1)


def gather_pages(pool, page_table, layer):
    rows = pool[page_table, :, layer]
    return rows.reshape(rows.shape[0], -1, *rows.shape[3:])


def gather_rows(pool, page_table, layer, pos):
    b = jnp.arange(pos.shape[0]).reshape((-1,) + (1,) * (pos.ndim - 1))
    phys = page_table[b, pos // PAGE_SIZE]
    return pool[phys, pos % PAGE_SIZE, layer]


def gather_group_rows(pool, page_table, layer, pos, kv_j):
    b = jnp.arange(pos.shape[0])[:, None, None, None, None]
    g = jnp.arange(C_KV)[None, None, :, None, None]
    phys = page_table[b, pos // PAGE_SIZE]
    return pool[phys, pos % PAGE_SIZE, layer, kv_j, g]


def dsa_split(z):
    B, T, _ = z.shape
    q, k, v, qi, ki, wi = split_cols(z, [A_HEADS * HEAD_DIM, A_KV * HEAD_DIM, A_KV * HEAD_DIM,
                                         A_IDX_HEADS * A_IDX_DIM, A_IDX_DIM, A_IDX_HEADS])
    q = q.reshape(B, T, A_KV, A_REP, HEAD_DIM)
    kv = jnp.stack([k.reshape(B, T, A_KV, HEAD_DIM), v.reshape(B, T, A_KV, HEAD_DIM)], axis=2)
    qi = qi.reshape(B, T, A_IDX_HEADS, A_IDX_DIM)
    return q, kv, qi, ki, wi * A_IDX_HEADS ** -0.5


def indexer_scores(qi, wi, ki, q_pos, k_pos):
    s = jax.nn.relu(jnp.einsum("bthd,bsd->bths", qi, ki).astype(jnp.float32) * A_IDX_DIM ** -0.5)
    score = jnp.einsum("bths,bth->bts", s, wi.astype(jnp.float32))
    return jnp.where(k_pos[None, None, :] <= q_pos[None, :, None], score, NEG)


def gathered_attend(q, kv_sel, ok):
    logits = jnp.einsum("btgrd,btkgd->btgrk", q, kv_sel[:, :, :, 0]).astype(jnp.float32) * ATTN_SCALE
    p = masked_softmax(logits, ok[:, :, None, None, :])
    return jnp.einsum("btgrk,btkgd->btgrd", p.astype(q.dtype), kv_sel[:, :, :, 1])


INT32_MIN = -2 ** 31
LANES = 128


def _sortable_key(x):
    bits = pltpu.bitcast(x, jnp.int32)
    return jnp.where(bits < 0, bits ^ 0x7FFFFFFF, bits)


def _kth_largest_key(key_ref, n_chunks, chunk, k, rows):
    def count_ge(cand):
        def body(c, acc):
            off = pl.multiple_of(c * chunk, chunk)
            for j in range(chunk // LANES):
                keys = key_ref[:, pl.ds(off + j * LANES, LANES)]
                acc = acc + jnp.where(keys >= cand, 1.0, 0.0)
            return acc

        acc = lax.fori_loop(0, n_chunks, body, jnp.zeros((rows, LANES), jnp.float32))
        return jnp.sum(acc, axis=-1, keepdims=True)

    def bit_step(b, ans):
        cand = ans + lax.shift_left(jnp.int32(1), 31 - b)
        return jnp.where(count_ge(cand) >= k, cand, ans)

    return lax.fori_loop(0, 32, bit_step, jnp.full((rows, 1), INT32_MIN, jnp.int32))


def _topk_bias(key_ref, bias_ref, thr, n_chunks, chunk, k, rows, row_pos):
    tri = (lax.broadcasted_iota(jnp.int32, (LANES, LANES), 0)
           <= lax.broadcasted_iota(jnp.int32, (LANES, LANES), 1)).astype(jnp.bfloat16)
    col = lax.broadcasted_iota(jnp.int32, (rows, LANES), 1)

    def count_gt(c, acc):
        off = pl.multiple_of(c * chunk, chunk)
        for j in range(chunk // LANES):
            acc = acc + jnp.where(key_ref[:, pl.ds(off + j * LANES, LANES)] > thr, 1.0, 0.0)
        return acc

    n_gt = jnp.sum(lax.fori_loop(0, n_chunks, count_gt, jnp.zeros((rows, LANES), jnp.float32)), axis=-1, keepdims=True)
    need = k - n_gt

    def body(c, run):
        off = pl.multiple_of(c * chunk, chunk)
        for j in range(chunk // LANES):
            keys = key_ref[:, pl.ds(off + j * LANES, LANES)]
            eq = keys == thr
            eq_f = jnp.where(eq, 1.0, 0.0)
            incl = jnp.dot(eq_f.astype(jnp.bfloat16), tri, preferred_element_type=jnp.float32)
            sel = (keys > thr) | (eq & (run + incl - eq_f < need))
            ok = sel & (off + j * LANES + col <= row_pos)
            bias_ref[:, pl.ds(off + j * LANES, LANES)] = jnp.where(ok, 0.0, NEG)
            run = run + incl[:, LANES - 1:LANES]
        return run

    lax.fori_loop(0, n_chunks, body, jnp.zeros((rows, 1), jnp.float32))


def _stack_heads(q_ref, g, rep):
    return jnp.concatenate([q_ref[0, :, (g * rep + r) * HEAD_DIM:(g * rep + r + 1) * HEAD_DIM] for r in range(rep)], axis=0)


def _flash_group(qg, kt_ref, v_ref, g, lo, hi, chunk, bias_fn):
    rows = qg.shape[0]

    def body(c, carry):
        m, l, acc = carry
        off = pl.multiple_of(c * chunk, chunk)
        kt = kt_ref[0, g * HEAD_DIM:(g + 1) * HEAD_DIM, pl.ds(off, chunk)]
        s = jnp.dot(qg, kt, preferred_element_type=jnp.float32) * ATTN_SCALE + bias_fn(off)
        m_new = jnp.maximum(m, jnp.max(s, axis=-1, keepdims=True))
        p = jnp.exp(s - m_new)
        alpha = jnp.exp(m - m_new)
        l = alpha * l + jnp.sum(p, axis=-1, keepdims=True)
        v = v_ref[0, pl.ds(off, chunk), g * HEAD_DIM:(g + 1) * HEAD_DIM]
        acc = alpha * acc + jnp.dot(_bf16(p), v, preferred_element_type=jnp.float32)
        return m_new, l, acc

    init = (jnp.full((rows, 1), NEG, jnp.float32), jnp.zeros((rows, 1), jnp.float32),
            jnp.zeros((rows, HEAD_DIM), jnp.float32))
    _, l, acc = lax.fori_loop(lo, hi, body, init)
    return acc / l


DSA_TQ = 128
DSA_TK = 512


def _dsa_prompt_kernel(q_ref, qi_ref, wi_ref, kit_ref, kt_ref, v_ref, o_ref, key_ref, bias_ref, *, topk):
    tq, tk = DSA_TQ, DSA_TK
    q_start = pl.program_id(1) * tq
    n_chunks = (q_start + tq + tk - 1) // tk
    row_pos = q_start + lax.broadcasted_iota(jnp.int32, (tq, tk), 0)
    col = lax.broadcasted_iota(jnp.int32, (tq, tk), 1)
    wi = wi_ref[0]

    def score_chunk(c, carry):
        off = pl.multiple_of(c * tk, tk)
        kit = kit_ref[0, :, pl.ds(off, tk)]
        acc = jnp.zeros((tq, tk), jnp.float32)
        for h in range(A_IDX_HEADS):
            s = jnp.dot(qi_ref[0, h], kit, preferred_element_type=jnp.float32)
            acc = acc + jnp.maximum(s, 0.0) * wi[:, h:h + 1]
        key_ref[:, pl.ds(off, tk)] = _sortable_key(jnp.where(off + col <= row_pos, acc, NEG))
        return carry

    lax.fori_loop(0, n_chunks, score_chunk, 0)
    thr = _kth_largest_key(key_ref, n_chunks, tk, topk, tq)
    _topk_bias(key_ref, bias_ref, thr, n_chunks, tk, topk, tq, row_pos[:, :LANES])
    bias_fn = lambda off: jnp.concatenate([bias_ref[:, pl.ds(off, tk)]] * A_REP, axis=0)
    for g in range(A_KV):
        out = _flash_group(_stack_heads(q_ref, g, A_REP), kt_ref, v_ref, g, 0, n_chunks, tk, bias_fn)
        for r in range(A_REP):
            h = g * A_REP + r
            o_ref[0, :, h * HEAD_DIM:(h + 1) * HEAD_DIM] = out[r * tq:(r + 1) * tq].astype(o_ref.dtype)


def dsa_prompt(z):
    B, T, _ = z.shape
    topk = min(A_TOPK, T // 4)
    nq, nk = A_HEADS * HEAD_DIM, A_KV * HEAD_DIM
    q = _bf16(z[..., :nq])
    k, v = z[..., nq:nq + nk], z[..., nq + nk:nq + 2 * nk]
    off = nq + 2 * nk
    qi = _bf16(z[..., off:off + A_IDX_HEADS * A_IDX_DIM]).reshape(B, T, A_IDX_HEADS, A_IDX_DIM).transpose(0, 2, 1, 3)
    off += A_IDX_HEADS * A_IDX_DIM
    ki = z[..., off:off + A_IDX_DIM]
    wi = z[..., off + A_IDX_DIM:] * (A_IDX_HEADS ** -0.5 * A_IDX_DIM ** -0.5)
    kit = _bf16(ki).transpose(0, 2, 1)
    kt = _bf16(k).transpose(0, 2, 1)
    tq = DSA_TQ
    o = pl.pallas_call(
        functools.partial(_dsa_prompt_kernel, topk=topk),
        grid=(B, T // tq),
        in_specs=[
            pl.BlockSpec((1, tq, nq), lambda b, i: (b, i, 0)),
            pl.BlockSpec((1, A_IDX_HEADS, tq, A_IDX_DIM), lambda b, i: (b, 0, i, 0)),
            pl.BlockSpec((1, tq, A_IDX_HEADS), lambda b, i: (b, i, 0)),
            pl.BlockSpec((1, A_IDX_DIM, T), lambda b, i: (b, 0, 0)),
            pl.BlockSpec((1, nk, T), lambda b, i: (b, 0, 0)),
            pl.BlockSpec((1, T, nk), lambda b, i: (b, 0, 0)),
        ],
        out_specs=pl.BlockSpec((1, tq, nq), lambda b, i: (b, i, 0)),
        out_shape=jax.ShapeDtypeStruct((B, T, nq), jnp.bfloat16),
        scratch_shapes=[pltpu.VMEM((tq, T), jnp.int32), pltpu.VMEM((tq, T), jnp.float32)],
        compiler_params=_params("parallel", "arbitrary"))(q, qi, wi, kit, kt, _bf16(v))
    kv = jnp.stack([k.reshape(B, T, A_KV, HEAD_DIM), v.reshape(B, T, A_KV, HEAD_DIM)], axis=2)
    return o, kv, ki


def _nt_dot(a, b):
    return lax.dot_general(a, b, (((1,), (1,)), ((), ())), preferred_element_type=jnp.float32)


def _dsa_sample_score_kernel(pt_ref, qi_ref, wi_ref, pool_ref, new_ref, o_ref, *, layer, n_pages, t_new):
    p = pl.program_id(1)

    def scores(ki):
        s = jnp.maximum(_nt_dot(qi_ref[0], _bf16(ki)), 0.0) * wi_ref[0]
        acc = s[0:t_new]
        for h in range(1, A_IDX_HEADS):
            acc = acc + s[h * t_new:(h + 1) * t_new]
        k_pos = p * PAGE_SIZE + lax.broadcasted_iota(jnp.int32, (t_new, PAGE_SIZE), 1)
        q_pos = n_pages * PAGE_SIZE + lax.broadcasted_iota(jnp.int32, (t_new, PAGE_SIZE), 0)
        o_ref[0] = jnp.where(k_pos <= q_pos, acc, NEG)

    @pl.when(p < n_pages)
    def _():
        scores(pool_ref[:, layer * A_IDX_DIM:(layer + 1) * A_IDX_DIM])

    @pl.when(p == n_pages)
    def _():
        scores(new_ref[0])


def _dsa_sample_select_kernel(s_ref, bias_ref, key_ref, *, topk, chunk, q_pos0):
    rows, n_keys = key_ref.shape
    n_chunks = n_keys // chunk
    for c in range(n_chunks):
        key_ref[:, c * chunk:(c + 1) * chunk] = _sortable_key(s_ref[0, :, c * chunk:(c + 1) * chunk])
    thr = _kth_largest_key(key_ref, n_chunks, chunk, topk, rows)
    q_pos = q_pos0 + lax.broadcasted_iota(jnp.int32, (rows, LANES), 0)
    _topk_bias(key_ref, bias_ref.at[0], thr, n_chunks, chunk, topk, rows, q_pos)


def _dsa_sample_attend_kernel(pt_ref, q_ref, bias_ref, pool_ref, new_ref, o_ref, m_ref, l_ref, acc_ref, *, n_pages, t_new):
    p = pl.program_id(1)
    nk = A_KV * HEAD_DIM
    rows_g = A_REP * t_new

    @pl.when(p == 0)
    def _():
        m_ref[...] = jnp.full(m_ref.shape, NEG, jnp.float32)
        l_ref[...] = jnp.zeros(l_ref.shape, jnp.float32)
        acc_ref[...] = jnp.zeros(acc_ref.shape, jnp.float32)

    def attend(kv):
        bias = jnp.concatenate([bias_ref[0]] * A_REP, axis=0)
        for g in range(A_KV):
            rs = slice(g * rows_g, (g + 1) * rows_g)
            k = _bf16(kv[:, g * HEAD_DIM:(g + 1) * HEAD_DIM])
            v = _bf16(kv[:, nk + g * HEAD_DIM:nk + (g + 1) * HEAD_DIM])
            s = _nt_dot(q_ref[0, rs], k) * ATTN_SCALE + bias
            m = m_ref[rs]
            m_new = jnp.maximum(m, jnp.max(s, axis=-1, keepdims=True))
            pr = jnp.exp(s - m_new)
            alpha = jnp.exp(m - m_new)
            l_ref[rs] = alpha * l_ref[rs] + jnp.sum(pr, axis=-1, keepdims=True)
            acc_ref[rs] = alpha * acc_ref[rs] + jnp.dot(_bf16(pr), v, preferred_element_type=jnp.float32)
            m_ref[rs] = m_new

    @pl.when(p < n_pages)
    def _():
        attend(pool_ref[...])

    @pl.when(p == n_pages)
    def _():
        attend(new_ref[0])
        o_ref[0] = (acc_ref[...] / l_ref[...]).astype(o_ref.dtype)


def dsa_sample(z, pool_kv, pool_idx, page_table, layer):
    B, T, _ = z.shape
    n_pages = page_table.shape[1]
    past = n_pages * PAGE_SIZE
    topk = min(A_TOPK, (past + T) // 4)
    nq, nk = A_HEADS * HEAD_DIM, A_KV * HEAD_DIM
    n_layers = pool_kv.shape[2]
    k, v = z[..., nq:nq + nk], z[..., nq + nk:nq + 2 * nk]
    off = nq + 2 * nk
    head_major = lambda a, d: a.reshape(B, T, -1, d).transpose(0, 2, 1, 3).reshape(B, -1, d)
    q = head_major(_bf16(z[..., :nq]), HEAD_DIM)
    qi = head_major(_bf16(z[..., off:off + A_IDX_HEADS * A_IDX_DIM]), A_IDX_DIM)
    off += A_IDX_HEADS * A_IDX_DIM
    ki = z[..., off:off + A_IDX_DIM]
    wi = head_major(z[..., off + A_IDX_DIM:] * (A_IDX_HEADS ** -0.5 * A_IDX_DIM ** -0.5), 1)
    pad_rows = lambda a: jnp.pad(a, ((0, 0), (0, PAGE_SIZE - T), (0, 0)))
    pool_idx2 = pool_idx.reshape(pool_idx.shape[0], PAGE_SIZE, n_layers * A_IDX_DIM)
    pool_kv2 = pool_kv.reshape(pool_kv.shape[0], PAGE_SIZE, n_layers * 2 * nk)
    n_keys = (n_pages + 1) * PAGE_SIZE
    page = lambda b, p, pt: (pt[b, jnp.minimum(p, n_pages - 1)], 0, 0)
    per_b = lambda b, p, pt: (b, 0, 0)
    scores = pl.pallas_call(
        functools.partial(_dsa_sample_score_kernel, layer=layer, n_pages=n_pages, t_new=T),
        grid_spec=pltpu.PrefetchScalarGridSpec(
            num_scalar_prefetch=1, grid=(B, n_pages + 1),
            in_specs=[pl.BlockSpec((1, A_IDX_HEADS * T, A_IDX_DIM), per_b),
                      pl.BlockSpec((1, A_IDX_HEADS * T, 1), per_b),
                      pl.BlockSpec((None, PAGE_SIZE, n_layers * A_IDX_DIM), page),
                      pl.BlockSpec((1, PAGE_SIZE, A_IDX_DIM), per_b)],
            out_specs=pl.BlockSpec((1, T, PAGE_SIZE), lambda b, p, pt: (b, 0, p))),
        out_shape=jax.ShapeDtypeStruct((B, T, n_keys), jnp.float32),
        compiler_params=_params("parallel", "arbitrary"))(page_table, qi, wi, pool_idx2, pad_rows(ki))
    chunk = 3 * LANES
    assert n_keys % chunk == 0
    bias = pl.pallas_call(
        functools.partial(_dsa_sample_select_kernel, topk=topk, chunk=chunk, q_pos0=past),
        grid=(B,),
        in_specs=[pl.BlockSpec((1, T, n_keys), lambda b: (b, 0, 0))],
        out_specs=pl.BlockSpec((1, T, n_keys), lambda b: (b, 0, 0)),
        out_shape=jax.ShapeDtypeStruct((B, T, n_keys), jnp.float32),
        scratch_shapes=[pltpu.VMEM((T, n_keys), jnp.int32)],
        compiler_params=_params("parallel"))(scores)
    kv_page = lambda b, p, pt: (pt[b, jnp.minimum(p, n_pages - 1)], 0, layer)
    o = pl.pallas_call(
        functools.partial(_dsa_sample_attend_kernel, n_pages=n_pages, t_new=T),
        grid_spec=pltpu.PrefetchScalarGridSpec(
            num_scalar_prefetch=1, grid=(B, n_pages + 1),
            in_specs=[pl.BlockSpec((1, A_HEADS * T, HEAD_DIM), per_b),
                      pl.BlockSpec((1, T, PAGE_SIZE), lambda b, p, pt: (b, 0, p)),
                      pl.BlockSpec((None, PAGE_SIZE, 2 * nk), kv_page),
                      pl.BlockSpec((1, PAGE_SIZE, 2 * nk), per_b)],
            out_specs=pl.BlockSpec((1, A_HEADS * T, HEAD_DIM), per_b),
            scratch_shapes=[pltpu.VMEM((A_HEADS * T, 1), jnp.float32), pltpu.VMEM((A_HEADS * T, 1), jnp.float32),
                            pltpu.VMEM((A_HEADS * T, HEAD_DIM), jnp.float32)]),
        out_shape=jax.ShapeDtypeStruct((B, A_HEADS * T, HEAD_DIM), jnp.bfloat16),
        compiler_params=_params("parallel", "arbitrary"))(
            page_table, q, bias, pool_kv2, pad_rows(jnp.concatenate([k, v], axis=-1)))
    o = o.reshape(B, A_HEADS, T, HEAD_DIM).transpose(0, 2, 1, 3).reshape(B, T, nq)
    kv = jnp.stack([k.reshape(B, T, A_KV, HEAD_DIM), v.reshape(B, T, A_KV, HEAD_DIM)], axis=2)
    return o, kv, ki


def chunk_gated_delta(q, k, v, g, beta, s0):
    B, T, H, DK = q.shape
    C = B_CHUNK
    n = -(-T // C)
    pad = n * C - T

    def chunks(a):
        a = jnp.pad(a, [(0, 0), (0, pad)] + [(0, 0)] * (a.ndim - 2))
        a = a.reshape(B, n, C, *a.shape[2:])
        return jnp.moveaxis(a, (1, 3), (0, 2))

    qc, kc, vc, bc = chunks(q) * DK ** -0.5, chunks(k), chunks(v), chunks(beta)
    gc = jnp.cumsum(chunks(g), axis=-1)
    incl = jnp.tril(jnp.ones((C, C), bool))
    strict = jnp.tril(jnp.ones((C, C), bool), -1)
    decay = jnp.exp(jnp.where(incl, gc[..., :, None] - gc[..., None, :], NEG))
    kb = kc * bc[..., None]
    a_mat = jnp.where(strict, jnp.einsum("...id,...jd->...ij", kb, kc) * decay, 0.0)
    eye = jnp.eye(C, dtype=jnp.float32)
    t_mat = lax.linalg.triangular_solve(eye + a_mat, jnp.broadcast_to(eye, a_mat.shape), left_side=True, lower=True)
    u = t_mat @ (vc * bc[..., None])
    w = t_mat @ (kb * jnp.exp(gc)[..., None])
    attn = jnp.einsum("...id,...jd->...ij", qc, kc) * decay

    def step(s, inp):
        q_i, k_i, u_i, w_i, g_i, attn_i = inp
        v_new = u_i - w_i @ s
        o = (q_i * jnp.exp(g_i)[..., None]) @ s + attn_i @ v_new
        g_last = g_i[..., -1]
        s = s * jnp.exp(g_last)[..., None, None] + jnp.einsum(
            "bhcd,bhce->bhde", k_i * jnp.exp(g_last[..., None] - g_i)[..., None], v_new)
        return s, o

    s_final, o = lax.scan(step, s0, (qc, kc, u, w, gc, attn))
    o = jnp.moveaxis(o, (0, 2), (1, 3)).reshape(B, n * C, H, -1)[:, :T]
    return o, s_final


def gdn_mixer(zin, conv_w, a_log, dt_bias, norm_g, conv_state, rec_state):
    B, T, _ = zin.shape
    qkv, z, b_raw, a_raw = split_cols(zin, [B_CONV_CH, B_V_HEADS * B_DV, B_V_HEADS, B_V_HEADS])
    x_ext = jnp.concatenate([conv_state, qkv], axis=1)
    conv = sum(x_ext[:, i:i + T] * conv_w[i][None, None, :] for i in range(B_CONV))
    q, k, v = split_cols(jax.nn.silu(conv), [B_QK_HEADS * B_DK, B_QK_HEADS * B_DK, B_V_HEADS * B_DV])
    rep = B_V_HEADS // B_QK_HEADS
    q = jnp.repeat(l2_normalize(q.reshape(B, T, B_QK_HEADS, B_DK)), rep, axis=2)
    k = jnp.repeat(l2_normalize(k.reshape(B, T, B_QK_HEADS, B_DK)), rep, axis=2)
    v = v.reshape(B, T, B_V_HEADS, B_DV)
    beta = jax.nn.sigmoid(b_raw)
    g = -jnp.exp(a_log) * jax.nn.softplus(a_raw + dt_bias)
    o, s_new = chunk_gated_delta(q, k, v, g, beta, rec_state)
    o = rms_norm(o, norm_g) * jax.nn.silu(z.reshape(B, T, B_V_HEADS, B_DV))
    return o.reshape(B, T, B_V_HEADS * B_DV), x_ext[:, -(B_CONV - 1):], s_new


def nsa_split(z):
    B, T, _ = z.shape
    parts = split_cols(z, [C_HEADS * HEAD_DIM] + [C_KV * HEAD_DIM] * 6 + [3 * C_HEADS])
    q = parts[0].reshape(B, T, C_KV, C_REP, HEAD_DIM)
    kvs = [p.reshape(B, T, C_KV, HEAD_DIM) for p in parts[1:7]]
    cmp_kv = jnp.stack(kvs[0:2], axis=2)
    slc_kv = jnp.stack(kvs[2:4], axis=2)
    win_kv = jnp.stack(kvs[4:6], axis=2)
    gate = jax.nn.sigmoid(parts[7]).reshape(B, T, C_KV, C_REP, 3)
    return q, cmp_kv, slc_kv, win_kv, gate


def compress(kv, pe, w1, w2):
    B, L = kv.shape[:2]
    n_cmp = (L - CMP_BLK) // CMP_STRIDE + 1
    idx = jnp.arange(n_cmp)[:, None] * CMP_STRIDE + jnp.arange(CMP_BLK)[None, :]
    blk = kv[:, idx] + pe[None, None, :, :, None, :]
    blk = jnp.moveaxis(blk, 2, 4).reshape(B, n_cmp, 2, C_KV, CMP_BLK * HEAD_DIM)
    hid = jax.nn.gelu(jnp.einsum("bnegi,eio->bnego", blk, w1))
    return jnp.einsum("bnegi,eio->bnego", hid, w2)


def cmp_attend(q, kv_cmp, q_pos):
    n_cmp = kv_cmp.shape[1]
    logits = jnp.einsum("btgrd,bngd->btgrn", q, kv_cmp[:, :, 0]).astype(jnp.float32) * ATTN_SCALE
    vis = ((jnp.arange(n_cmp) * CMP_STRIDE + CMP_BLK - 1)[None, :] <= q_pos[:, None])[None, :, None, None, :]
    p = jnp.where(vis, masked_softmax(logits, vis), 0.0)
    return jnp.einsum("btgrn,bngd->btgrd", p.astype(q.dtype), kv_cmp[:, :, 1]), p


def select_blocks(p_cmp, q_pos, L):
    n_cmp = p_cmp.shape[-1]
    n_slc = -(-L // SLC_BLK)
    cmp_start = jnp.arange(n_cmp)[:, None] * CMP_STRIDE
    slc_start = jnp.arange(n_slc)[None, :] * SLC_BLK
    overlap = ((cmp_start < slc_start + SLC_BLK) & (cmp_start + CMP_BLK > slc_start)).astype(jnp.float32)
    imp = jnp.einsum("btgrn,nm->btgm", p_cmp, overlap, precision=lax.Precision.HIGHEST)
    cur = (q_pos // SLC_BLK)[None, :, None, None]
    blk = jnp.arange(n_slc)
    imp = jnp.where((blk == 0) | (blk == cur), FORCE, imp)
    imp = jnp.where(blk <= cur, imp, NEG)
    _, idx = lax.top_k(imp, min(N_SEL, n_slc))
    return idx, idx <= cur


def slc_attend(q, kb, vb, blk_idx, blk_ok, q_pos):
    B, T, G, NS, SB, dh = kb.shape
    key_pos = blk_idx[..., None] * SLC_BLK + jnp.arange(SLC_BLK)
    ok = blk_ok[..., None] & (key_pos <= q_pos[None, :, None, None, None])
    logits = jnp.einsum("btgrd,btgnsd->btgrns", q, kb).astype(jnp.float32) * ATTN_SCALE
    p = masked_softmax(logits.reshape(B, T, G, C_REP, NS * SB), ok.reshape(B, T, G, 1, NS * SB))
    return jnp.einsum("btgrk,btgkd->btgrd", p.astype(q.dtype), vb.reshape(B, T, G, NS * SB, dh))


def win_attend(q, kv, q_pos, k_pos):
    diff = q_pos[:, :, None] - k_pos[:, None, :]
    ok = (k_pos[:, None, :] >= 0) & (diff >= 0) & (diff < WINDOW)
    logits = jnp.einsum("bnqgrd,bnkgd->bnqgrk", q, kv[:, :, :, 0]).astype(jnp.float32) * ATTN_SCALE
    p = masked_softmax(logits, ok[None, :, :, None, None, :])
    return jnp.einsum("bnqgrk,bnkgd->bnqgrd", p.astype(q.dtype), kv[:, :, :, 1])


def to_blocks(k):
    B, L = k.shape[:2]
    n = -(-L // SLC_BLK)
    k = jnp.pad(k, ((0, 0), (0, n * SLC_BLK - L), (0, 0), (0, 0)))
    return k.reshape(B, n, SLC_BLK, C_KV, HEAD_DIM).transpose(0, 3, 1, 2, 4)


def nsa_combine(gate, o_cmp, o_slc, o_win):
    o = gate[..., 0:1] * o_cmp + gate[..., 1:2] * o_slc + gate[..., 2:3] * o_win
    B, T = o.shape[:2]
    return o.reshape(B, T, C_HEADS * HEAD_DIM)


def nsa_compress(kv, pe, w1, w2, layer):
    B, L, _ = kv.shape
    n_cmp = (L - CMP_BLK) // CMP_STRIDE + 1
    n_row = n_cmp + 1
    half = CMP_STRIDE * HEAD_DIM
    x = kv[:, :n_row * CMP_STRIDE].reshape(B, n_row, CMP_STRIDE, 2, C_KV, HEAD_DIM)
    x = x.transpose(3, 0, 4, 1, 2, 5).reshape(2, B * C_KV * n_row, half)
    w1l = w1[layer]
    w_halves = jnp.concatenate([w1l[:, :half], w1l[:, half:]], axis=-1)
    pe_rows = jnp.broadcast_to(pe[layer].transpose(1, 0, 2).reshape(2, 1, CMP_BLK * HEAD_DIM), (2, 8, CMP_BLK * HEAD_DIM))
    outs = []
    for e in range(2):
        ab = matmul(x[e], w_halves, e).reshape(B, C_KV, n_row, 2, HEAD_DIM)
        c = matmul(pe_rows[e], w1l, e)[0]
        hid = jax.nn.gelu(ab[:, :, :n_cmp, 0] + ab[:, :, 1:, 1] + c)
        outs.append(matmul(hid.reshape(B * C_KV * n_cmp, HEAD_DIM), w2[layer], e).reshape(B, C_KV, n_cmp, HEAD_DIM))
    return outs


NSA_TQ = 128
NSA_TK = 512
SLC_SHIFT = SLC_BLK.bit_length() - 1


def _nsa_prompt_kernel(q_ref, gl_ref, kct_ref, vc_ref, kst_ref, vs_ref, kwt_ref, vw_ref, o_ref, key_ref, bias_ref, *,
                       n_cmp, n_slc):
    tq, tk = NSA_TQ, NSA_TK
    rows = C_REP * tq
    q_start = pl.program_id(1) * tq
    n_chunks = (q_start + tq + tk - 1) // tk
    win_lo = jnp.maximum(q_start - (WINDOW - 1), 0) // tk
    n_pad = kct_ref.shape[2]
    iota = lambda shape, ax: lax.broadcasted_iota(jnp.int32, shape, ax)
    tile = lambda a: jnp.concatenate([a] * C_REP, axis=0)

    n_c = iota((tq, n_pad), 1)
    vis = tile((n_c * CMP_STRIDE + CMP_BLK - 1 <= q_start + iota((tq, n_pad), 0)) & (n_c < n_cmp))
    n_o, m_o = iota((n_pad, LANES), 0) * CMP_STRIDE, iota((n_pad, LANES), 1) * SLC_BLK
    overlap = jnp.where((n_o < m_o + SLC_BLK) & (n_o + CMP_BLK > m_o) & (iota((n_pad, LANES), 1) < n_slc), 1.0, 0.0)
    cur = jnp.right_shift(q_start + iota((tq, LANES), 0), SLC_SHIFT)
    blk = iota((tq, LANES), 1)
    o_cmp = []
    for g in range(C_KV):
        qg = _stack_heads(q_ref, g, C_REP)
        s = jnp.dot(qg, kct_ref[0, g * HEAD_DIM:(g + 1) * HEAD_DIM, :], preferred_element_type=jnp.float32) * ATTN_SCALE
        s = jnp.where(vis, s, NEG)
        e = jnp.exp(s - jnp.max(s, axis=-1, keepdims=True))
        p = jnp.where(vis, e / jnp.sum(e, axis=-1, keepdims=True), 0.0)
        o_cmp.append(jnp.dot(_bf16(p), vc_ref[0, :, g * HEAD_DIM:(g + 1) * HEAD_DIM], preferred_element_type=jnp.float32))
        p_grp = sum(p[r * tq:(r + 1) * tq] for r in range(C_REP))
        imp = jnp.dot(p_grp, overlap, preferred_element_type=jnp.float32, precision=lax.Precision.HIGHEST)
        imp = jnp.where((blk == 0) | (blk == cur), FORCE, imp)
        key_ref[g * tq:(g + 1) * tq, :] = _sortable_key(jnp.where(blk <= cur, imp, NEG))
    k_sel = min(N_SEL, n_slc)
    thr = _kth_largest_key(key_ref, 1, LANES, k_sel, rows)
    _topk_bias(key_ref, bias_ref, thr, 1, LANES, k_sel, rows, tile(cur))

    t_k = q_start + iota((tq, tk), 0)
    s_k = iota((tq, tk), 1)
    blk_e, s_e = iota((LANES, tk), 0), iota((LANES, tk), 1)
    gate = jax.nn.sigmoid(gl_ref[0])

    def win_bias(off):
        d = t_k - (off + s_k)
        return tile(jnp.where((d >= 0) & (d < WINDOW), 0.0, NEG))

    for g in range(C_KV):
        qg = _stack_heads(q_ref, g, C_REP)
        sel = _bf16(jnp.where(bias_ref[g * tq:(g + 1) * tq, :] == 0.0, 1.0, 0.0))

        def slc_bias(off, sel=sel):
            expand = _bf16(jnp.where(blk_e == jnp.right_shift(off + s_e, SLC_SHIFT), 1.0, 0.0))
            hit = jnp.dot(sel, expand, preferred_element_type=jnp.float32)
            return tile(jnp.where((hit > 0.5) & (off + s_k <= t_k), 0.0, NEG))

        o_slc = _flash_group(qg, kst_ref, vs_ref, g, 0, n_chunks, tk, slc_bias)
        o_win = _flash_group(qg, kwt_ref, vw_ref, g, win_lo, n_chunks, tk, win_bias)
        for r in range(C_REP):
            h = g * C_REP + r
            rs = slice(r * tq, (r + 1) * tq)
            o = (gate[:, 3 * h:3 * h + 1] * o_cmp[g][rs] + gate[:, 3 * h + 1:3 * h + 2] * o_slc[rs]
                 + gate[:, 3 * h + 2:3 * h + 3] * o_win[rs])
            o_ref[0, :, h * HEAD_DIM:(h + 1) * HEAD_DIM] = o.astype(o_ref.dtype)


def nsa_prompt(z, pe, w1, w2, layer):
    B, T, _ = z.shape
    nq, nk = C_HEADS * HEAD_DIM, C_KV * HEAD_DIM
    q = _bf16(z[..., :nq])
    branch = [z[..., nq + 2 * i * nk:nq + 2 * (i + 1) * nk] for i in range(3)]
    gl = z[..., nq + 6 * nk:]
    k_cmp, v_cmp = nsa_compress(branch[0], pe, w1, w2, layer)
    n_cmp = k_cmp.shape[2]
    n_pad = -(-n_cmp // LANES) * LANES
    pad = ((0, 0), (0, 0), (0, n_pad - n_cmp), (0, 0))
    kct = _bf16(jnp.pad(k_cmp, pad)).transpose(0, 1, 3, 2).reshape(B, nk, n_pad)
    vc = _bf16(jnp.pad(v_cmp, pad)).transpose(0, 2, 1, 3).reshape(B, n_pad, nk)
    kt = lambda a: _bf16(a[..., :nk]).transpose(0, 2, 1)
    tq = NSA_TQ
    full_t = lambda shape: pl.BlockSpec(shape, lambda b, i: (b, 0, 0))
    o = pl.pallas_call(
        functools.partial(_nsa_prompt_kernel, n_cmp=n_cmp, n_slc=-(-T // SLC_BLK)),
        grid=(B, T // tq),
        in_specs=[
            pl.BlockSpec((1, tq, nq), lambda b, i: (b, i, 0)),
            pl.BlockSpec((1, tq, 3 * C_HEADS), lambda b, i: (b, i, 0)),
            full_t((1, nk, n_pad)), full_t((1, n_pad, nk)),
            full_t((1, nk, T)), full_t((1, T, nk)), full_t((1, nk, T)), full_t((1, T, nk)),
        ],
        out_specs=pl.BlockSpec((1, tq, nq), lambda b, i: (b, i, 0)),
        out_shape=jax.ShapeDtypeStruct((B, T, nq), jnp.bfloat16),
        scratch_shapes=[pltpu.VMEM((C_REP * tq, LANES), jnp.int32), pltpu.VMEM((C_REP * tq, LANES), jnp.float32)],
        compiler_params=_params("parallel", "arbitrary"))(
            q, gl, kct, vc, kt(branch[1]), _bf16(branch[1][..., nk:]), kt(branch[2]), _bf16(branch[2][..., nk:]))
    as_kv = lambda a: a.reshape(B, T, 2, C_KV, HEAD_DIM)
    return o, as_kv(branch[0]), as_kv(branch[1]), as_kv(branch[2])[:, -WINDOW:]


def nsa_sample(z, pe, w1, w2, pool_cmp, pool_slc, win_buf, page_table, layer):
    B, T, _ = z.shape
    q, cmp_kv, slc_kv, win_kv, gate = nsa_split(z)
    L = PAST_LEN + T
    q_pos = PAST_LEN + jnp.arange(T)
    cmp_all = jnp.concatenate([gather_pages(pool_cmp, page_table, layer), cmp_kv], axis=1)
    o_cmp, p_cmp = cmp_attend(q, compress(cmp_all, pe, w1, w2), q_pos)
    idx, ok = select_blocks(p_cmp, q_pos, L)
    past_blk = PAST_LEN // SLC_BLK
    is_past = (idx < past_blk)[..., None, None]
    pos = jnp.minimum(idx, past_blk - 1)[..., None] * SLC_BLK + jnp.arange(SLC_BLK)
    new_idx = jnp.clip(idx - past_blk, 0, -(-T // SLC_BLK) - 1)
    b_ix = jnp.arange(B)[:, None, None, None]
    g_ix = jnp.arange(C_KV)[None, None, :, None]
    kb = jnp.where(is_past, gather_group_rows(pool_slc, page_table, layer, pos, 0),
                   to_blocks(slc_kv[:, :, 0])[b_ix, g_ix, new_idx])
    vb = jnp.where(is_past, gather_group_rows(pool_slc, page_table, layer, pos, 1),
                   to_blocks(slc_kv[:, :, 1])[b_ix, g_ix, new_idx])
    o_slc = slc_attend(q, kb, vb, idx, ok, q_pos)
    kv_all = jnp.concatenate([win_buf, win_kv], axis=1)
    k_pos = PAST_LEN - WINDOW + jnp.arange(WINDOW + T)
    o_win = win_attend(q[:, None], kv_all[:, None], q_pos[None], k_pos[None])[:, 0]
    return nsa_combine(gate, o_cmp, o_slc, o_win), cmp_kv, slc_kv, kv_all[:, -WINDOW:]


def kernel(x_prompt, x_sample, cache_dsa_kv, cache_dsa_idx, state_gdn_conv, state_gdn_rec, cache_nsa_cmp, cache_nsa_slc, cache_nsa_win, page_table, c_prompt, c_sample, ada_w, ada_b, ln_g, ln_b, dsa_w_in, dsa_w_out, gdn_w_in, gdn_conv_w, gdn_a_log, gdn_dt_bias, gdn_norm_g, gdn_w_out, nsa_w_in, nsa_cmp_pe, nsa_cmp_w1, nsa_cmp_w2, nsa_w_out, moe_w_router, moe_b_router, moe_w_in, moe_b_in, moe_w_out, moe_b_out):
    xp, xs = x_prompt, x_sample
    bp, tp, _ = xp.shape
    bs, ts, _ = xs.shape
    n_p, n_s = bp * tp, bs * ts
    outs = {k: [] for k in ("dsa_kv_p", "dsa_kv_s", "dsa_idx_p", "dsa_idx_s", "gdn_conv_p", "gdn_conv_s", "gdn_rec_p",
                            "gdn_rec_s", "nsa_cmp_p", "nsa_cmp_s", "nsa_slc_p", "nsa_slc_s", "nsa_win_p", "nsa_win_s")}
    moe_w = (moe_w_router, moe_b_router, moe_w_in, moe_b_in, moe_w_out, moe_b_out)
    c_all = jax.nn.silu(jnp.concatenate([c_prompt, c_sample], axis=0))
    for i in range(DEPTH):
        kind, j = i % N_MIXERS, i // N_MIXERS
        mod = matmul(c_all, ada_w, i, ada_b).reshape(bp + bs, 6, D_MODEL)
        mp, ms = mod[:bp], mod[bp:]
        if kind == 0:
            w_in, w_out = dsa_w_in, dsa_w_out
        elif kind == 1:
            w_in, w_out = gdn_w_in, gdn_w_out
        else:
            w_in, w_out = nsa_w_in, nsa_w_out
        zp = matmul_modulated(xp, mp[:, 1], mp[:, 0], w_in, j)
        hs = xs * (1.0 + ms[:, 1][:, None, :]) + ms[:, 0][:, None, :]
        zs = matmul(hs.reshape(n_s, D_MODEL), w_in, j).reshape(bs, ts, -1)
        if kind == 0:
            op, kv_p, ki_p = dsa_prompt(zp)
            os_, kv_s, ki_s = dsa_sample(zs, cache_dsa_kv, cache_dsa_idx, page_table, j)
            outs["dsa_kv_p"].append(kv_p)
            outs["dsa_kv_s"].append(kv_s)
            outs["dsa_idx_p"].append(ki_p)
            outs["dsa_idx_s"].append(ki_s)
        elif kind == 1:
            gdn_args = (gdn_conv_w[j], gdn_a_log[j], gdn_dt_bias[j], gdn_norm_g[j])
            zero_conv = jnp.zeros((bp, B_CONV - 1, B_CONV_CH), xp.dtype)
            zero_rec = jnp.zeros((bp, B_V_HEADS, B_DK, B_DV), xp.dtype)
            op, conv_p, rec_p = gdn_mixer(zp, *gdn_args, zero_conv, zero_rec)
            os_, conv_s, rec_s = gdn_mixer(zs, *gdn_args, state_gdn_conv[:, j], state_gdn_rec[:, j])
            outs["gdn_conv_p"].append(conv_p)
            outs["gdn_conv_s"].append(conv_s)
            outs["gdn_rec_p"].append(rec_p)
            outs["gdn_rec_s"].append(rec_s)
        else:
            nsa_args = (nsa_cmp_pe[j], nsa_cmp_w1[j], nsa_cmp_w2[j])
            op, cmp_p, slc_p, win_p = nsa_prompt(zp, nsa_cmp_pe, nsa_cmp_w1, nsa_cmp_w2, j)
            os_, cmp_s, slc_s, win_s = nsa_sample(zs, *nsa_args, cache_nsa_cmp, cache_nsa_slc, cache_nsa_win[:, j],
                                                  page_table, j)
            outs["nsa_cmp_p"].append(cmp_p)
            outs["nsa_cmp_s"].append(cmp_s)
            outs["nsa_slc_p"].append(slc_p)
            outs["nsa_slc_s"].append(slc_s)
            outs["nsa_win_p"].append(win_p)
            outs["nsa_win_s"].append(win_s)
        yp = matmul(op.reshape(n_p, -1), w_out, j).reshape(bp, tp, D_MODEL)
        ys = matmul(os_.reshape(n_s, -1), w_out, j).reshape(bs, ts, D_MODEL)
        xp, hp = post_norm_modulate(xp, yp, mp[:, 2], ln_g[i, 0], ln_b[i, 0], mp[:, 4], mp[:, 3])
        xs, hs = post_norm_modulate(xs, ys, ms[:, 2], ln_g[i, 0], ln_b[i, 0], ms[:, 4], ms[:, 3])
        h_all = jnp.concatenate([hp.reshape(n_p, D_MODEL), hs.reshape(n_s, D_MODEL)], axis=0)
        y_all = moe(h_all, i, *moe_w)
        zero = jnp.zeros((bp + bs, D_MODEL), jnp.float32)
        xp, _ = post_norm_modulate(xp, y_all[:n_p].reshape(bp, tp, D_MODEL), mp[:, 5], ln_g[i, 1], ln_b[i, 1],
                                   zero[:bp], zero[:bp])
        xs, _ = post_norm_modulate(xs, y_all[n_p:].reshape(bs, ts, D_MODEL), ms[:, 5], ln_g[i, 1], ln_b[i, 1],
                                   zero[bp:], zero[bp:])
    st = lambda k, ax: jnp.stack(outs[k], axis=ax)
    return (xp, xs,
            st("dsa_kv_p", 2), st("dsa_kv_s", 2), st("dsa_idx_p", 2), st("dsa_idx_s", 2),
            st("gdn_conv_p", 1), st("gdn_conv_s", 1), st("gdn_rec_p", 1), st("gdn_rec_s", 1),
            st("nsa_cmp_p", 2), st("nsa_cmp_s", 2), st("nsa_slc_p", 2), st("nsa_slc_s", 2),
            st("nsa_win_p", 1), st("nsa_win_s", 1))
```

```python
import functools
import math

import jax
import jax.numpy as jnp
import numpy as np
from jax import lax
from jax.experimental import pallas as pl
from jax.experimental.pallas import tpu as pltpu

D_MODEL = 2048
DEPTH = 4
PAST_LEN = 16384
PAGE_SIZE = 128
N_MIXERS = 3
ALPHA = (2.0 * DEPTH) ** 0.25
LN_EPS = 1e-5
NORM_EPS = 1e-6
NEG = -1e30
FORCE = 1e9
HEAD_DIM = 128
ATTN_SCALE = HEAD_DIM ** -0.5

A_HEADS = D_MODEL // HEAD_DIM
A_KV = 4
A_REP = A_HEADS // A_KV
A_IDX_HEADS = 16
A_IDX_DIM = 64
A_TOPK = 256
A_QBLK = 128

B_QK_HEADS = 16
B_V_HEADS = 32
B_DK = 128
B_DV = 128
B_CONV = 4
B_CHUNK = 64
B_CONV_CH = 2 * B_QK_HEADS * B_DK + B_V_HEADS * B_DV

C_HEADS = D_MODEL // HEAD_DIM
C_KV = 4
C_REP = C_HEADS // C_KV
CMP_BLK = 32
CMP_STRIDE = 16
SLC_BLK = 64
N_SEL = 16
WINDOW = 512
C_QBLK = 32
WIN_QBLK = 128

N_EXPERTS = 32
TOP_K = 4
D_FF = 2048
SWIGLU_LIMIT = 7.0
SWIGLU_ALPHA = 1.702

VMEM_LIMIT_BYTES = 56 * 1024 * 1024
MOE_ROW_BLK = 256
MOE_FF_TILE = 1024
MOE_OUT_TILE = 1024


def _params(*sem):
    return pltpu.CompilerParams(dimension_semantics=sem, vmem_limit_bytes=VMEM_LIMIT_BYTES)


def _bf16(x):
    return x.astype(jnp.bfloat16)


def _mm_kernel(x_ref, w_ref, o_ref):
    o_ref[...] = jnp.dot(_bf16(x_ref[...]), _bf16(w_ref[...]), preferred_element_type=jnp.float32)


def _mm_bias_kernel(x_ref, w_ref, b_ref, o_ref):
    o_ref[...] = jnp.dot(_bf16(x_ref[...]), _bf16(w_ref[...]), preferred_element_type=jnp.float32) + b_ref[...]


def _mm_mod_kernel(x_ref, sc_ref, sh_ref, w_ref, o_ref):
    h = x_ref[0] * (1.0 + sc_ref[0]) + sh_ref[0]
    o_ref[0] = jnp.dot(_bf16(h), _bf16(w_ref[...]), preferred_element_type=jnp.float32)


def matmul(x, w, layer, b=None, tm=512, tn=512):
    m, k = x.shape
    n = w.shape[2]
    tm = min(tm, m)
    tn = min(tn, n)
    grid = (pl.cdiv(m, tm), pl.cdiv(n, tn))
    in_specs = [pl.BlockSpec((tm, k), lambda i, j: (i, 0)), pl.BlockSpec((None, k, tn), lambda i, j: (layer, 0, j))]
    args = [x, w]
    body = _mm_kernel
    if b is not None:
        in_specs.append(pl.BlockSpec((None, 1, tn), lambda i, j: (layer, 0, j)))
        args.append(b.reshape(b.shape[0], 1, n))
        body = _mm_bias_kernel
    return pl.pallas_call(
        body, grid=grid, in_specs=in_specs,
        out_specs=pl.BlockSpec((tm, tn), lambda i, j: (i, j)),
        out_shape=jax.ShapeDtypeStruct((m, n), jnp.float32),
        compiler_params=_params("parallel", "parallel"))(*args)


def matmul_modulated(x, scale, shift, w, layer, tm=512, tn=512):
    bsz, t, k = x.shape
    n = w.shape[2]
    tm = min(tm, t)
    tn = min(tn, n)
    grid = (bsz, pl.cdiv(t, tm), pl.cdiv(n, tn))
    mod_spec = pl.BlockSpec((1, 1, k), lambda b, i, j: (b, 0, 0))
    return pl.pallas_call(
        _mm_mod_kernel, grid=grid,
        in_specs=[pl.BlockSpec((1, tm, k), lambda b, i, j: (b, i, 0)), mod_spec, mod_spec,
                  pl.BlockSpec((None, k, tn), lambda b, i, j: (layer, 0, j))],
        out_specs=pl.BlockSpec((1, tm, tn), lambda b, i, j: (b, i, j)),
        out_shape=jax.ShapeDtypeStruct((bsz, t, n), jnp.float32),
        compiler_params=_params("parallel", "parallel", "parallel"))(
            x, scale.reshape(bsz, 1, k), shift.reshape(bsz, 1, k), w)


def _postnorm_kernel(x_ref, y_ref, gate_ref, g_ref, b_ref, sc_ref, sh_ref, xo_ref, ho_ref):
    z = ALPHA * x_ref[0] + gate_ref[0] * y_ref[0]
    mu = jnp.mean(z, axis=-1, keepdims=True)
    zc = z - mu
    var = jnp.mean(zc * zc, axis=-1, keepdims=True)
    xn = zc * lax.rsqrt(var + LN_EPS) * g_ref[...] + b_ref[...]
    xo_ref[0] = xn
    ho_ref[0] = xn * (1.0 + sc_ref[0]) + sh_ref[0]


def post_norm_modulate(x, y, gate, g, b, scale, shift, tm=256):
    bsz, t, d = x.shape
    tm = min(tm, t)
    row = pl.BlockSpec((1, tm, d), lambda bi, i: (bi, i, 0))
    per_seq = pl.BlockSpec((1, 1, d), lambda bi, i: (bi, 0, 0))
    shared = pl.BlockSpec((1, d), lambda bi, i: (0, 0))
    return pl.pallas_call(
        _postnorm_kernel, grid=(bsz, t // tm),
        in_specs=[row, row, per_seq, shared, shared, per_seq, per_seq],
        out_specs=[row, row],
        out_shape=[jax.ShapeDtypeStruct(x.shape, jnp.float32)] * 2,
        compiler_params=_params("parallel", "parallel"))(
            x, y, gate.reshape(bsz, 1, d), g.reshape(1, d), b.reshape(1, d),
            scale.reshape(bsz, 1, d), shift.reshape(bsz, 1, d))


def _moe_in_kernel(blk_e_ref, n_used_ref, x_ref, wg_ref, wl_ref, bg_ref, bl_ref, o_ref):
    used = pl.program_id(1) < n_used_ref[0]

    @pl.when(used)
    def _():
        x = x_ref[...]
        glu = jnp.dot(x, _bf16(wg_ref[...]), preferred_element_type=jnp.float32) + bg_ref[...]
        lin = jnp.dot(x, _bf16(wl_ref[...]), preferred_element_type=jnp.float32) + bl_ref[...]
        glu = jnp.minimum(glu, SWIGLU_LIMIT)
        lin = jnp.clip(lin, -SWIGLU_LIMIT, SWIGLU_LIMIT)
        o_ref[...] = (glu * jax.nn.sigmoid(SWIGLU_ALPHA * glu) * (lin + 1.0)).astype(o_ref.dtype)

    @pl.when(jnp.logical_not(used))
    def _():
        o_ref[...] = jnp.zeros(o_ref.shape, o_ref.dtype)


def _moe_out_kernel(blk_e_ref, n_used_ref, h_ref, w_ref, b_ref, o_ref):
    used = pl.program_id(1) < n_used_ref[0]

    @pl.when(used)
    def _():
        y = jnp.dot(h_ref[...], _bf16(w_ref[...]), preferred_element_type=jnp.float32) + b_ref[...]
        o_ref[...] = y.astype(o_ref.dtype)

    @pl.when(jnp.logical_not(used))
    def _():
        o_ref[...] = jnp.zeros(o_ref.shape, o_ref.dtype)


def moe_ffn(xs, blk_expert, n_used, layer, w_in, b_in, w_out, b_out):
    r = xs.shape[0]
    n_blk = r // MOE_ROW_BLK
    n_ff = D_FF // MOE_FF_TILE
    tm, tf, tn = MOE_ROW_BLK, MOE_FF_TILE, MOE_OUT_TILE
    b_in4 = b_in.reshape(DEPTH, N_EXPERTS, 1, 2 * D_FF)
    b_out4 = b_out.reshape(DEPTH, N_EXPERTS, 1, D_MODEL)
    row_blk = lambda j, m, be, nu: (jnp.minimum(m, nu[0] - 1), 0)
    act = pl.pallas_call(
        _moe_in_kernel,
        grid_spec=pltpu.PrefetchScalarGridSpec(
            num_scalar_prefetch=2, grid=(n_ff, n_blk),
            in_specs=[
                pl.BlockSpec((tm, D_MODEL), row_blk),
                pl.BlockSpec((None, None, D_MODEL, tf), lambda j, m, be, nu: (layer, be[m], 0, j)),
                pl.BlockSpec((None, None, D_MODEL, tf), lambda j, m, be, nu: (layer, be[m], 0, n_ff + j)),
                pl.BlockSpec((None, None, 1, tf), lambda j, m, be, nu: (layer, be[m], 0, j)),
                pl.BlockSpec((None, None, 1, tf), lambda j, m, be, nu: (layer, be[m], 0, n_ff + j)),
            ],
            out_specs=pl.BlockSpec((tm, tf), lambda j, m, be, nu: (m, j))),
        out_shape=jax.ShapeDtypeStruct((r, D_FF), jnp.bfloat16),
        compiler_params=_params("arbitrary", "arbitrary"))(blk_expert, n_used, xs, w_in, w_in, b_in4, b_in4)
    return pl.pallas_call(
        _moe_out_kernel,
        grid_spec=pltpu.PrefetchScalarGridSpec(
            num_scalar_prefetch=2, grid=(D_MODEL // tn, n_blk),
            in_specs=[
                pl.BlockSpec((tm, D_FF), row_blk),
                pl.BlockSpec((None, None, D_FF, tn), lambda j, m, be, nu: (layer, be[m], 0, j)),
                pl.BlockSpec((None, None, 1, tn), lambda j, m, be, nu: (layer, be[m], 0, j)),
            ],
            out_specs=pl.BlockSpec((tm, tn), lambda j, m, be, nu: (m, j))),
        out_shape=jax.ShapeDtypeStruct((r, D_MODEL), jnp.bfloat16),
        compiler_params=_params("arbitrary", "arbitrary"))(blk_expert, n_used, act, w_out, b_out4)


def moe(x, layer, w_router, b_router, w_in, b_in, w_out, b_out):
    n_tok = x.shape[0]
    n_asg = n_tok * TOP_K
    logits = jnp.dot(x, w_router[layer], precision=lax.Precision.HIGHEST) + b_router[layer]
    top_val, top_idx = lax.top_k(logits, TOP_K)
    gate = jax.nn.softmax(top_val, axis=-1)
    e_flat = top_idx.reshape(-1)
    order = jnp.argsort(e_flat)
    e_sorted = e_flat[order]
    counts = jnp.bincount(e_flat, length=N_EXPERTS)
    blocks = (counts + MOE_ROW_BLK - 1) // MOE_ROW_BLK
    blk_end = jnp.cumsum(blocks)
    dest = (blk_end - blocks)[e_sorted] * MOE_ROW_BLK + jnp.arange(n_asg) - (jnp.cumsum(counts) - counts)[e_sorted]
    n_blk = -(-n_asg // MOE_ROW_BLK) + N_EXPERTS
    row_tok = jnp.zeros((n_blk * MOE_ROW_BLK,), jnp.int32).at[dest].set((order // TOP_K).astype(jnp.int32))
    n_used = blk_end[-1:].astype(jnp.int32)
    blk_expert = jnp.searchsorted(blk_end, jnp.minimum(jnp.arange(n_blk), n_used - 1), side="right").astype(jnp.int32)
    xs = _bf16(x)[row_tok]
    ys = moe_ffn(xs, blk_expert, n_used, layer, w_in, b_in, w_out, b_out)
    dest_of_asg = jnp.zeros((n_asg,), jnp.int32).at[order].set(dest.astype(jnp.int32))
    return jnp.sum(ys[dest_of_asg].reshape(n_tok, TOP_K, D_MODEL).astype(jnp.float32) * gate[..., None], axis=1)


def split_cols(z, sizes):
    return jnp.split(z, np.cumsum(sizes)[:-1].tolist(), axis=-1)


def rms_norm(x, g):
    return x * lax.rsqrt(jnp.mean(jnp.square(x), axis=-1, keepdims=True) + NORM_EPS) * g


def l2_normalize(x):
    return x * lax.rsqrt(jnp.sum(jnp.square(x), axis=-1, keepdims=True) + NORM_EPS)


def masked_softmax(logits, mask):
    return jax.nn.softmax(jnp.where(mask, logits, NEG), axis=-1)


def gather_pages(pool, page_table, layer):
    rows = pool[page_table, :, layer]
    return rows.reshape(rows.shape[0], -1, *rows.shape[3:])


def gather_rows(pool, page_table, layer, pos):
    b = jnp.arange(pos.shape[0]).reshape((-1,) + (1,) * (pos.ndim - 1))
    phys = page_table[b, pos // PAGE_SIZE]
    return pool[phys, pos % PAGE_SIZE, layer]


def gather_group_rows(pool, page_table, layer, pos, kv_j):
    b = jnp.arange(pos.shape[0])[:, None, None, None, None]
    g = jnp.arange(C_KV)[None, None, :, None, None]
    phys = page_table[b, pos // PAGE_SIZE]
    return pool[phys, pos % PAGE_SIZE, layer, kv_j, g]


def dsa_split(z):
    B, T, _ = z.shape
    q, k, v, qi, ki, wi = split_cols(z, [A_HEADS * HEAD_DIM, A_KV * HEAD_DIM, A_KV * HEAD_DIM,
                                         A_IDX_HEADS * A_IDX_DIM, A_IDX_DIM, A_IDX_HEADS])
    q = q.reshape(B, T, A_KV, A_REP, HEAD_DIM)
    kv = jnp.stack([k.reshape(B, T, A_KV, HEAD_DIM), v.reshape(B, T, A_KV, HEAD_DIM)], axis=2)
    qi = qi.reshape(B, T, A_IDX_HEADS, A_IDX_DIM)
    return q, kv, qi, ki, wi * A_IDX_HEADS ** -0.5


def indexer_scores(qi, wi, ki, q_pos, k_pos):
    s = jax.nn.relu(jnp.einsum("bthd,bsd->bths", qi, ki).astype(jnp.float32) * A_IDX_DIM ** -0.5)
    score = jnp.einsum("bths,bth->bts", s, wi.astype(jnp.float32))
    return jnp.where(k_pos[None, None, :] <= q_pos[None, :, None], score, NEG)


def gathered_attend(q, kv_sel, ok):
    logits = jnp.einsum("btgrd,btkgd->btgrk", q, kv_sel[:, :, :, 0]).astype(jnp.float32) * ATTN_SCALE
    p = masked_softmax(logits, ok[:, :, None, None, :])
    return jnp.einsum("btgrk,btkgd->btgrd", p.astype(q.dtype), kv_sel[:, :, :, 1])


INT32_MIN = -2 ** 31
LANES = 128
PAGES_PER_STEP = 4
STEP_KEYS = PAGES_PER_STEP * PAGE_SIZE


def _sortable_key(x):
    bits = pltpu.bitcast(x, jnp.int32)
    return jnp.where(bits < 0, bits ^ 0x7FFFFFFF, bits)


def _kth_largest_key(key_ref, n_chunks, chunk, k, rows):
    def count_ge(cand):
        def body(c, acc):
            off = pl.multiple_of(c * chunk, chunk)
            for j in range(chunk // LANES):
                keys = key_ref[:, pl.ds(off + j * LANES, LANES)]
                acc = acc + jnp.where(keys >= cand, 1.0, 0.0)
            return acc

        acc = lax.fori_loop(0, n_chunks, body, jnp.zeros((rows, LANES), jnp.float32))
        return jnp.sum(acc, axis=-1, keepdims=True)

    def bit_step(b, ans):
        cand = ans + lax.shift_left(jnp.int32(1), 31 - b)
        return jnp.where(count_ge(cand) >= k, cand, ans)

    return lax.fori_loop(0, 32, bit_step, jnp.full((rows, 1), INT32_MIN, jnp.int32))


def _topk_bias(key_ref, bias_ref, thr, n_chunks, chunk, k, rows, row_pos):
    tri = (lax.broadcasted_iota(jnp.int32, (LANES, LANES), 0)
           <= lax.broadcasted_iota(jnp.int32, (LANES, LANES), 1)).astype(jnp.bfloat16)
    col = lax.broadcasted_iota(jnp.int32, (rows, LANES), 1)

    def count_gt(c, acc):
        off = pl.multiple_of(c * chunk, chunk)
        for j in range(chunk // LANES):
            acc = acc + jnp.where(key_ref[:, pl.ds(off + j * LANES, LANES)] > thr, 1.0, 0.0)
        return acc

    n_gt = jnp.sum(lax.fori_loop(0, n_chunks, count_gt, jnp.zeros((rows, LANES), jnp.float32)), axis=-1, keepdims=True)
    need = k - n_gt

    def body(c, run):
        off = pl.multiple_of(c * chunk, chunk)
        for j in range(chunk // LANES):
            keys = key_ref[:, pl.ds(off + j * LANES, LANES)]
            eq = keys == thr
            eq_f = jnp.where(eq, 1.0, 0.0)
            incl = jnp.dot(eq_f.astype(jnp.bfloat16), tri, preferred_element_type=jnp.float32)
            sel = (keys > thr) | (eq & (run + incl - eq_f < need))
            ok = sel & (off + j * LANES + col <= row_pos)
            bias_ref[:, pl.ds(off + j * LANES, LANES)] = jnp.where(ok, 0.0, NEG)
            run = run + incl[:, LANES - 1:LANES]
        return run

    lax.fori_loop(0, n_chunks, body, jnp.zeros((rows, 1), jnp.float32))


def _stack_heads(q_ref, g, rep):
    return jnp.concatenate([q_ref[0, :, (g * rep + r) * HEAD_DIM:(g * rep + r + 1) * HEAD_DIM] for r in range(rep)], axis=0)


def _flash_group(qg, kt_ref, v_ref, g, lo, hi, chunk, bias_fn):
    rows = qg.shape[0]

    def body(c, carry):
        m, l, acc = carry
        off = pl.multiple_of(c * chunk, chunk)
        kt = kt_ref[0, g * HEAD_DIM:(g + 1) * HEAD_DIM, pl.ds(off, chunk)]
        s = jnp.dot(qg, kt, preferred_element_type=jnp.float32) * ATTN_SCALE + bias_fn(off)
        m_new = jnp.maximum(m, jnp.max(s, axis=-1, keepdims=True))
        p = jnp.exp(s - m_new)
        alpha = jnp.exp(m - m_new)
        l = alpha * l + jnp.sum(p, axis=-1, keepdims=True)
        v = v_ref[0, pl.ds(off, chunk), g * HEAD_DIM:(g + 1) * HEAD_DIM]
        acc = alpha * acc + jnp.dot(_bf16(p), v, preferred_element_type=jnp.float32)
        return m_new, l, acc

    init = (jnp.full((rows, 1), NEG, jnp.float32), jnp.zeros((rows, 1), jnp.float32),
            jnp.zeros((rows, HEAD_DIM), jnp.float32))
    _, l, acc = lax.fori_loop(lo, hi, body, init)
    return acc / l


DSA_TQ = 128
DSA_TK = 512


def _dsa_prompt_kernel(q_ref, qi_ref, wi_ref, kit_ref, kt_ref, v_ref, o_ref, key_ref, bias_ref, *, topk):
    tq, tk = DSA_TQ, DSA_TK
    q_start = pl.program_id(1) * tq
    n_chunks = (q_start + tq + tk - 1) // tk
    row_pos = q_start + lax.broadcasted_iota(jnp.int32, (tq, tk), 0)
    col = lax.broadcasted_iota(jnp.int32, (tq, tk), 1)
    wi = wi_ref[0]

    def score_chunk(c, carry):
        off = pl.multiple_of(c * tk, tk)
        kit = kit_ref[0, :, pl.ds(off, tk)]
        acc = jnp.zeros((tq, tk), jnp.float32)
        for h in range(A_IDX_HEADS):
            s = jnp.dot(qi_ref[0, h], kit, preferred_element_type=jnp.float32)
            acc = acc + jnp.maximum(s, 0.0) * wi[:, h:h + 1]
        key_ref[:, pl.ds(off, tk)] = _sortable_key(jnp.where(off + col <= row_pos, acc, NEG))
        return carry

    lax.fori_loop(0, n_chunks, score_chunk, 0)
    thr = _kth_largest_key(key_ref, n_chunks, tk, topk, tq)
    _topk_bias(key_ref, bias_ref, thr, n_chunks, tk, topk, tq, row_pos[:, :LANES])
    bias_fn = lambda off: jnp.concatenate([bias_ref[:, pl.ds(off, tk)]] * A_REP, axis=0)
    for g in range(A_KV):
        out = _flash_group(_stack_heads(q_ref, g, A_REP), kt_ref, v_ref, g, 0, n_chunks, tk, bias_fn)
        for r in range(A_REP):
            h = g * A_REP + r
            o_ref[0, :, h * HEAD_DIM:(h + 1) * HEAD_DIM] = out[r * tq:(r + 1) * tq].astype(o_ref.dtype)


def dsa_prompt(z):
    B, T, _ = z.shape
    topk = min(A_TOPK, T // 4)
    nq, nk = A_HEADS * HEAD_DIM, A_KV * HEAD_DIM
    q = _bf16(z[..., :nq])
    k, v = z[..., nq:nq + nk], z[..., nq + nk:nq + 2 * nk]
    off = nq + 2 * nk
    qi = _bf16(z[..., off:off + A_IDX_HEADS * A_IDX_DIM]).reshape(B, T, A_IDX_HEADS, A_IDX_DIM).transpose(0, 2, 1, 3)
    off += A_IDX_HEADS * A_IDX_DIM
    ki = z[..., off:off + A_IDX_DIM]
    wi = z[..., off + A_IDX_DIM:] * (A_IDX_HEADS ** -0.5 * A_IDX_DIM ** -0.5)
    kit = _bf16(ki).transpose(0, 2, 1)
    kt = _bf16(k).transpose(0, 2, 1)
    tq = DSA_TQ
    o = pl.pallas_call(
        functools.partial(_dsa_prompt_kernel, topk=topk),
        grid=(B, T // tq),
        in_specs=[
            pl.BlockSpec((1, tq, nq), lambda b, i: (b, i, 0)),
            pl.BlockSpec((1, A_IDX_HEADS, tq, A_IDX_DIM), lambda b, i: (b, 0, i, 0)),
            pl.BlockSpec((1, tq, A_IDX_HEADS), lambda b, i: (b, i, 0)),
            pl.BlockSpec((1, A_IDX_DIM, T), lambda b, i: (b, 0, 0)),
            pl.BlockSpec((1, nk, T), lambda b, i: (b, 0, 0)),
            pl.BlockSpec((1, T, nk), lambda b, i: (b, 0, 0)),
        ],
        out_specs=pl.BlockSpec((1, tq, nq), lambda b, i: (b, i, 0)),
        out_shape=jax.ShapeDtypeStruct((B, T, nq), jnp.bfloat16),
        scratch_shapes=[pltpu.VMEM((tq, T), jnp.int32), pltpu.VMEM((tq, T), jnp.float32)],
        compiler_params=_params("parallel", "arbitrary"))(q, qi, wi, kit, kt, _bf16(v))
    kv = jnp.stack([k.reshape(B, T, A_KV, HEAD_DIM), v.reshape(B, T, A_KV, HEAD_DIM)], axis=2)
    return o, kv, ki


def _nt_dot(a, b):
    return lax.dot_general(a, b, (((1,), (1,)), ((), ())), preferred_element_type=jnp.float32)


def _dsa_sample_score_kernel(pt_ref, qi_ref, wi_ref, *rest, layer, n_pages, t_new):
    pool_refs, (new_ref, o_ref) = rest[:PAGES_PER_STEP], rest[PAGES_PER_STEP:]
    p = pl.program_id(1)

    def scores(ki):
        s = jnp.maximum(_nt_dot(qi_ref[0], _bf16(ki)), 0.0) * wi_ref[0]
        acc = s[0:t_new]
        for h in range(1, A_IDX_HEADS):
            acc = acc + s[h * t_new:(h + 1) * t_new]
        k_pos = p * STEP_KEYS + lax.broadcasted_iota(jnp.int32, (t_new, STEP_KEYS), 1)
        q_pos = n_pages * PAGE_SIZE + lax.broadcasted_iota(jnp.int32, (t_new, STEP_KEYS), 0)
        o_ref[0] = jnp.where(k_pos <= q_pos, acc, NEG)

    @pl.when(p < n_pages // PAGES_PER_STEP)
    def _():
        scores(jnp.concatenate([r[:, layer, :] for r in pool_refs], axis=0))

    @pl.when(p == n_pages // PAGES_PER_STEP)
    def _():
        scores(new_ref[0])


def _dsa_sample_select_kernel(s_ref, bias_ref, key_ref, *, topk, chunk, q_pos0):
    rows, n_keys = key_ref.shape
    n_chunks = n_keys // chunk
    for c in range(n_chunks):
        key_ref[:, c * chunk:(c + 1) * chunk] = _sortable_key(s_ref[0, :, c * chunk:(c + 1) * chunk])
    thr = _kth_largest_key(key_ref, n_chunks, chunk, topk, rows)
    q_pos = q_pos0 + lax.broadcasted_iota(jnp.int32, (rows, LANES), 0)
    _topk_bias(key_ref, bias_ref.at[0], thr, n_chunks, chunk, topk, rows, q_pos)


def _dsa_sample_attend_kernel(pt_ref, q_ref, bias_ref, *rest, n_pages, t_new):
    pool_refs, (new_ref, o_ref, m_ref, l_ref, acc_ref) = rest[:PAGES_PER_STEP], rest[PAGES_PER_STEP:]
    p = pl.program_id(1)

    @pl.when(p == 0)
    def _():
        _init_online_softmax(m_ref, l_ref, acc_ref)

    def attend(kv_of):
        bias = jnp.concatenate([bias_ref[0]] * A_REP, axis=0)
        _paged_attend_step(q_ref, kv_of, lambda g: bias, m_ref, l_ref, acc_ref, A_KV, A_REP * t_new)

    @pl.when(p < n_pages // PAGES_PER_STEP)
    def _():
        attend(lambda e, g: jnp.concatenate([r[:, e, g, :] for r in pool_refs], axis=0))

    @pl.when(p == n_pages // PAGES_PER_STEP)
    def _():
        attend(lambda e, g: new_ref[0, :, e, g, :])
        o_ref[0] = (acc_ref[...] / l_ref[...]).astype(o_ref.dtype)


def dsa_sample(z, pool_kv, pool_idx, page_table, layer):
    B, T, _ = z.shape
    n_pages = page_table.shape[1]
    past = n_pages * PAGE_SIZE
    topk = min(A_TOPK, (past + T) // 4)
    nq, nk = A_HEADS * HEAD_DIM, A_KV * HEAD_DIM
    n_layers = pool_kv.shape[2]
    k, v = z[..., nq:nq + nk], z[..., nq + nk:nq + 2 * nk]
    off = nq + 2 * nk
    head_major = lambda a, d: a.reshape(B, T, -1, d).transpose(0, 2, 1, 3).reshape(B, -1, d)
    q = head_major(_bf16(z[..., :nq]), HEAD_DIM)
    qi = head_major(_bf16(z[..., off:off + A_IDX_HEADS * A_IDX_DIM]), A_IDX_DIM)
    off += A_IDX_HEADS * A_IDX_DIM
    ki = z[..., off:off + A_IDX_DIM]
    wi = head_major(z[..., off + A_IDX_DIM:] * (A_IDX_HEADS ** -0.5 * A_IDX_DIM ** -0.5), 1)
    assert n_pages % PAGES_PER_STEP == 0 and T <= STEP_KEYS
    n_steps = n_pages // PAGES_PER_STEP + 1
    n_keys = n_steps * STEP_KEYS
    pad_rows = lambda a: jnp.pad(a, ((0, 0), (0, STEP_KEYS - T)) + ((0, 0),) * (a.ndim - 2))
    page_of = lambda b, p, pt, i: pt[b, jnp.minimum(p * PAGES_PER_STEP + i, n_pages - 1)]
    per_b = lambda b, p, pt: (b, 0, 0)
    scores = pl.pallas_call(
        functools.partial(_dsa_sample_score_kernel, layer=layer, n_pages=n_pages, t_new=T),
        grid_spec=pltpu.PrefetchScalarGridSpec(
            num_scalar_prefetch=1, grid=(B, n_steps),
            in_specs=[pl.BlockSpec((1, A_IDX_HEADS * T, A_IDX_DIM), per_b),
                      pl.BlockSpec((1, A_IDX_HEADS * T, 1), per_b)]
            + [pl.BlockSpec((None, PAGE_SIZE, n_layers, A_IDX_DIM), lambda b, p, pt, i=i: (page_of(b, p, pt, i), 0, 0, 0))
               for i in range(PAGES_PER_STEP)]
            + [pl.BlockSpec((1, STEP_KEYS, A_IDX_DIM), per_b)],
            out_specs=pl.BlockSpec((1, T, STEP_KEYS), lambda b, p, pt: (b, 0, p))),
        out_shape=jax.ShapeDtypeStruct((B, T, n_keys), jnp.float32),
        compiler_params=_params("parallel", "arbitrary"))(page_table, qi, wi, *[pool_idx] * PAGES_PER_STEP, pad_rows(ki))
    chunk = STEP_KEYS
    bias = pl.pallas_call(
        functools.partial(_dsa_sample_select_kernel, topk=topk, chunk=chunk, q_pos0=past),
        grid=(B,),
        in_specs=[pl.BlockSpec((1, T, n_keys), lambda b: (b, 0, 0))],
        out_specs=pl.BlockSpec((1, T, n_keys), lambda b: (b, 0, 0)),
        out_shape=jax.ShapeDtypeStruct((B, T, n_keys), jnp.float32),
        scratch_shapes=[pltpu.VMEM((T, n_keys), jnp.int32)],
        compiler_params=_params("parallel"))(scores)
    kv_new = jnp.stack([k.reshape(B, T, A_KV, HEAD_DIM), v.reshape(B, T, A_KV, HEAD_DIM)], axis=2)
    o = pl.pallas_call(
        functools.partial(_dsa_sample_attend_kernel, n_pages=n_pages, t_new=T),
        grid_spec=pltpu.PrefetchScalarGridSpec(
            num_scalar_prefetch=1, grid=(B, n_steps),
            in_specs=[pl.BlockSpec((1, A_HEADS * T, HEAD_DIM), per_b),
                      pl.BlockSpec((1, T, STEP_KEYS), lambda b, p, pt: (b, 0, p))]
            + [pl.BlockSpec((None, PAGE_SIZE, None, 2, A_KV, HEAD_DIM),
                            lambda b, p, pt, i=i: (page_of(b, p, pt, i), 0, layer, 0, 0, 0)) for i in range(PAGES_PER_STEP)]
            + [pl.BlockSpec((1, STEP_KEYS, 2, A_KV, HEAD_DIM), lambda b, p, pt: (b, 0, 0, 0, 0))],
            out_specs=pl.BlockSpec((1, A_HEADS * T, HEAD_DIM), per_b),
            scratch_shapes=[pltpu.VMEM((A_HEADS * T, 1), jnp.float32), pltpu.VMEM((A_HEADS * T, 1), jnp.float32),
                            pltpu.VMEM((A_HEADS * T, HEAD_DIM), jnp.float32)]),
        out_shape=jax.ShapeDtypeStruct((B, A_HEADS * T, HEAD_DIM), jnp.bfloat16),
        compiler_params=_params("parallel", "arbitrary"))(
            page_table, q, bias, *[pool_kv] * PAGES_PER_STEP, pad_rows(kv_new))
    o = o.reshape(B, A_HEADS, T, HEAD_DIM).transpose(0, 2, 1, 3).reshape(B, T, nq)
    return o, kv_new, ki


TRI_PACK = 4
TRI_GROUPS = 2


def _dot_bf16x3(a, b):
    a_hi, b_hi = _bf16(a), _bf16(b)
    a_lo, b_lo = _bf16(a - a_hi.astype(jnp.float32)), _bf16(b - b_hi.astype(jnp.float32))
    dot = functools.partial(jnp.dot, preferred_element_type=jnp.float32)
    return dot(a_hi, b_hi) + (dot(a_hi, b_lo) + dot(a_lo, b_hi))


def _unit_lower_inverse_kernel(a_ref, o_ref):
    c = a_ref.shape[-1]
    size = TRI_PACK * c
    eye = jnp.where(lax.broadcasted_iota(jnp.int32, (size, size), 0) == lax.broadcasted_iota(jnp.int32, (size, size), 1), 1.0, 0.0)
    zero = jnp.zeros((c, c), jnp.float32)
    for i in range(TRI_GROUPS):
        n = jnp.concatenate([jnp.concatenate([-a_ref[i, j] if jj == j else zero for jj in range(TRI_PACK)], axis=1)
                             for j in range(TRI_PACK)], axis=0)
        t, p = eye + n, n
        for _ in range((c - 1).bit_length() - 1):
            p = _dot_bf16x3(p, p)
            t = t + _dot_bf16x3(t, p)
        for j in range(TRI_PACK):
            o_ref[i, j] = t[j * c:(j + 1) * c, j * c:(j + 1) * c]


def unit_lower_inverse(a):
    c = a.shape[-1]
    n = math.prod(a.shape[:-2])
    per_step = TRI_GROUPS * TRI_PACK
    assert n % per_step == 0
    blk = (TRI_GROUPS, TRI_PACK, c, c)
    out = pl.pallas_call(
        _unit_lower_inverse_kernel, grid=(n // per_step,),
        in_specs=[pl.BlockSpec(blk, lambda i: (i, 0, 0, 0))],
        out_specs=pl.BlockSpec(blk, lambda i: (i, 0, 0, 0)),
        out_shape=jax.ShapeDtypeStruct((n // TRI_PACK, TRI_PACK, c, c), jnp.float32),
        compiler_params=_params("parallel"))(a.reshape(n // TRI_PACK, TRI_PACK, c, c))
    return out.reshape(a.shape)
def chunk_gated_delta(q, k, v, g, beta, s0):
    B, T, H, DK = q.shape
    C = B_CHUNK
    n = -(-T // C)
    pad = n * C - T

    def chunks(a):
        a = jnp.pad(a, [(0, 0), (0, pad)] + [(0, 0)] * (a.ndim - 2))
        a = a.reshape(B, n, C, *a.shape[2:])
        return jnp.moveaxis(a, (1, 3), (0, 2))

    qc, kc, vc, bc = chunks(q) * DK ** -0.5, chunks(k), chunks(v), chunks(beta)
    gc = jnp.cumsum(chunks(g), axis=-1)
    incl = jnp.tril(jnp.ones((C, C), bool))
    strict = jnp.tril(jnp.ones((C, C), bool), -1)
    decay = jnp.exp(jnp.where(incl, gc[..., :, None] - gc[..., None, :], NEG))
    kb = kc * bc[..., None]
    a_mat = jnp.where(strict, jnp.einsum("...id,...jd->...ij", kb, kc) * decay, 0.0)
    t_mat = unit_lower_inverse(a_mat)
    u = t_mat @ (vc * bc[..., None])
    w = t_mat @ (kb * jnp.exp(gc)[..., None])
    attn = jnp.einsum("...id,...jd->...ij", qc, kc) * decay

    def step(s, inp):
        q_i, k_i, u_i, w_i, g_i, attn_i = inp
        v_new = u_i - w_i @ s
        o = (q_i * jnp.exp(g_i)[..., None]) @ s + attn_i @ v_new
        g_last = g_i[..., -1]
        s = s * jnp.exp(g_last)[..., None, None] + jnp.einsum(
            "bhcd,bhce->bhde", k_i * jnp.exp(g_last[..., None] - g_i)[..., None], v_new)
        return s, o

    s_final, o = lax.scan(step, s0, (qc, kc, u, w, gc, attn))
    o = jnp.moveaxis(o, (0, 2), (1, 3)).reshape(B, n * C, H, -1)[:, :T]
    return o, s_final


def gdn_mixer(zin, conv_w, a_log, dt_bias, norm_g, conv_state, rec_state):
    B, T, _ = zin.shape
    qkv, z, b_raw, a_raw = split_cols(zin, [B_CONV_CH, B_V_HEADS * B_DV, B_V_HEADS, B_V_HEADS])
    x_ext = jnp.concatenate([conv_state, qkv], axis=1)
    conv = sum(x_ext[:, i:i + T] * conv_w[i][None, None, :] for i in range(B_CONV))
    q, k, v = split_cols(jax.nn.silu(conv), [B_QK_HEADS * B_DK, B_QK_HEADS * B_DK, B_V_HEADS * B_DV])
    rep = B_V_HEADS // B_QK_HEADS
    q = jnp.repeat(l2_normalize(q.reshape(B, T, B_QK_HEADS, B_DK)), rep, axis=2)
    k = jnp.repeat(l2_normalize(k.reshape(B, T, B_QK_HEADS, B_DK)), rep, axis=2)
    v = v.reshape(B, T, B_V_HEADS, B_DV)
    beta = jax.nn.sigmoid(b_raw)
    g = -jnp.exp(a_log) * jax.nn.softplus(a_raw + dt_bias)
    o, s_new = chunk_gated_delta(q, k, v, g, beta, rec_state)
    o = rms_norm(o, norm_g) * jax.nn.silu(z.reshape(B, T, B_V_HEADS, B_DV))
    return o.reshape(B, T, B_V_HEADS * B_DV), x_ext[:, -(B_CONV - 1):], s_new


def nsa_split(z):
    B, T, _ = z.shape
    parts = split_cols(z, [C_HEADS * HEAD_DIM] + [C_KV * HEAD_DIM] * 6 + [3 * C_HEADS])
    q = parts[0].reshape(B, T, C_KV, C_REP, HEAD_DIM)
    kvs = [p.reshape(B, T, C_KV, HEAD_DIM) for p in parts[1:7]]
    cmp_kv = jnp.stack(kvs[0:2], axis=2)
    slc_kv = jnp.stack(kvs[2:4], axis=2)
    win_kv = jnp.stack(kvs[4:6], axis=2)
    gate = jax.nn.sigmoid(parts[7]).reshape(B, T, C_KV, C_REP, 3)
    return q, cmp_kv, slc_kv, win_kv, gate


def compress(kv, pe, w1, w2):
    B, L = kv.shape[:2]
    n_cmp = (L - CMP_BLK) // CMP_STRIDE + 1
    idx = jnp.arange(n_cmp)[:, None] * CMP_STRIDE + jnp.arange(CMP_BLK)[None, :]
    blk = kv[:, idx] + pe[None, None, :, :, None, :]
    blk = jnp.moveaxis(blk, 2, 4).reshape(B, n_cmp, 2, C_KV, CMP_BLK * HEAD_DIM)
    hid = jax.nn.gelu(jnp.einsum("bnegi,eio->bnego", blk, w1))
    return jnp.einsum("bnegi,eio->bnego", hid, w2)


def cmp_attend(q, kv_cmp, q_pos):
    n_cmp = kv_cmp.shape[1]
    logits = jnp.einsum("btgrd,bngd->btgrn", q, kv_cmp[:, :, 0]).astype(jnp.float32) * ATTN_SCALE
    vis = ((jnp.arange(n_cmp) * CMP_STRIDE + CMP_BLK - 1)[None, :] <= q_pos[:, None])[None, :, None, None, :]
    p = jnp.where(vis, masked_softmax(logits, vis), 0.0)
    return jnp.einsum("btgrn,bngd->btgrd", p.astype(q.dtype), kv_cmp[:, :, 1]), p


def select_blocks(p_cmp, q_pos, L):
    n_cmp = p_cmp.shape[-1]
    n_slc = -(-L // SLC_BLK)
    cmp_start = jnp.arange(n_cmp)[:, None] * CMP_STRIDE
    slc_start = jnp.arange(n_slc)[None, :] * SLC_BLK
    overlap = ((cmp_start < slc_start + SLC_BLK) & (cmp_start + CMP_BLK > slc_start)).astype(jnp.float32)
    imp = jnp.einsum("btgrn,nm->btgm", p_cmp, overlap, precision=lax.Precision.HIGHEST)
    cur = (q_pos // SLC_BLK)[None, :, None, None]
    blk = jnp.arange(n_slc)
    imp = jnp.where((blk == 0) | (blk == cur), FORCE, imp)
    imp = jnp.where(blk <= cur, imp, NEG)
    _, idx = lax.top_k(imp, min(N_SEL, n_slc))
    return idx, idx <= cur


def slc_attend(q, kb, vb, blk_idx, blk_ok, q_pos):
    B, T, G, NS, SB, dh = kb.shape
    key_pos = blk_idx[..., None] * SLC_BLK + jnp.arange(SLC_BLK)
    ok = blk_ok[..., None] & (key_pos <= q_pos[None, :, None, None, None])
    logits = jnp.einsum("btgrd,btgnsd->btgrns", q, kb).astype(jnp.float32) * ATTN_SCALE
    p = masked_softmax(logits.reshape(B, T, G, C_REP, NS * SB), ok.reshape(B, T, G, 1, NS * SB))
    return jnp.einsum("btgrk,btgkd->btgrd", p.astype(q.dtype), vb.reshape(B, T, G, NS * SB, dh))


def win_attend(q, kv, q_pos, k_pos):
    diff = q_pos[:, :, None] - k_pos[:, None, :]
    ok = (k_pos[:, None, :] >= 0) & (diff >= 0) & (diff < WINDOW)
    logits = jnp.einsum("bnqgrd,bnkgd->bnqgrk", q, kv[:, :, :, 0]).astype(jnp.float32) * ATTN_SCALE
    p = masked_softmax(logits, ok[None, :, :, None, None, :])
    return jnp.einsum("bnqgrk,bnkgd->bnqgrd", p.astype(q.dtype), kv[:, :, :, 1])


def to_blocks(k):
    B, L = k.shape[:2]
    n = -(-L // SLC_BLK)
    k = jnp.pad(k, ((0, 0), (0, n * SLC_BLK - L), (0, 0), (0, 0)))
    return k.reshape(B, n, SLC_BLK, C_KV, HEAD_DIM).transpose(0, 3, 1, 2, 4)


def nsa_combine(gate, o_cmp, o_slc, o_win):
    o = gate[..., 0:1] * o_cmp + gate[..., 1:2] * o_slc + gate[..., 2:3] * o_win
    B, T = o.shape[:2]
    return o.reshape(B, T, C_HEADS * HEAD_DIM)


def nsa_compress(kv, n_cmp, pe, w1, w2, layer):
    B = kv.shape[0]
    n_row = n_cmp + 1
    half = CMP_STRIDE * HEAD_DIM
    x = kv[:, :n_row * CMP_STRIDE].reshape(B, n_row, CMP_STRIDE, 2, C_KV, HEAD_DIM)
    x = x.transpose(3, 0, 4, 1, 2, 5).reshape(2, B * C_KV * n_row, half)
    w1l = w1[layer]
    w_halves = jnp.concatenate([w1l[:, :half], w1l[:, half:]], axis=-1)
    pe_rows = jnp.broadcast_to(pe[layer].transpose(1, 0, 2).reshape(2, 1, CMP_BLK * HEAD_DIM), (2, 8, CMP_BLK * HEAD_DIM))
    outs = []
    for e in range(2):
        ab = matmul(x[e], w_halves, e).reshape(B, C_KV, n_row, 2, HEAD_DIM)
        c = matmul(pe_rows[e], w1l, e)[0]
        hid = jax.nn.gelu(ab[:, :, :n_cmp, 0] + ab[:, :, 1:, 1] + c)
        outs.append(matmul(hid.reshape(B * C_KV * n_cmp, HEAD_DIM), w2[layer], e).reshape(B, C_KV, n_cmp, HEAD_DIM))
    return outs


NSA_TQ = 128
NSA_TK = 512
SLC_SHIFT = SLC_BLK.bit_length() - 1


def _nsa_prompt_kernel(q_ref, gl_ref, kct_ref, vc_ref, kst_ref, vs_ref, kwt_ref, vw_ref, o_ref, key_ref, bias_ref, *,
                       n_cmp, n_slc):
    tq, tk = NSA_TQ, NSA_TK
    rows = C_REP * tq
    q_start = pl.program_id(1) * tq
    n_chunks = (q_start + tq + tk - 1) // tk
    win_lo = jnp.maximum(q_start - (WINDOW - 1), 0) // tk
    n_pad = kct_ref.shape[2]
    iota = lambda shape, ax: lax.broadcasted_iota(jnp.int32, shape, ax)
    tile = lambda a: jnp.concatenate([a] * C_REP, axis=0)

    n_c = iota((tq, n_pad), 1)
    vis = tile((n_c * CMP_STRIDE + CMP_BLK - 1 <= q_start + iota((tq, n_pad), 0)) & (n_c < n_cmp))
    n_o, m_o = iota((n_pad, LANES), 0) * CMP_STRIDE, iota((n_pad, LANES), 1) * SLC_BLK
    overlap = jnp.where((n_o < m_o + SLC_BLK) & (n_o + CMP_BLK > m_o) & (iota((n_pad, LANES), 1) < n_slc), 1.0, 0.0)
    cur = jnp.right_shift(q_start + iota((tq, LANES), 0), SLC_SHIFT)
    blk = iota((tq, LANES), 1)
    o_cmp = []
    for g in range(C_KV):
        qg = _stack_heads(q_ref, g, C_REP)
        s = jnp.dot(qg, kct_ref[0, g * HEAD_DIM:(g + 1) * HEAD_DIM, :], preferred_element_type=jnp.float32) * ATTN_SCALE
        s = jnp.where(vis, s, NEG)
        e = jnp.exp(s - jnp.max(s, axis=-1, keepdims=True))
        p = jnp.where(vis, e / jnp.sum(e, axis=-1, keepdims=True), 0.0)
        o_cmp.append(jnp.dot(_bf16(p), vc_ref[0, :, g * HEAD_DIM:(g + 1) * HEAD_DIM], preferred_element_type=jnp.float32))
        p_grp = sum(p[r * tq:(r + 1) * tq] for r in range(C_REP))
        imp = jnp.dot(p_grp, overlap, preferred_element_type=jnp.float32, precision=lax.Precision.HIGHEST)
        imp = jnp.where((blk == 0) | (blk == cur), FORCE, imp)
        key_ref[g * tq:(g + 1) * tq, :] = _sortable_key(jnp.where(blk <= cur, imp, NEG))
    k_sel = min(N_SEL, n_slc)
    thr = _kth_largest_key(key_ref, 1, LANES, k_sel, rows)
    _topk_bias(key_ref, bias_ref, thr, 1, LANES, k_sel, rows, tile(cur))

    t_k = q_start + iota((tq, tk), 0)
    s_k = iota((tq, tk), 1)
    blk_e, s_e = iota((LANES, tk), 0), iota((LANES, tk), 1)
    gate = jax.nn.sigmoid(gl_ref[0])

    def win_bias(off):
        d = t_k - (off + s_k)
        return tile(jnp.where((d >= 0) & (d < WINDOW), 0.0, NEG))

    for g in range(C_KV):
        qg = _stack_heads(q_ref, g, C_REP)
        sel = _bf16(jnp.where(bias_ref[g * tq:(g + 1) * tq, :] == 0.0, 1.0, 0.0))

        def slc_bias(off, sel=sel):
            expand = _bf16(jnp.where(blk_e == jnp.right_shift(off + s_e, SLC_SHIFT), 1.0, 0.0))
            hit = jnp.dot(sel, expand, preferred_element_type=jnp.float32)
            return tile(jnp.where((hit > 0.5) & (off + s_k <= t_k), 0.0, NEG))

        o_slc = _flash_group(qg, kst_ref, vs_ref, g, 0, n_chunks, tk, slc_bias)
        o_win = _flash_group(qg, kwt_ref, vw_ref, g, win_lo, n_chunks, tk, win_bias)
        for r in range(C_REP):
            h = g * C_REP + r
            rs = slice(r * tq, (r + 1) * tq)
            o = (gate[:, 3 * h:3 * h + 1] * o_cmp[g][rs] + gate[:, 3 * h + 1:3 * h + 2] * o_slc[rs]
                 + gate[:, 3 * h + 2:3 * h + 3] * o_win[rs])
            o_ref[0, :, h * HEAD_DIM:(h + 1) * HEAD_DIM] = o.astype(o_ref.dtype)


def nsa_prompt(z, pe, w1, w2, layer):
    B, T, _ = z.shape
    nq, nk = C_HEADS * HEAD_DIM, C_KV * HEAD_DIM
    q = _bf16(z[..., :nq])
    branch = [z[..., nq + 2 * i * nk:nq + 2 * (i + 1) * nk] for i in range(3)]
    gl = z[..., nq + 6 * nk:]
    as_kv = lambda a: a.reshape(B, T, 2, C_KV, HEAD_DIM)
    n_cmp = (T - CMP_BLK) // CMP_STRIDE + 1
    k_cmp, v_cmp = nsa_compress(as_kv(branch[0]), n_cmp, pe, w1, w2, layer)
    n_pad = -(-n_cmp // LANES) * LANES
    pad = ((0, 0), (0, 0), (0, n_pad - n_cmp), (0, 0))
    kct = _bf16(jnp.pad(k_cmp, pad)).transpose(0, 1, 3, 2).reshape(B, nk, n_pad)
    vc = _bf16(jnp.pad(v_cmp, pad)).transpose(0, 2, 1, 3).reshape(B, n_pad, nk)
    kt = lambda a: _bf16(a[..., :nk]).transpose(0, 2, 1)
    tq = NSA_TQ
    full_t = lambda shape: pl.BlockSpec(shape, lambda b, i: (b, 0, 0))
    o = pl.pallas_call(
        functools.partial(_nsa_prompt_kernel, n_cmp=n_cmp, n_slc=-(-T // SLC_BLK)),
        grid=(B, T // tq),
        in_specs=[
            pl.BlockSpec((1, tq, nq), lambda b, i: (b, i, 0)),
            pl.BlockSpec((1, tq, 3 * C_HEADS), lambda b, i: (b, i, 0)),
            full_t((1, nk, n_pad)), full_t((1, n_pad, nk)),
            full_t((1, nk, T)), full_t((1, T, nk)), full_t((1, nk, T)), full_t((1, T, nk)),
        ],
        out_specs=pl.BlockSpec((1, tq, nq), lambda b, i: (b, i, 0)),
        out_shape=jax.ShapeDtypeStruct((B, T, nq), jnp.bfloat16),
        scratch_shapes=[pltpu.VMEM((C_REP * tq, LANES), jnp.int32), pltpu.VMEM((C_REP * tq, LANES), jnp.float32)],
        compiler_params=_params("parallel", "arbitrary"))(
            q, gl, kct, vc, kt(branch[1]), _bf16(branch[1][..., nk:]), kt(branch[2]), _bf16(branch[2][..., nk:]))
    return o, as_kv(branch[0]), as_kv(branch[1]), as_kv(branch[2])[:, -WINDOW:]


def _page_copy_kernel(pt_ref, pool_ref, o_ref):
    o_ref[0] = pool_ref[...]


def gather_pages_pallas(pool, page_table, layer):
    B, n_pages = page_table.shape
    row = pool.shape[3:]
    return pl.pallas_call(
        _page_copy_kernel,
        grid_spec=pltpu.PrefetchScalarGridSpec(
            num_scalar_prefetch=1, grid=(B, n_pages),
            in_specs=[pl.BlockSpec((None, PAGE_SIZE, None) + row, lambda b, p, pt: (pt[b, p], 0, layer, 0, 0, 0))],
            out_specs=pl.BlockSpec((1, PAGE_SIZE) + row, lambda b, p, pt: (b, p, 0, 0, 0))),
        out_shape=jax.ShapeDtypeStruct((B, n_pages * PAGE_SIZE) + row, pool.dtype),
        compiler_params=_params("parallel", "arbitrary"))(page_table, pool)


def _nsa_sample_cmpwin_kernel(q_ref, kct_ref, vc_ref, win_ref, ocmp_ref, owin_ref, sel_ref, key_ref, *,
                              n_cmp, n_slc, q_pos0, win_pos0, t_new):
    rows_g = C_REP * t_new
    nk = C_KV * HEAD_DIM
    n_pad, n_blk, n_win = kct_ref.shape[2], key_ref.shape[1], win_ref.shape[1]
    iota = lambda shape, ax: lax.broadcasted_iota(jnp.int32, shape, ax)
    tile = lambda a: jnp.concatenate([a] * C_REP, axis=0)
    n_c = iota((t_new, n_pad), 1)
    vis = tile((n_c * CMP_STRIDE + CMP_BLK - 1 <= q_pos0 + iota((t_new, n_pad), 0)) & (n_c < n_cmp))
    n_o, m_o = iota((n_pad, n_blk), 0) * CMP_STRIDE, iota((n_pad, n_blk), 1) * SLC_BLK
    overlap = jnp.where((n_o < m_o + SLC_BLK) & (n_o + CMP_BLK > m_o) & (iota((n_pad, n_blk), 1) < n_slc), 1.0, 0.0)
    cur = jnp.right_shift(q_pos0 + iota((t_new, n_blk), 0), SLC_SHIFT)
    blk = iota((t_new, n_blk), 1)
    d = q_pos0 + iota((t_new, n_win), 0) - (win_pos0 + iota((t_new, n_win), 1))
    win_ok = tile((win_pos0 + iota((t_new, n_win), 1) >= 0) & (d >= 0) & (d < WINDOW))
    for g in range(C_KV):
        rs = slice(g * rows_g, (g + 1) * rows_g)
        qg = q_ref[0, rs]
        s = jnp.dot(qg, kct_ref[0, g * HEAD_DIM:(g + 1) * HEAD_DIM, :], preferred_element_type=jnp.float32) * ATTN_SCALE
        s = jnp.where(vis, s, NEG)
        e = jnp.exp(s - jnp.max(s, axis=-1, keepdims=True))
        p = jnp.where(vis, e / jnp.sum(e, axis=-1, keepdims=True), 0.0)
        ocmp_ref[0, rs] = jnp.dot(_bf16(p), vc_ref[0, :, g * HEAD_DIM:(g + 1) * HEAD_DIM], preferred_element_type=jnp.float32)
        p_grp = sum(p[r * t_new:(r + 1) * t_new] for r in range(C_REP))
        imp = jnp.dot(p_grp, overlap, preferred_element_type=jnp.float32, precision=lax.Precision.HIGHEST)
        imp = jnp.where((blk == 0) | (blk == cur), FORCE, imp)
        key_ref[g * t_new:(g + 1) * t_new, :] = _sortable_key(jnp.where(blk <= cur, imp, NEG))
        kw = _bf16(win_ref[0, :, g * HEAD_DIM:(g + 1) * HEAD_DIM])
        vw = _bf16(win_ref[0, :, nk + g * HEAD_DIM:nk + (g + 1) * HEAD_DIM])
        sw = jnp.where(win_ok, _nt_dot(qg, kw) * ATTN_SCALE, NEG)
        ew = jnp.exp(sw - jnp.max(sw, axis=-1, keepdims=True))
        pw = ew / jnp.sum(ew, axis=-1, keepdims=True)
        owin_ref[0, rs] = jnp.dot(_bf16(pw), vw, preferred_element_type=jnp.float32)
    k_sel = min(N_SEL, n_slc)
    rows = C_KV * t_new
    thr = _kth_largest_key(key_ref, 1, n_blk, k_sel, rows)
    _topk_bias(key_ref, sel_ref.at[0], thr, 1, n_blk, k_sel, rows, jnp.concatenate([cur[:, :LANES]] * C_KV, axis=0))


def _paged_attend_step(q_ref, kv_of, bias_of_group, m_ref, l_ref, acc_ref, n_groups, rows_g):
    for g in range(n_groups):
        rs = slice(g * rows_g, (g + 1) * rows_g)
        s = _nt_dot(q_ref[0, rs], _bf16(kv_of(0, g))) * ATTN_SCALE + bias_of_group(g)
        m = m_ref[rs]
        m_new = jnp.maximum(m, jnp.max(s, axis=-1, keepdims=True))
        pr = jnp.exp(s - m_new)
        alpha = jnp.exp(m - m_new)
        l_ref[rs] = alpha * l_ref[rs] + jnp.sum(pr, axis=-1, keepdims=True)
        acc_ref[rs] = alpha * acc_ref[rs] + jnp.dot(_bf16(pr), _bf16(kv_of(1, g)), preferred_element_type=jnp.float32)
        m_ref[rs] = m_new


def _init_online_softmax(m_ref, l_ref, acc_ref):
    m_ref[...] = jnp.full(m_ref.shape, NEG, jnp.float32)
    l_ref[...] = jnp.zeros(l_ref.shape, jnp.float32)
    acc_ref[...] = jnp.zeros(acc_ref.shape, jnp.float32)


def _nsa_sample_slc_kernel(pt_ref, q_ref, sel_ref, gate_ref, ocmp_ref, owin_ref, *rest, n_pages, t_new):
    pool_refs, (new_ref, o_ref, m_ref, l_ref, acc_ref) = rest[:PAGES_PER_STEP], rest[PAGES_PER_STEP:]
    p = pl.program_id(1)
    n_blk = sel_ref.shape[2]

    @pl.when(p == 0)
    def _():
        _init_online_softmax(m_ref, l_ref, acc_ref)

    def attend(kv_of):
        iota = lambda shape, ax: lax.broadcasted_iota(jnp.int32, shape, ax)
        causal = p * STEP_KEYS + iota((t_new, STEP_KEYS), 1) <= n_pages * PAGE_SIZE + iota((t_new, STEP_KEYS), 0)
        key_blk = (STEP_KEYS // SLC_BLK) * p + jnp.right_shift(iota((n_blk, STEP_KEYS), 1), SLC_SHIFT)
        expand = _bf16(jnp.where(iota((n_blk, STEP_KEYS), 0) == key_blk, 1.0, 0.0))

        def bias_of_group(g):
            sel = _bf16(jnp.where(sel_ref[0, g * t_new:(g + 1) * t_new, :] == 0.0, 1.0, 0.0))
            hit = jnp.dot(sel, expand, preferred_element_type=jnp.float32)
            return jnp.concatenate([jnp.where((hit > 0.5) & causal, 0.0, NEG)] * C_REP, axis=0)

        _paged_attend_step(q_ref, kv_of, bias_of_group, m_ref, l_ref, acc_ref, C_KV, C_REP * t_new)

    @pl.when(p < n_pages // PAGES_PER_STEP)
    def _():
        attend(lambda e, g: jnp.concatenate([r[:, e, g, :] for r in pool_refs], axis=0))

    @pl.when(p == n_pages // PAGES_PER_STEP)
    def _():
        attend(lambda e, g: new_ref[0, :, e, g, :])
        gate = jax.nn.sigmoid(gate_ref[0])
        o = gate[:, 0:1] * ocmp_ref[0] + gate[:, 1:2] * (acc_ref[...] / l_ref[...]) + gate[:, 2:3] * owin_ref[0]
        o_ref[0] = o.astype(o_ref.dtype)


def nsa_sample(z, pe, w1, w2, pool_cmp, pool_slc, win_buf, page_table, layer):
    B, T, _ = z.shape
    n_pages = page_table.shape[1]
    past = n_pages * PAGE_SIZE
    nq, nk = C_HEADS * HEAD_DIM, C_KV * HEAD_DIM
    head_major = lambda a, d: a.reshape(B, T, -1, d).transpose(0, 2, 1, 3).reshape(B, -1, d)
    as_kv = lambda a: a.reshape(B, -1, 2, C_KV, HEAD_DIM)
    q = head_major(_bf16(z[..., :nq]), HEAD_DIM)
    branch = [z[..., nq + 2 * i * nk:nq + 2 * (i + 1) * nk] for i in range(3)]
    gl = head_major(z[..., nq + 6 * nk:], 3)
    cmp_rows = gather_pages_pallas(pool_cmp, page_table, layer)
    n_cmp = (past + T - CMP_BLK) // CMP_STRIDE + 1
    if (n_cmp + 1) * CMP_STRIDE > past:
        cmp_rows = jnp.concatenate([cmp_rows, as_kv(branch[0])], axis=1)
    k_cmp, v_cmp = nsa_compress(cmp_rows, n_cmp, pe, w1, w2, layer)
    n_pad = -(-n_cmp // LANES) * LANES
    pad = ((0, 0), (0, 0), (0, n_pad - n_cmp), (0, 0))
    kct = _bf16(jnp.pad(k_cmp, pad)).transpose(0, 1, 3, 2).reshape(B, nk, n_pad)
    vc = _bf16(jnp.pad(v_cmp, pad)).transpose(0, 2, 1, 3).reshape(B, n_pad, nk)
    n_slc = -(-(past + T) // SLC_BLK)
    n_blk = -(-n_slc // LANES) * LANES
    win_all = jnp.concatenate([win_buf.reshape(B, WINDOW, 2 * nk), branch[2]], axis=1)
    n_win = -(-(WINDOW + T) // LANES) * LANES
    win_pad = jnp.pad(win_all, ((0, 0), (0, n_win - WINDOW - T), (0, 0)))
    per_b = lambda shape: pl.BlockSpec(shape, lambda b: (b, 0, 0))
    hm_rows = C_HEADS * T
    o_cmp, o_win, sel = pl.pallas_call(
        functools.partial(_nsa_sample_cmpwin_kernel, n_cmp=n_cmp, n_slc=n_slc, q_pos0=past, win_pos0=past - WINDOW, t_new=T),
        grid=(B,),
        in_specs=[per_b((1, hm_rows, HEAD_DIM)), per_b((1, nk, n_pad)), per_b((1, n_pad, nk)), per_b((1, n_win, 2 * nk))],
        out_specs=[per_b((1, hm_rows, HEAD_DIM)), per_b((1, hm_rows, HEAD_DIM)), per_b((1, C_KV * T, n_blk))],
        out_shape=[jax.ShapeDtypeStruct((B, hm_rows, HEAD_DIM), jnp.float32)] * 2
        + [jax.ShapeDtypeStruct((B, C_KV * T, n_blk), jnp.float32)],
        scratch_shapes=[pltpu.VMEM((C_KV * T, n_blk), jnp.int32)],
        compiler_params=_params("parallel"))(q, kct, vc, win_pad)
    per_b3 = lambda shape: pl.BlockSpec(shape, lambda b, p, pt: (b, 0, 0))
    assert n_pages % PAGES_PER_STEP == 0 and T <= STEP_KEYS
    new_pad = jnp.pad(as_kv(branch[1]), ((0, 0), (0, STEP_KEYS - T), (0, 0), (0, 0), (0, 0)))
    kv_row = (2, C_KV, HEAD_DIM)
    page_of = lambda b, p, pt, i: pt[b, jnp.minimum(p * PAGES_PER_STEP + i, n_pages - 1)]
    o = pl.pallas_call(
        functools.partial(_nsa_sample_slc_kernel, n_pages=n_pages, t_new=T),
        grid_spec=pltpu.PrefetchScalarGridSpec(
            num_scalar_prefetch=1, grid=(B, n_pages // PAGES_PER_STEP + 1),
            in_specs=[per_b3((1, hm_rows, HEAD_DIM)), per_b3((1, C_KV * T, n_blk)), per_b3((1, hm_rows, 3)),
                      per_b3((1, hm_rows, HEAD_DIM)), per_b3((1, hm_rows, HEAD_DIM))]
            + [pl.BlockSpec((None, PAGE_SIZE, None) + kv_row, lambda b, p, pt, i=i: (page_of(b, p, pt, i), 0, layer, 0, 0, 0))
               for i in range(PAGES_PER_STEP)]
            + [pl.BlockSpec((1, STEP_KEYS) + kv_row, lambda b, p, pt: (b, 0, 0, 0, 0))],
            out_specs=per_b3((1, hm_rows, HEAD_DIM)),
            scratch_shapes=[pltpu.VMEM((hm_rows, 1), jnp.float32), pltpu.VMEM((hm_rows, 1), jnp.float32),
                            pltpu.VMEM((hm_rows, HEAD_DIM), jnp.float32)]),
        out_shape=jax.ShapeDtypeStruct((B, hm_rows, HEAD_DIM), jnp.bfloat16),
        compiler_params=_params("parallel", "arbitrary"))(
            page_table, q, sel, gl, o_cmp, o_win, *[pool_slc] * PAGES_PER_STEP, new_pad)
    o = o.reshape(B, C_HEADS, T, HEAD_DIM).transpose(0, 2, 1, 3).reshape(B, T, nq)
    return o, as_kv(branch[0]), as_kv(branch[1]), as_kv(win_all)[:, -WINDOW:]


def kernel(x_prompt, x_sample, cache_dsa_kv, cache_dsa_idx, state_gdn_conv, state_gdn_rec, cache_nsa_cmp, cache_nsa_slc, cache_nsa_win, page_table, c_prompt, c_sample, ada_w, ada_b, ln_g, ln_b, dsa_w_in, dsa_w_out, gdn_w_in, gdn_conv_w, gdn_a_log, gdn_dt_bias, gdn_norm_g, gdn_w_out, nsa_w_in, nsa_cmp_pe, nsa_cmp_w1, nsa_cmp_w2, nsa_w_out, moe_w_router, moe_b_router, moe_w_in, moe_b_in, moe_w_out, moe_b_out):
    xp, xs = x_prompt, x_sample
    bp, tp, _ = xp.shape
    bs, ts, _ = xs.shape
    n_p, n_s = bp * tp, bs * ts
    outs = {k: [] for k in ("dsa_kv_p", "dsa_kv_s", "dsa_idx_p", "dsa_idx_s", "gdn_conv_p", "gdn_conv_s", "gdn_rec_p",
                            "gdn_rec_s", "nsa_cmp_p", "nsa_cmp_s", "nsa_slc_p", "nsa_slc_s", "nsa_win_p", "nsa_win_s")}
    moe_w = (moe_w_router, moe_b_router, moe_w_in, moe_b_in, moe_w_out, moe_b_out)
    c_all = jax.nn.silu(jnp.concatenate([c_prompt, c_sample], axis=0))
    for i in range(DEPTH):
        kind, j = i % N_MIXERS, i // N_MIXERS
        mod = matmul(c_all, ada_w, i, ada_b).reshape(bp + bs, 6, D_MODEL)
        mp, ms = mod[:bp], mod[bp:]
        if kind == 0:
            w_in, w_out = dsa_w_in, dsa_w_out
        elif kind == 1:
            w_in, w_out = gdn_w_in, gdn_w_out
        else:
            w_in, w_out = nsa_w_in, nsa_w_out
        zp = matmul_modulated(xp, mp[:, 1], mp[:, 0], w_in, j)
        hs = xs * (1.0 + ms[:, 1][:, None, :]) + ms[:, 0][:, None, :]
        zs = matmul(hs.reshape(n_s, D_MODEL), w_in, j).reshape(bs, ts, -1)
        if kind == 0:
            op, kv_p, ki_p = dsa_prompt(zp)
            os_, kv_s, ki_s = dsa_sample(zs, cache_dsa_kv, cache_dsa_idx, page_table, j)
            outs["dsa_kv_p"].append(kv_p)
            outs["dsa_kv_s"].append(kv_s)
            outs["dsa_idx_p"].append(ki_p)
            outs["dsa_idx_s"].append(ki_s)
        elif kind == 1:
            gdn_args = (gdn_conv_w[j], gdn_a_log[j], gdn_dt_bias[j], gdn_norm_g[j])
            zero_conv = jnp.zeros((bp, B_CONV - 1, B_CONV_CH), xp.dtype)
            zero_rec = jnp.zeros((bp, B_V_HEADS, B_DK, B_DV), xp.dtype)
            op, conv_p, rec_p = gdn_mixer(zp, *gdn_args, zero_conv, zero_rec)
            os_, conv_s, rec_s = gdn_mixer(zs, *gdn_args, state_gdn_conv[:, j], state_gdn_rec[:, j])
            outs["gdn_conv_p"].append(conv_p)
            outs["gdn_conv_s"].append(conv_s)
            outs["gdn_rec_p"].append(rec_p)
            outs["gdn_rec_s"].append(rec_s)
        else:
            nsa_args = (nsa_cmp_pe, nsa_cmp_w1, nsa_cmp_w2)
            op, cmp_p, slc_p, win_p = nsa_prompt(zp, *nsa_args, j)
            os_, cmp_s, slc_s, win_s = nsa_sample(zs, *nsa_args, cache_nsa_cmp, cache_nsa_slc, cache_nsa_win[:, j],
                                                  page_table, j)
            outs["nsa_cmp_p"].append(cmp_p)
            outs["nsa_cmp_s"].append(cmp_s)
            outs["nsa_slc_p"].append(slc_p)
            outs["nsa_slc_s"].append(slc_s)
            outs["nsa_win_p"].append(win_p)
            outs["nsa_win_s"].append(win_s)
        yp = matmul(op.reshape(n_p, -1), w_out, j).reshape(bp, tp, D_MODEL)
        ys = matmul(os_.reshape(n_s, -1), w_out, j).reshape(bs, ts, D_MODEL)
        xp, hp = post_norm_modulate(xp, yp, mp[:, 2], ln_g[i, 0], ln_b[i, 0], mp[:, 4], mp[:, 3])
        xs, hs = post_norm_modulate(xs, ys, ms[:, 2], ln_g[i, 0], ln_b[i, 0], ms[:, 4], ms[:, 3])
        h_all = jnp.concatenate([hp.reshape(n_p, D_MODEL), hs.reshape(n_s, D_MODEL)], axis=0)
        y_all = moe(h_all, i, *moe_w)
        zero = jnp.zeros((bp + bs, D_MODEL), jnp.float32)
        xp, _ = post_norm_modulate(xp, y_all[:n_p].reshape(bp, tp, D_MODEL), mp[:, 5], ln_g[i, 1], ln_b[i, 1],
                                   zero[:bp], zero[:bp])
        xs, _ = post_norm_modulate(xs, y_all[n_p:].reshape(bs, ts, D_MODEL), ms[:, 5], ln_g[i, 1], ln_b[i, 1],
                                   zero[bp:], zero[bp:])
    st = lambda k, ax: jnp.stack(outs[k], axis=ax)
    return (xp, xs,
            st("dsa_kv_p", 2), st("dsa_kv_s", 2), st("dsa_idx_p", 2), st("dsa_idx_s", 2),
            st("gdn_conv_p", 1), st("gdn_conv_s", 1), st("gdn_rec_p", 1), st("gdn_rec_s", 1),
            st("nsa_cmp_p", 2), st("nsa_cmp_s", 2), st("nsa_slc_p", 2), st("nsa_slc_s", 2),
            st("nsa_win_p", 1), st("nsa_win_s", 1))
```

```python
import functools
import math

import jax
import jax.numpy as jnp
import numpy as np
from jax import lax
from jax.experimental import pallas as pl
from jax.experimental.pallas import tpu as pltpu

D_MODEL = 2048
DEPTH = 4
PAST_LEN = 16384
PAGE_SIZE = 128
N_MIXERS = 3
ALPHA = (2.0 * DEPTH) ** 0.25
LN_EPS = 1e-5
NORM_EPS = 1e-6
NEG = -1e30
FORCE = 1e9
HEAD_DIM = 128
ATTN_SCALE = HEAD_DIM ** -0.5

A_HEADS = D_MODEL // HEAD_DIM
A_KV = 4
A_REP = A_HEADS // A_KV
A_IDX_HEADS = 16
A_IDX_DIM = 64
A_TOPK = 256
A_QBLK = 128

B_QK_HEADS = 16
B_V_HEADS = 32
B_DK = 128
B_DV = 128
B_CONV = 4
B_CHUNK = 64
B_CONV_CH = 2 * B_QK_HEADS * B_DK + B_V_HEADS * B_DV

C_HEADS = D_MODEL // HEAD_DIM
C_KV = 4
C_REP = C_HEADS // C_KV
CMP_BLK = 32
CMP_STRIDE = 16
SLC_BLK = 64
N_SEL = 16
WINDOW = 512
C_QBLK = 32
WIN_QBLK = 128

N_EXPERTS = 32
TOP_K = 4
D_FF = 2048
SWIGLU_LIMIT = 7.0
SWIGLU_ALPHA = 1.702

VMEM_LIMIT_BYTES = 56 * 1024 * 1024
MOE_ROW_BLK = 256
MOE_FF_TILE = 1024
MOE_OUT_TILE = 1024


def _params(*sem):
    return pltpu.CompilerParams(dimension_semantics=sem, vmem_limit_bytes=VMEM_LIMIT_BYTES)


def _bf16(x):
    return x.astype(jnp.bfloat16)


def _mm_kernel(x_ref, w_ref, o_ref):
    o_ref[...] = jnp.dot(_bf16(x_ref[...]), _bf16(w_ref[...]), preferred_element_type=jnp.float32)


def _mm_bias_kernel(x_ref, w_ref, b_ref, o_ref):
    o_ref[...] = jnp.dot(_bf16(x_ref[...]), _bf16(w_ref[...]), preferred_element_type=jnp.float32) + b_ref[...]


def _mm_mod_kernel(x_ref, sc_ref, sh_ref, w_ref, o_ref):
    h = x_ref[0] * (1.0 + sc_ref[0]) + sh_ref[0]
    o_ref[0] = jnp.dot(_bf16(h), _bf16(w_ref[...]), preferred_element_type=jnp.float32)


def matmul(x, w, layer, b=None, tm=512, tn=512):
    m, k = x.shape
    n = w.shape[2]
    tm = min(tm, m)
    tn = min(tn, n)
    grid = (pl.cdiv(m, tm), pl.cdiv(n, tn))
    in_specs = [pl.BlockSpec((tm, k), lambda i, j: (i, 0)), pl.BlockSpec((None, k, tn), lambda i, j: (layer, 0, j))]
    args = [x, w]
    body = _mm_kernel
    if b is not None:
        in_specs.append(pl.BlockSpec((None, 1, tn), lambda i, j: (layer, 0, j)))
        args.append(b.reshape(b.shape[0], 1, n))
        body = _mm_bias_kernel
    return pl.pallas_call(
        body, grid=grid, in_specs=in_specs,
        out_specs=pl.BlockSpec((tm, tn), lambda i, j: (i, j)),
        out_shape=jax.ShapeDtypeStruct((m, n), jnp.float32),
        compiler_params=_params("parallel", "parallel"))(*args)


def matmul_modulated(x, scale, shift, w, layer, tm=512, tn=512):
    bsz, t, k = x.shape
    n = w.shape[2]
    tm = min(tm, t)
    tn = min(tn, n)
    grid = (bsz, pl.cdiv(t, tm), pl.cdiv(n, tn))
    mod_spec = pl.BlockSpec((1, 1, k), lambda b, i, j: (b, 0, 0))
    return pl.pallas_call(
        _mm_mod_kernel, grid=grid,
        in_specs=[pl.BlockSpec((1, tm, k), lambda b, i, j: (b, i, 0)), mod_spec, mod_spec,
                  pl.BlockSpec((None, k, tn), lambda b, i, j: (layer, 0, j))],
        out_specs=pl.BlockSpec((1, tm, tn), lambda b, i, j: (b, i, j)),
        out_shape=jax.ShapeDtypeStruct((bsz, t, n), jnp.float32),
        compiler_params=_params("parallel", "parallel", "parallel"))(
            x, scale.reshape(bsz, 1, k), shift.reshape(bsz, 1, k), w)


def _layer_norm(z, g, b):
    mu = jnp.mean(z, axis=-1, keepdims=True)
    zc = z - mu
    var = jnp.mean(zc * zc, axis=-1, keepdims=True)
    return zc * lax.rsqrt(var + LN_EPS) * g + b


def _postnorm_kernel(x_ref, y_ref, gate_ref, g_ref, b_ref, sc_ref, sh_ref, xo_ref, ho_ref):
    xn = _layer_norm(ALPHA * x_ref[0] + gate_ref[0] * y_ref[0], g_ref[...], b_ref[...])
    xo_ref[0] = xn
    ho_ref[0] = (xn * (1.0 + sc_ref[0]) + sh_ref[0]).astype(ho_ref.dtype)


def post_norm_modulate(x, y, gate, g, b, scale, shift, tm=256):
    bsz, t, d = x.shape
    tm = min(tm, t)
    row = pl.BlockSpec((1, tm, d), lambda bi, i: (bi, i, 0))
    per_seq = pl.BlockSpec((1, 1, d), lambda bi, i: (bi, 0, 0))
    shared = pl.BlockSpec((1, d), lambda bi, i: (0, 0))
    return pl.pallas_call(
        _postnorm_kernel, grid=(bsz, t // tm),
        in_specs=[row, row, per_seq, shared, shared, per_seq, per_seq],
        out_specs=[row, row],
        out_shape=[jax.ShapeDtypeStruct(x.shape, jnp.float32), jax.ShapeDtypeStruct(x.shape, jnp.bfloat16)],
        compiler_params=_params("parallel", "parallel"))(
            x, y, gate.reshape(bsz, 1, d), g.reshape(1, d), b.reshape(1, d),
            scale.reshape(bsz, 1, d), shift.reshape(bsz, 1, d))


def _postnorm_moe_kernel(x_ref, y0_ref, y1_ref, y2_ref, y3_ref, w_ref, gate_ref, g_ref, b_ref, xo_ref):
    w = w_ref[...]
    y = sum(w[:, k:k + 1] * r[...].astype(jnp.float32) for k, r in enumerate((y0_ref, y1_ref, y2_ref, y3_ref)))
    xo_ref[0] = _layer_norm(ALPHA * x_ref[0] + gate_ref[0] * y, g_ref[...], b_ref[...])


def post_norm_moe(x, ys, w, row0, gate, g, b, tm=256):
    bsz, t, d = x.shape
    tm = min(tm, t)
    per_seq_blocks = t // tm
    assert row0 % tm == 0
    row = pl.BlockSpec((1, tm, d), lambda bi, i: (bi, i, 0))
    tok = lambda width: pl.BlockSpec((tm, width), lambda bi, i: (row0 // tm + bi * per_seq_blocks + i, 0))
    per_seq = pl.BlockSpec((1, 1, d), lambda bi, i: (bi, 0, 0))
    shared = pl.BlockSpec((1, d), lambda bi, i: (0, 0))
    return pl.pallas_call(
        _postnorm_moe_kernel, grid=(bsz, per_seq_blocks),
        in_specs=[row] + [tok(d)] * TOP_K + [tok(TOP_K), per_seq, shared, shared],
        out_specs=row,
        out_shape=jax.ShapeDtypeStruct(x.shape, jnp.float32),
        compiler_params=_params("parallel", "parallel"))(x, *ys, w, gate.reshape(bsz, 1, d), g.reshape(1, d), b.reshape(1, d))


def _moe_in_kernel(blk_e_ref, n_used_ref, x_ref, wg_ref, wl_ref, bg_ref, bl_ref, o_ref):
    used = pl.program_id(1) < n_used_ref[0]

    @pl.when(used)
    def _():
        x = x_ref[...]
        glu = jnp.dot(x, _bf16(wg_ref[...]), preferred_element_type=jnp.float32) + bg_ref[...]
        lin = jnp.dot(x, _bf16(wl_ref[...]), preferred_element_type=jnp.float32) + bl_ref[...]
        glu = jnp.minimum(glu, SWIGLU_LIMIT)
        lin = jnp.clip(lin, -SWIGLU_LIMIT, SWIGLU_LIMIT)
        o_ref[...] = (glu * jax.nn.sigmoid(SWIGLU_ALPHA * glu) * (lin + 1.0)).astype(o_ref.dtype)

    @pl.when(jnp.logical_not(used))
    def _():
        o_ref[...] = jnp.zeros(o_ref.shape, o_ref.dtype)


def _moe_out_kernel(blk_e_ref, n_used_ref, h_ref, w_ref, b_ref, o_ref):
    used = pl.program_id(1) < n_used_ref[0]

    @pl.when(used)
    def _():
        y = jnp.dot(h_ref[...], _bf16(w_ref[...]), preferred_element_type=jnp.float32) + b_ref[...]
        o_ref[...] = y.astype(o_ref.dtype)

    @pl.when(jnp.logical_not(used))
    def _():
        o_ref[...] = jnp.zeros(o_ref.shape, o_ref.dtype)


def moe_ffn(xs, blk_expert, n_used, layer, w_in, b_in, w_out, b_out):
    r = xs.shape[0]
    n_blk = r // MOE_ROW_BLK
    n_ff = D_FF // MOE_FF_TILE
    tm, tf, tn = MOE_ROW_BLK, MOE_FF_TILE, MOE_OUT_TILE
    b_in4 = b_in.reshape(DEPTH, N_EXPERTS, 1, 2 * D_FF)
    b_out4 = b_out.reshape(DEPTH, N_EXPERTS, 1, D_MODEL)
    row_blk = lambda j, m, be, nu: (jnp.minimum(m, nu[0] - 1), 0)
    act = pl.pallas_call(
        _moe_in_kernel,
        grid_spec=pltpu.PrefetchScalarGridSpec(
            num_scalar_prefetch=2, grid=(n_ff, n_blk),
            in_specs=[
                pl.BlockSpec((tm, D_MODEL), row_blk),
                pl.BlockSpec((None, None, D_MODEL, tf), lambda j, m, be, nu: (layer, be[m], 0, j)),
                pl.BlockSpec((None, None, D_MODEL, tf), lambda j, m, be, nu: (layer, be[m], 0, n_ff + j)),
                pl.BlockSpec((None, None, 1, tf), lambda j, m, be, nu: (layer, be[m], 0, j)),
                pl.BlockSpec((None, None, 1, tf), lambda j, m, be, nu: (layer, be[m], 0, n_ff + j)),
            ],
            out_specs=pl.BlockSpec((tm, tf), lambda j, m, be, nu: (m, j))),
        out_shape=jax.ShapeDtypeStruct((r, D_FF), jnp.bfloat16),
        compiler_params=_params("arbitrary", "arbitrary"))(blk_expert, n_used, xs, w_in, w_in, b_in4, b_in4)
    return pl.pallas_call(
        _moe_out_kernel,
        grid_spec=pltpu.PrefetchScalarGridSpec(
            num_scalar_prefetch=2, grid=(D_MODEL // tn, n_blk),
            in_specs=[
                pl.BlockSpec((tm, D_FF), row_blk),
                pl.BlockSpec((None, None, D_FF, tn), lambda j, m, be, nu: (layer, be[m], 0, j)),
                pl.BlockSpec((None, None, 1, tn), lambda j, m, be, nu: (layer, be[m], 0, j)),
            ],
            out_specs=pl.BlockSpec((tm, tn), lambda j, m, be, nu: (m, j))),
        out_shape=jax.ShapeDtypeStruct((r, D_MODEL), jnp.bfloat16),
        compiler_params=_params("arbitrary", "arbitrary"))(blk_expert, n_used, act, w_out, b_out4)


def moe(x, layer, w_router, b_router, w_in, b_in, w_out, b_out):
    n_tok = x.shape[0]
    n_asg = n_tok * TOP_K
    logits = jnp.dot(x.astype(jnp.float32), w_router[layer], precision=lax.Precision.HIGHEST) + b_router[layer]
    top_val, top_idx = lax.top_k(logits, TOP_K)
    gate = jax.nn.softmax(top_val, axis=-1)
    assert n_asg % MOE_ROW_BLK == 0
    onehot = (top_idx.reshape(-1, MOE_ROW_BLK, 1) == jnp.arange(N_EXPERTS)).astype(jnp.float32)
    tri = jnp.tril(jnp.ones((MOE_ROW_BLK, MOE_ROW_BLK), jnp.float32), -1)
    in_blk = jnp.einsum("ij,bje->bie", tri, onehot)
    blk_tot = jnp.sum(onehot, axis=1)
    before = jnp.cumsum(blk_tot, axis=0) - blk_tot
    rank = jnp.sum(onehot * (in_blk + before[:, None, :]), axis=-1).reshape(-1).astype(jnp.int32)
    counts = jnp.sum(blk_tot, axis=0).astype(jnp.int32)
    blocks = (counts + MOE_ROW_BLK - 1) // MOE_ROW_BLK
    blk_end = jnp.cumsum(blocks)
    dest = (blk_end - blocks)[top_idx.reshape(-1)] * MOE_ROW_BLK + rank
    n_blk = n_asg // MOE_ROW_BLK + N_EXPERTS
    row_tok = jnp.zeros((n_blk * MOE_ROW_BLK,), jnp.int32).at[dest].set(jnp.arange(n_asg, dtype=jnp.int32) // TOP_K)
    n_used = blk_end[-1:].astype(jnp.int32)
    blk_of = jnp.minimum(jnp.arange(n_blk, dtype=jnp.int32), n_used - 1)
    blk_expert = jnp.sum((blk_end[None, :] <= blk_of[:, None]).astype(jnp.int32), axis=1)
    xs = x[row_tok]
    ys = moe_ffn(xs, blk_expert, n_used, layer, w_in, b_in, w_out, b_out)
    dest = dest.reshape(n_tok, TOP_K)
    return [ys[dest[:, k]] for k in range(TOP_K)], gate


def split_cols(z, sizes):
    return jnp.split(z, np.cumsum(sizes)[:-1].tolist(), axis=-1)


def rms_norm(x, g):
    return x * lax.rsqrt(jnp.mean(jnp.square(x), axis=-1, keepdims=True) + NORM_EPS) * g


def l2_normalize(x):
    return x * lax.rsqrt(jnp.sum(jnp.square(x), axis=-1, keepdims=True) + NORM_EPS)


def masked_softmax(logits, mask):
    return jax.nn.softmax(jnp.where(mask, logits, NEG), axis=-1)


def gather_pages(pool, page_table, layer):
    rows = pool[page_table, :, layer]
    return rows.reshape(rows.shape[0], -1, *rows.shape[3:])


def gather_rows(pool, page_table, layer, pos):
    b = jnp.arange(pos.shape[0]).reshape((-1,) + (1,) * (pos.ndim - 1))
    phys = page_table[b, pos // PAGE_SIZE]
    return pool[phys, pos % PAGE_SIZE, layer]


def gather_group_rows(pool, page_table, layer, pos, kv_j):
    b = jnp.arange(pos.shape[0])[:, None, None, None, None]
    g = jnp.arange(C_KV)[None, None, :, None, None]
    phys = page_table[b, pos // PAGE_SIZE]
    return pool[phys, pos % PAGE_SIZE, layer, kv_j, g]


def dsa_split(z):
    B, T, _ = z.shape
    q, k, v, qi, ki, wi = split_cols(z, [A_HEADS * HEAD_DIM, A_KV * HEAD_DIM, A_KV * HEAD_DIM,
                                         A_IDX_HEADS * A_IDX_DIM, A_IDX_DIM, A_IDX_HEADS])
    q = q.reshape(B, T, A_KV, A_REP, HEAD_DIM)
    kv = jnp.stack([k.reshape(B, T, A_KV, HEAD_DIM), v.reshape(B, T, A_KV, HEAD_DIM)], axis=2)
    qi = qi.reshape(B, T, A_IDX_HEADS, A_IDX_DIM)
    return q, kv, qi, ki, wi * A_IDX_HEADS ** -0.5


def indexer_scores(qi, wi, ki, q_pos, k_pos):
    s = jax.nn.relu(jnp.einsum("bthd,bsd->bths", qi, ki).astype(jnp.float32) * A_IDX_DIM ** -0.5)
    score = jnp.einsum("bths,bth->bts", s, wi.astype(jnp.float32))
    return jnp.where(k_pos[None, None, :] <= q_pos[None, :, None], score, NEG)


def gathered_attend(q, kv_sel, ok):
    logits = jnp.einsum("btgrd,btkgd->btgrk", q, kv_sel[:, :, :, 0]).astype(jnp.float32) * ATTN_SCALE
    p = masked_softmax(logits, ok[:, :, None, None, :])
    return jnp.einsum("btgrk,btkgd->btgrd", p.astype(q.dtype), kv_sel[:, :, :, 1])


INT32_MIN = -2 ** 31
LANES = 128
PAGES_PER_STEP = 4
STEP_KEYS = PAGES_PER_STEP * PAGE_SIZE


def _sortable_key(x):
    bits = pltpu.bitcast(x, jnp.int32)
    return jnp.where(bits < 0, bits ^ 0x7FFFFFFF, bits)


def _kth_largest_key(key_ref, n_chunks, chunk, k, rows):
    def count_ge(cand):
        def body(c, acc):
            off = pl.multiple_of(c * chunk, chunk)
            for j in range(chunk // LANES):
                keys = key_ref[:, pl.ds(off + j * LANES, LANES)]
                acc = acc + jnp.where(keys >= cand, 1.0, 0.0)
            return acc

        acc = lax.fori_loop(0, n_chunks, body, jnp.zeros((rows, LANES), jnp.float32))
        return jnp.sum(acc, axis=-1, keepdims=True)

    def bit_step(b, ans):
        cand = ans + lax.shift_left(jnp.int32(1), 31 - b)
        return jnp.where(count_ge(cand) >= k, cand, ans)

    return lax.fori_loop(0, 32, bit_step, jnp.full((rows, 1), INT32_MIN, jnp.int32))


def _topk_bias(key_ref, bias_ref, thr, n_chunks, chunk, k, rows, row_pos):
    tri = (lax.broadcasted_iota(jnp.int32, (LANES, LANES), 0)
           <= lax.broadcasted_iota(jnp.int32, (LANES, LANES), 1)).astype(jnp.bfloat16)
    col = lax.broadcasted_iota(jnp.int32, (rows, LANES), 1)

    def count_gt(c, acc):
        off = pl.multiple_of(c * chunk, chunk)
        for j in range(chunk // LANES):
            acc = acc + jnp.where(key_ref[:, pl.ds(off + j * LANES, LANES)] > thr, 1.0, 0.0)
        return acc

    n_gt = jnp.sum(lax.fori_loop(0, n_chunks, count_gt, jnp.zeros((rows, LANES), jnp.float32)), axis=-1, keepdims=True)
    need = k - n_gt

    def body(c, run):
        off = pl.multiple_of(c * chunk, chunk)
        for j in range(chunk // LANES):
            keys = key_ref[:, pl.ds(off + j * LANES, LANES)]
            eq = keys == thr
            eq_f = jnp.where(eq, 1.0, 0.0)
            incl = jnp.dot(eq_f.astype(jnp.bfloat16), tri, preferred_element_type=jnp.float32)
            sel = (keys > thr) | (eq & (run + incl - eq_f < need))
            ok = sel & (off + j * LANES + col <= row_pos)
            bias_ref[:, pl.ds(off + j * LANES, LANES)] = jnp.where(ok, 0.0, NEG)
            run = run + incl[:, LANES - 1:LANES]
        return run

    lax.fori_loop(0, n_chunks, body, jnp.zeros((rows, 1), jnp.float32))


def _stack_heads(q_ref, g, rep):
    return jnp.concatenate([q_ref[0, :, (g * rep + r) * HEAD_DIM:(g * rep + r + 1) * HEAD_DIM] for r in range(rep)], axis=0)


def _flash_group(qg, kt_ref, v_ref, g, lo, hi, chunk, bias_fn):
    rows = qg.shape[0]

    def body(c, carry):
        m, l, acc = carry
        off = pl.multiple_of(c * chunk, chunk)
        kt = kt_ref[0, g * HEAD_DIM:(g + 1) * HEAD_DIM, pl.ds(off, chunk)]
        s = jnp.dot(qg, kt, preferred_element_type=jnp.float32) * ATTN_SCALE + bias_fn(off)
        m_new = jnp.maximum(m, jnp.max(s, axis=-1, keepdims=True))
        p = jnp.exp(s - m_new)
        alpha = jnp.exp(m - m_new)
        l = alpha * l + jnp.sum(p, axis=-1, keepdims=True)
        v = v_ref[0, pl.ds(off, chunk), g * HEAD_DIM:(g + 1) * HEAD_DIM]
        acc = alpha * acc + jnp.dot(_bf16(p), v, preferred_element_type=jnp.float32)
        return m_new, l, acc

    init = (jnp.full((rows, 1), NEG, jnp.float32), jnp.zeros((rows, 1), jnp.float32),
            jnp.zeros((rows, HEAD_DIM), jnp.float32))
    _, l, acc = lax.fori_loop(lo, hi, body, init)
    return acc / l


DSA_TQ = 128
DSA_TK = 512


def _dsa_prompt_kernel(q_ref, qi_ref, wi_ref, kit_ref, kt_ref, v_ref, o_ref, key_ref, bias_ref, *, topk):
    tq, tk = DSA_TQ, DSA_TK
    q_start = pl.program_id(1) * tq
    n_chunks = (q_start + tq + tk - 1) // tk
    row_pos = q_start + lax.broadcasted_iota(jnp.int32, (tq, tk), 0)
    col = lax.broadcasted_iota(jnp.int32, (tq, tk), 1)
    wi = wi_ref[0]

    def score_chunk(c, carry):
        off = pl.multiple_of(c * tk, tk)
        kit = kit_ref[0, :, pl.ds(off, tk)]
        acc = jnp.zeros((tq, tk), jnp.float32)
        for h in range(A_IDX_HEADS):
            s = jnp.dot(qi_ref[0, h], kit, preferred_element_type=jnp.float32)
            acc = acc + jnp.maximum(s, 0.0) * wi[:, h:h + 1]
        key_ref[:, pl.ds(off, tk)] = _sortable_key(jnp.where(off + col <= row_pos, acc, NEG))
        return carry

    lax.fori_loop(0, n_chunks, score_chunk, 0)
    thr = _kth_largest_key(key_ref, n_chunks, tk, topk, tq)
    _topk_bias(key_ref, bias_ref, thr, n_chunks, tk, topk, tq, row_pos[:, :LANES])
    bias_fn = lambda off: jnp.concatenate([bias_ref[:, pl.ds(off, tk)]] * A_REP, axis=0)
    for g in range(A_KV):
        out = _flash_group(_stack_heads(q_ref, g, A_REP), kt_ref, v_ref, g, 0, n_chunks, tk, bias_fn)
        for r in range(A_REP):
            h = g * A_REP + r
            o_ref[0, :, h * HEAD_DIM:(h + 1) * HEAD_DIM] = out[r * tq:(r + 1) * tq].astype(o_ref.dtype)


def dsa_prompt(z):
    B, T, _ = z.shape
    topk = min(A_TOPK, T // 4)
    nq, nk = A_HEADS * HEAD_DIM, A_KV * HEAD_DIM
    q = _bf16(z[..., :nq])
    k, v = z[..., nq:nq + nk], z[..., nq + nk:nq + 2 * nk]
    off = nq + 2 * nk
    qi = _bf16(z[..., off:off + A_IDX_HEADS * A_IDX_DIM]).reshape(B, T, A_IDX_HEADS, A_IDX_DIM).transpose(0, 2, 1, 3)
    off += A_IDX_HEADS * A_IDX_DIM
    ki = z[..., off:off + A_IDX_DIM]
    wi = z[..., off + A_IDX_DIM:] * (A_IDX_HEADS ** -0.5 * A_IDX_DIM ** -0.5)
    kit = _bf16(ki).transpose(0, 2, 1)
    kt = _bf16(k).transpose(0, 2, 1)
    tq = DSA_TQ
    o = pl.pallas_call(
        functools.partial(_dsa_prompt_kernel, topk=topk),
        grid=(B, T // tq),
        in_specs=[
            pl.BlockSpec((1, tq, nq), lambda b, i: (b, i, 0)),
            pl.BlockSpec((1, A_IDX_HEADS, tq, A_IDX_DIM), lambda b, i: (b, 0, i, 0)),
            pl.BlockSpec((1, tq, A_IDX_HEADS), lambda b, i: (b, i, 0)),
            pl.BlockSpec((1, A_IDX_DIM, T), lambda b, i: (b, 0, 0)),
            pl.BlockSpec((1, nk, T), lambda b, i: (b, 0, 0)),
            pl.BlockSpec((1, T, nk), lambda b, i: (b, 0, 0)),
        ],
        out_specs=pl.BlockSpec((1, tq, nq), lambda b, i: (b, i, 0)),
        out_shape=jax.ShapeDtypeStruct((B, T, nq), jnp.bfloat16),
        scratch_shapes=[pltpu.VMEM((tq, T), jnp.int32), pltpu.VMEM((tq, T), jnp.float32)],
        compiler_params=_params("parallel", "arbitrary"))(q, qi, wi, kit, kt, _bf16(v))
    kv = jnp.stack([k.reshape(B, T, A_KV, HEAD_DIM), v.reshape(B, T, A_KV, HEAD_DIM)], axis=2)
    return o, kv, ki


def _nt_dot(a, b):
    return lax.dot_general(a, b, (((1,), (1,)), ((), ())), preferred_element_type=jnp.float32)


def _dsa_sample_score_kernel(pt_ref, qi_ref, wi_ref, *rest, layer, n_pages, t_new):
    pool_refs, (new_ref, o_ref) = rest[:PAGES_PER_STEP], rest[PAGES_PER_STEP:]
    p = pl.program_id(1)

    def scores(ki):
        s = jnp.maximum(_nt_dot(qi_ref[0], _bf16(ki)), 0.0) * wi_ref[0]
        acc = s[0:t_new]
        for h in range(1, A_IDX_HEADS):
            acc = acc + s[h * t_new:(h + 1) * t_new]
        k_pos = p * STEP_KEYS + lax.broadcasted_iota(jnp.int32, (t_new, STEP_KEYS), 1)
        q_pos = n_pages * PAGE_SIZE + lax.broadcasted_iota(jnp.int32, (t_new, STEP_KEYS), 0)
        o_ref[0] = jnp.where(k_pos <= q_pos, acc, NEG)

    @pl.when(p < n_pages // PAGES_PER_STEP)
    def _():
        scores(jnp.concatenate([r[:, layer, :] for r in pool_refs], axis=0))

    @pl.when(p == n_pages // PAGES_PER_STEP)
    def _():
        scores(new_ref[0])


def _dsa_sample_select_kernel(s_ref, bias_ref, key_ref, *, topk, chunk, q_pos0):
    rows, n_keys = key_ref.shape
    n_chunks = n_keys // chunk
    for c in range(n_chunks):
        key_ref[:, c * chunk:(c + 1) * chunk] = _sortable_key(s_ref[0, :, c * chunk:(c + 1) * chunk])
    thr = _kth_largest_key(key_ref, n_chunks, chunk, topk, rows)
    q_pos = q_pos0 + lax.broadcasted_iota(jnp.int32, (rows, LANES), 0)
    _topk_bias(key_ref, bias_ref.at[0], thr, n_chunks, chunk, topk, rows, q_pos)


def _dsa_sample_attend_kernel(pt_ref, q_ref, bias_ref, *rest, n_pages, t_new):
    pool_refs, (new_ref, o_ref, m_ref, l_ref, acc_ref) = rest[:PAGES_PER_STEP], rest[PAGES_PER_STEP:]
    p = pl.program_id(1)

    @pl.when(p == 0)
    def _():
        _init_online_softmax(m_ref, l_ref, acc_ref)

    def attend(kv_of):
        bias = jnp.concatenate([bias_ref[0]] * A_REP, axis=0)
        _paged_attend_step(q_ref, kv_of, lambda g: bias, m_ref, l_ref, acc_ref, A_KV, A_REP * t_new)

    @pl.when(p < n_pages // PAGES_PER_STEP)
    def _():
        attend(lambda e, g: jnp.concatenate([r[:, e, g, :] for r in pool_refs], axis=0))

    @pl.when(p == n_pages // PAGES_PER_STEP)
    def _():
        attend(lambda e, g: new_ref[0, :, e, g, :])
        o_ref[0] = (acc_ref[...] / l_ref[...]).astype(o_ref.dtype)


def dsa_sample(z, pool_kv, pool_idx, page_table, layer):
    B, T, _ = z.shape
    n_pages = page_table.shape[1]
    past = n_pages * PAGE_SIZE
    topk = min(A_TOPK, (past + T) // 4)
    nq, nk = A_HEADS * HEAD_DIM, A_KV * HEAD_DIM
    n_layers = pool_kv.shape[2]
    k, v = z[..., nq:nq + nk], z[..., nq + nk:nq + 2 * nk]
    off = nq + 2 * nk
    head_major = lambda a, d: a.reshape(B, T, -1, d).transpose(0, 2, 1, 3).reshape(B, -1, d)
    q = head_major(_bf16(z[..., :nq]), HEAD_DIM)
    qi = head_major(_bf16(z[..., off:off + A_IDX_HEADS * A_IDX_DIM]), A_IDX_DIM)
    off += A_IDX_HEADS * A_IDX_DIM
    ki = z[..., off:off + A_IDX_DIM]
    wi = head_major(z[..., off + A_IDX_DIM:] * (A_IDX_HEADS ** -0.5 * A_IDX_DIM ** -0.5), 1)
    assert n_pages % PAGES_PER_STEP == 0 and T <= STEP_KEYS
    n_steps = n_pages // PAGES_PER_STEP + 1
    n_keys = n_steps * STEP_KEYS
    pad_rows = lambda a: jnp.pad(a, ((0, 0), (0, STEP_KEYS - T)) + ((0, 0),) * (a.ndim - 2))
    page_of = lambda b, p, pt, i: pt[b, jnp.minimum(p * PAGES_PER_STEP + i, n_pages - 1)]
    per_b = lambda b, p, pt: (b, 0, 0)
    scores = pl.pallas_call(
        functools.partial(_dsa_sample_score_kernel, layer=layer, n_pages=n_pages, t_new=T),
        grid_spec=pltpu.PrefetchScalarGridSpec(
            num_scalar_prefetch=1, grid=(B, n_steps),
            in_specs=[pl.BlockSpec((1, A_IDX_HEADS * T, A_IDX_DIM), per_b),
                      pl.BlockSpec((1, A_IDX_HEADS * T, 1), per_b)]
            + [pl.BlockSpec((None, PAGE_SIZE, n_layers, A_IDX_DIM), lambda b, p, pt, i=i: (page_of(b, p, pt, i), 0, 0, 0))
               for i in range(PAGES_PER_STEP)]
            + [pl.BlockSpec((1, STEP_KEYS, A_IDX_DIM), per_b)],
            out_specs=pl.BlockSpec((1, T, STEP_KEYS), lambda b, p, pt: (b, 0, p))),
        out_shape=jax.ShapeDtypeStruct((B, T, n_keys), jnp.float32),
        compiler_params=_params("parallel", "arbitrary"))(page_table, qi, wi, *[pool_idx] * PAGES_PER_STEP, pad_rows(ki))
    chunk = STEP_KEYS
    bias = pl.pallas_call(
        functools.partial(_dsa_sample_select_kernel, topk=topk, chunk=chunk, q_pos0=past),
        grid=(B,),
        in_specs=[pl.BlockSpec((1, T, n_keys), lambda b: (b, 0, 0))],
        out_specs=pl.BlockSpec((1, T, n_keys), lambda b: (b, 0, 0)),
        out_shape=jax.ShapeDtypeStruct((B, T, n_keys), jnp.float32),
        scratch_shapes=[pltpu.VMEM((T, n_keys), jnp.int32)],
        compiler_params=_params("parallel"))(scores)
    kv_new = jnp.stack([k.reshape(B, T, A_KV, HEAD_DIM), v.reshape(B, T, A_KV, HEAD_DIM)], axis=2)
    o = pl.pallas_call(
        functools.partial(_dsa_sample_attend_kernel, n_pages=n_pages, t_new=T),
        grid_spec=pltpu.PrefetchScalarGridSpec(
            num_scalar_prefetch=1, grid=(B, n_steps),
            in_specs=[pl.BlockSpec((1, A_HEADS * T, HEAD_DIM), per_b),
                      pl.BlockSpec((1, T, STEP_KEYS), lambda b, p, pt: (b, 0, p))]
            + [pl.BlockSpec((None, PAGE_SIZE, None, 2, A_KV, HEAD_DIM),
                            lambda b, p, pt, i=i: (page_of(b, p, pt, i), 0, layer, 0, 0, 0)) for i in range(PAGES_PER_STEP)]
            + [pl.BlockSpec((1, STEP_KEYS, 2, A_KV, HEAD_DIM), lambda b, p, pt: (b, 0, 0, 0, 0))],
            out_specs=pl.BlockSpec((1, A_HEADS * T, HEAD_DIM), per_b),
            scratch_shapes=[pltpu.VMEM((A_HEADS * T, 1), jnp.float32), pltpu.VMEM((A_HEADS * T, 1), jnp.float32),
                            pltpu.VMEM((A_HEADS * T, HEAD_DIM), jnp.float32)]),
        out_shape=jax.ShapeDtypeStruct((B, A_HEADS * T, HEAD_DIM), jnp.bfloat16),
        compiler_params=_params("parallel", "arbitrary"))(
            page_table, q, bias, *[pool_kv] * PAGES_PER_STEP, pad_rows(kv_new))
    o = o.reshape(B, A_HEADS, T, HEAD_DIM).transpose(0, 2, 1, 3).reshape(B, T, nq)
    return o, kv_new, ki


TRI_PACK = 4
TRI_GROUPS = 2


def _dot_bf16x3(a, b):
    a_hi, b_hi = _bf16(a), _bf16(b)
    a_lo, b_lo = _bf16(a - a_hi.astype(jnp.float32)), _bf16(b - b_hi.astype(jnp.float32))
    dot = functools.partial(jnp.dot, preferred_element_type=jnp.float32)
    return dot(a_hi, b_hi) + (dot(a_hi, b_lo) + dot(a_lo, b_hi))


def _unit_lower_inverse_kernel(a_ref, o_ref):
    c = a_ref.shape[-1]
    size = TRI_PACK * c
    eye = jnp.where(lax.broadcasted_iota(jnp.int32, (size, size), 0) == lax.broadcasted_iota(jnp.int32, (size, size), 1), 1.0, 0.0)
    zero = jnp.zeros((c, c), jnp.float32)
    for i in range(TRI_GROUPS):
        n = jnp.concatenate([jnp.concatenate([-a_ref[i, j] if jj == j else zero for jj in range(TRI_PACK)], axis=1)
                             for j in range(TRI_PACK)], axis=0)
        t, p = eye + n, n
        for _ in range((c - 1).bit_length() - 1):
            p = _dot_bf16x3(p, p)
            t = t + _dot_bf16x3(t, p)
        for j in range(TRI_PACK):
            o_ref[i, j] = t[j * c:(j + 1) * c, j * c:(j + 1) * c]


def unit_lower_inverse(a):
    c = a.shape[-1]
    n = math.prod(a.shape[:-2])
    per_step = TRI_GROUPS * TRI_PACK
    assert n % per_step == 0
    blk = (TRI_GROUPS, TRI_PACK, c, c)
    out = pl.pallas_call(
        _unit_lower_inverse_kernel, grid=(n // per_step,),
        in_specs=[pl.BlockSpec(blk, lambda i: (i, 0, 0, 0))],
        out_specs=pl.BlockSpec(blk, lambda i: (i, 0, 0, 0)),
        out_shape=jax.ShapeDtypeStruct((n // TRI_PACK, TRI_PACK, c, c), jnp.float32),
        compiler_params=_params("parallel"))(a.reshape(n // TRI_PACK, TRI_PACK, c, c))
    return out.reshape(a.shape)
def chunk_gated_delta(q, k, v, g, beta, s0):
    B, T, H, DK = q.shape
    C = B_CHUNK
    n = -(-T // C)
    pad = n * C - T

    def chunks(a):
        a = jnp.pad(a, [(0, 0), (0, pad)] + [(0, 0)] * (a.ndim - 2))
        a = a.reshape(B, n, C, *a.shape[2:])
        return jnp.moveaxis(a, (1, 3), (0, 2))

    qc, kc, vc, bc = chunks(q) * DK ** -0.5, chunks(k), chunks(v), chunks(beta)
    gc = jnp.cumsum(chunks(g), axis=-1)
    incl = jnp.tril(jnp.ones((C, C), bool))
    strict = jnp.tril(jnp.ones((C, C), bool), -1)
    decay = jnp.exp(jnp.where(incl, gc[..., :, None] - gc[..., None, :], NEG))
    kb = kc * bc[..., None]
    a_mat = jnp.where(strict, jnp.einsum("...id,...jd->...ij", kb, kc) * decay, 0.0)
    t_mat = unit_lower_inverse(a_mat)
    u = t_mat @ (vc * bc[..., None])
    w = t_mat @ (kb * jnp.exp(gc)[..., None])
    attn = jnp.einsum("...id,...jd->...ij", qc, kc) * decay

    def step(s, inp):
        q_i, k_i, u_i, w_i, g_i, attn_i = inp
        v_new = u_i - w_i @ s
        o = (q_i * jnp.exp(g_i)[..., None]) @ s + attn_i @ v_new
        g_last = g_i[..., -1]
        s = s * jnp.exp(g_last)[..., None, None] + jnp.einsum(
            "bhcd,bhce->bhde", k_i * jnp.exp(g_last[..., None] - g_i)[..., None], v_new)
        return s, o

    s_final, o = lax.scan(step, s0, (qc, kc, u, w, gc, attn))
    o = jnp.moveaxis(o, (0, 2), (1, 3)).reshape(B, n * C, H, -1)[:, :T]
    return o, s_final


def gdn_mixer(zin, conv_w, a_log, dt_bias, norm_g, conv_state, rec_state):
    B, T, _ = zin.shape
    qkv, z, b_raw, a_raw = split_cols(zin, [B_CONV_CH, B_V_HEADS * B_DV, B_V_HEADS, B_V_HEADS])
    x_ext = jnp.concatenate([conv_state, qkv], axis=1)
    conv = sum(x_ext[:, i:i + T] * conv_w[i][None, None, :] for i in range(B_CONV))
    q, k, v = split_cols(jax.nn.silu(conv), [B_QK_HEADS * B_DK, B_QK_HEADS * B_DK, B_V_HEADS * B_DV])
    rep = B_V_HEADS // B_QK_HEADS
    q = jnp.repeat(l2_normalize(q.reshape(B, T, B_QK_HEADS, B_DK)), rep, axis=2)
    k = jnp.repeat(l2_normalize(k.reshape(B, T, B_QK_HEADS, B_DK)), rep, axis=2)
    v = v.reshape(B, T, B_V_HEADS, B_DV)
    beta = jax.nn.sigmoid(b_raw)
    g = -jnp.exp(a_log) * jax.nn.softplus(a_raw + dt_bias)
    o, s_new = chunk_gated_delta(q, k, v, g, beta, rec_state)
    o = rms_norm(o, norm_g) * jax.nn.silu(z.reshape(B, T, B_V_HEADS, B_DV))
    return o.reshape(B, T, B_V_HEADS * B_DV), x_ext[:, -(B_CONV - 1):], s_new


def nsa_split(z):
    B, T, _ = z.shape
    parts = split_cols(z, [C_HEADS * HEAD_DIM] + [C_KV * HEAD_DIM] * 6 + [3 * C_HEADS])
    q = parts[0].reshape(B, T, C_KV, C_REP, HEAD_DIM)
    kvs = [p.reshape(B, T, C_KV, HEAD_DIM) for p in parts[1:7]]
    cmp_kv = jnp.stack(kvs[0:2], axis=2)
    slc_kv = jnp.stack(kvs[2:4], axis=2)
    win_kv = jnp.stack(kvs[4:6], axis=2)
    gate = jax.nn.sigmoid(parts[7]).reshape(B, T, C_KV, C_REP, 3)
    return q, cmp_kv, slc_kv, win_kv, gate


def compress(kv, pe, w1, w2):
    B, L = kv.shape[:2]
    n_cmp = (L - CMP_BLK) // CMP_STRIDE + 1
    idx = jnp.arange(n_cmp)[:, None] * CMP_STRIDE + jnp.arange(CMP_BLK)[None, :]
    blk = kv[:, idx] + pe[None, None, :, :, None, :]
    blk = jnp.moveaxis(blk, 2, 4).reshape(B, n_cmp, 2, C_KV, CMP_BLK * HEAD_DIM)
    hid = jax.nn.gelu(jnp.einsum("bnegi,eio->bnego", blk, w1))
    return jnp.einsum("bnegi,eio->bnego", hid, w2)


def cmp_attend(q, kv_cmp, q_pos):
    n_cmp = kv_cmp.shape[1]
    logits = jnp.einsum("btgrd,bngd->btgrn", q, kv_cmp[:, :, 0]).astype(jnp.float32) * ATTN_SCALE
    vis = ((jnp.arange(n_cmp) * CMP_STRIDE + CMP_BLK - 1)[None, :] <= q_pos[:, None])[None, :, None, None, :]
    p = jnp.where(vis, masked_softmax(logits, vis), 0.0)
    return jnp.einsum("btgrn,bngd->btgrd", p.astype(q.dtype), kv_cmp[:, :, 1]), p


def select_blocks(p_cmp, q_pos, L):
    n_cmp = p_cmp.shape[-1]
    n_slc = -(-L // SLC_BLK)
    cmp_start = jnp.arange(n_cmp)[:, None] * CMP_STRIDE
    slc_start = jnp.arange(n_slc)[None, :] * SLC_BLK
    overlap = ((cmp_start < slc_start + SLC_BLK) & (cmp_start + CMP_BLK > slc_start)).astype(jnp.float32)
    imp = jnp.einsum("btgrn,nm->btgm", p_cmp, overlap, precision=lax.Precision.HIGHEST)
    cur = (q_pos // SLC_BLK)[None, :, None, None]
    blk = jnp.arange(n_slc)
    imp = jnp.where((blk == 0) | (blk == cur), FORCE, imp)
    imp = jnp.where(blk <= cur, imp, NEG)
    _, idx = lax.top_k(imp, min(N_SEL, n_slc))
    return idx, idx <= cur


def slc_attend(q, kb, vb, blk_idx, blk_ok, q_pos):
    B, T, G, NS, SB, dh = kb.shape
    key_pos = blk_idx[..., None] * SLC_BLK + jnp.arange(SLC_BLK)
    ok = blk_ok[..., None] & (key_pos <= q_pos[None, :, None, None, None])
    logits = jnp.einsum("btgrd,btgnsd->btgrns", q, kb).astype(jnp.float32) * ATTN_SCALE
    p = masked_softmax(logits.reshape(B, T, G, C_REP, NS * SB), ok.reshape(B, T, G, 1, NS * SB))
    return jnp.einsum("btgrk,btgkd->btgrd", p.astype(q.dtype), vb.reshape(B, T, G, NS * SB, dh))


def win_attend(q, kv, q_pos, k_pos):
    diff = q_pos[:, :, None] - k_pos[:, None, :]
    ok = (k_pos[:, None, :] >= 0) & (diff >= 0) & (diff < WINDOW)
    logits = jnp.einsum("bnqgrd,bnkgd->bnqgrk", q, kv[:, :, :, 0]).astype(jnp.float32) * ATTN_SCALE
    p = masked_softmax(logits, ok[None, :, :, None, None, :])
    return jnp.einsum("bnqgrk,bnkgd->bnqgrd", p.astype(q.dtype), kv[:, :, :, 1])


def to_blocks(k):
    B, L = k.shape[:2]
    n = -(-L // SLC_BLK)
    k = jnp.pad(k, ((0, 0), (0, n * SLC_BLK - L), (0, 0), (0, 0)))
    return k.reshape(B, n, SLC_BLK, C_KV, HEAD_DIM).transpose(0, 3, 1, 2, 4)


def nsa_combine(gate, o_cmp, o_slc, o_win):
    o = gate[..., 0:1] * o_cmp + gate[..., 1:2] * o_slc + gate[..., 2:3] * o_win
    B, T = o.shape[:2]
    return o.reshape(B, T, C_HEADS * HEAD_DIM)


def nsa_compress(kv, n_cmp, pe, w1, w2, layer):
    B = kv.shape[0]
    n_row = n_cmp + 1
    half = CMP_STRIDE * HEAD_DIM
    x = kv[:, :n_row * CMP_STRIDE].reshape(B, n_row, CMP_STRIDE, 2, C_KV, HEAD_DIM)
    x = x.transpose(3, 0, 4, 1, 2, 5).reshape(2, B * C_KV * n_row, half)
    w1l = w1[layer]
    w_halves = jnp.concatenate([w1l[:, :half], w1l[:, half:]], axis=-1)
    pe_rows = jnp.broadcast_to(pe[layer].transpose(1, 0, 2).reshape(2, 1, CMP_BLK * HEAD_DIM), (2, 8, CMP_BLK * HEAD_DIM))
    outs = []
    for e in range(2):
        ab = matmul(x[e], w_halves, e).reshape(B, C_KV, n_row, 2, HEAD_DIM)
        c = matmul(pe_rows[e], w1l, e)[0]
        hid = jax.nn.gelu(ab[:, :, :n_cmp, 0] + ab[:, :, 1:, 1] + c)
        outs.append(matmul(hid.reshape(B * C_KV * n_cmp, HEAD_DIM), w2[layer], e).reshape(B, C_KV, n_cmp, HEAD_DIM))
    return outs


NSA_TQ = 128
NSA_TK = 512
SLC_SHIFT = SLC_BLK.bit_length() - 1


def _nsa_prompt_kernel(q_ref, gl_ref, kct_ref, vc_ref, kst_ref, vs_ref, kwt_ref, vw_ref, o_ref, key_ref, bias_ref, *,
                       n_cmp, n_slc):
    tq, tk = NSA_TQ, NSA_TK
    rows = C_REP * tq
    q_start = pl.program_id(1) * tq
    n_chunks = (q_start + tq + tk - 1) // tk
    win_lo = jnp.maximum(q_start - (WINDOW - 1), 0) // tk
    n_pad = kct_ref.shape[2]
    iota = lambda shape, ax: lax.broadcasted_iota(jnp.int32, shape, ax)
    tile = lambda a: jnp.concatenate([a] * C_REP, axis=0)

    n_c = iota((tq, n_pad), 1)
    vis = tile((n_c * CMP_STRIDE + CMP_BLK - 1 <= q_start + iota((tq, n_pad), 0)) & (n_c < n_cmp))
    n_o, m_o = iota((n_pad, LANES), 0) * CMP_STRIDE, iota((n_pad, LANES), 1) * SLC_BLK
    overlap = jnp.where((n_o < m_o + SLC_BLK) & (n_o + CMP_BLK > m_o) & (iota((n_pad, LANES), 1) < n_slc), 1.0, 0.0)
    cur = jnp.right_shift(q_start + iota((tq, LANES), 0), SLC_SHIFT)
    blk = iota((tq, LANES), 1)
    o_cmp = []
    for g in range(C_KV):
        qg = _stack_heads(q_ref, g, C_REP)
        s = jnp.dot(qg, kct_ref[0, g * HEAD_DIM:(g + 1) * HEAD_DIM, :], preferred_element_type=jnp.float32) * ATTN_SCALE
        s = jnp.where(vis, s, NEG)
        e = jnp.exp(s - jnp.max(s, axis=-1, keepdims=True))
        p = jnp.where(vis, e / jnp.sum(e, axis=-1, keepdims=True), 0.0)
        o_cmp.append(jnp.dot(_bf16(p), vc_ref[0, :, g * HEAD_DIM:(g + 1) * HEAD_DIM], preferred_element_type=jnp.float32))
        p_grp = sum(p[r * tq:(r + 1) * tq] for r in range(C_REP))
        imp = jnp.dot(p_grp, overlap, preferred_element_type=jnp.float32, precision=lax.Precision.HIGHEST)
        imp = jnp.where((blk == 0) | (blk == cur), FORCE, imp)
        key_ref[g * tq:(g + 1) * tq, :] = _sortable_key(jnp.where(blk <= cur, imp, NEG))
    k_sel = min(N_SEL, n_slc)
    thr = _kth_largest_key(key_ref, 1, LANES, k_sel, rows)
    _topk_bias(key_ref, bias_ref, thr, 1, LANES, k_sel, rows, tile(cur))

    t_k = q_start + iota((tq, tk), 0)
    s_k = iota((tq, tk), 1)
    blk_e, s_e = iota((LANES, tk), 0), iota((LANES, tk), 1)
    gate = jax.nn.sigmoid(gl_ref[0])

    def win_bias(off):
        d = t_k - (off + s_k)
        return tile(jnp.where((d >= 0) & (d < WINDOW), 0.0, NEG))

    for g in range(C_KV):
        qg = _stack_heads(q_ref, g, C_REP)
        sel = _bf16(jnp.where(bias_ref[g * tq:(g + 1) * tq, :] == 0.0, 1.0, 0.0))

        def slc_bias(off, sel=sel):
            expand = _bf16(jnp.where(blk_e == jnp.right_shift(off + s_e, SLC_SHIFT), 1.0, 0.0))
            hit = jnp.dot(sel, expand, preferred_element_type=jnp.float32)
            return tile(jnp.where((hit > 0.5) & (off + s_k <= t_k), 0.0, NEG))

        o_slc = _flash_group(qg, kst_ref, vs_ref, g, 0, n_chunks, tk, slc_bias)
        o_win = _flash_group(qg, kwt_ref, vw_ref, g, win_lo, n_chunks, tk, win_bias)
        for r in range(C_REP):
            h = g * C_REP + r
            rs = slice(r * tq, (r + 1) * tq)
            o = (gate[:, 3 * h:3 * h + 1] * o_cmp[g][rs] + gate[:, 3 * h + 1:3 * h + 2] * o_slc[rs]
                 + gate[:, 3 * h + 2:3 * h + 3] * o_win[rs])
            o_ref[0, :, h * HEAD_DIM:(h + 1) * HEAD_DIM] = o.astype(o_ref.dtype)


def nsa_prompt(z, pe, w1, w2, layer):
    B, T, _ = z.shape
    nq, nk = C_HEADS * HEAD_DIM, C_KV * HEAD_DIM
    q = _bf16(z[..., :nq])
    branch = [z[..., nq + 2 * i * nk:nq + 2 * (i + 1) * nk] for i in range(3)]
    gl = z[..., nq + 6 * nk:]
    as_kv = lambda a: a.reshape(B, T, 2, C_KV, HEAD_DIM)
    n_cmp = (T - CMP_BLK) // CMP_STRIDE + 1
    k_cmp, v_cmp = nsa_compress(as_kv(branch[0]), n_cmp, pe, w1, w2, layer)
    n_pad = -(-n_cmp // LANES) * LANES
    pad = ((0, 0), (0, 0), (0, n_pad - n_cmp), (0, 0))
    kct = _bf16(jnp.pad(k_cmp, pad)).transpose(0, 1, 3, 2).reshape(B, nk, n_pad)
    vc = _bf16(jnp.pad(v_cmp, pad)).transpose(0, 2, 1, 3).reshape(B, n_pad, nk)
    kt = lambda a: _bf16(a[..., :nk]).transpose(0, 2, 1)
    tq = NSA_TQ
    full_t = lambda shape: pl.BlockSpec(shape, lambda b, i: (b, 0, 0))
    o = pl.pallas_call(
        functools.partial(_nsa_prompt_kernel, n_cmp=n_cmp, n_slc=-(-T // SLC_BLK)),
        grid=(B, T // tq),
        in_specs=[
            pl.BlockSpec((1, tq, nq), lambda b, i: (b, i, 0)),
            pl.BlockSpec((1, tq, 3 * C_HEADS), lambda b, i: (b, i, 0)),
            full_t((1, nk, n_pad)), full_t((1, n_pad, nk)),
            full_t((1, nk, T)), full_t((1, T, nk)), full_t((1, nk, T)), full_t((1, T, nk)),
        ],
        out_specs=pl.BlockSpec((1, tq, nq), lambda b, i: (b, i, 0)),
        out_shape=jax.ShapeDtypeStruct((B, T, nq), jnp.bfloat16),
        scratch_shapes=[pltpu.VMEM((C_REP * tq, LANES), jnp.int32), pltpu.VMEM((C_REP * tq, LANES), jnp.float32)],
        compiler_params=_params("parallel", "arbitrary"))(
            q, gl, kct, vc, kt(branch[1]), _bf16(branch[1][..., nk:]), kt(branch[2]), _bf16(branch[2][..., nk:]))
    return o, as_kv(branch[0]), as_kv(branch[1]), as_kv(branch[2])[:, -WINDOW:]


def _page_copy_kernel(pt_ref, pool_ref, o_ref):
    o_ref[0] = pool_ref[...]


def gather_pages_pallas(pool, page_table, layer):
    B, n_pages = page_table.shape
    row = pool.shape[3:]
    return pl.pallas_call(
        _page_copy_kernel,
        grid_spec=pltpu.PrefetchScalarGridSpec(
            num_scalar_prefetch=1, grid=(B, n_pages),
            in_specs=[pl.BlockSpec((None, PAGE_SIZE, None) + row, lambda b, p, pt: (pt[b, p], 0, layer, 0, 0, 0))],
            out_specs=pl.BlockSpec((1, PAGE_SIZE) + row, lambda b, p, pt: (b, p, 0, 0, 0))),
        out_shape=jax.ShapeDtypeStruct((B, n_pages * PAGE_SIZE) + row, pool.dtype),
        compiler_params=_params("parallel", "arbitrary"))(page_table, pool)


def _nsa_sample_cmpwin_kernel(q_ref, kct_ref, vc_ref, win_ref, ocmp_ref, owin_ref, sel_ref, key_ref, *,
                              n_cmp, n_slc, q_pos0, win_pos0, t_new):
    rows_g = C_REP * t_new
    nk = C_KV * HEAD_DIM
    n_pad, n_blk, n_win = kct_ref.shape[2], key_ref.shape[1], win_ref.shape[1]
    iota = lambda shape, ax: lax.broadcasted_iota(jnp.int32, shape, ax)
    tile = lambda a: jnp.concatenate([a] * C_REP, axis=0)
    n_c = iota((t_new, n_pad), 1)
    vis = tile((n_c * CMP_STRIDE + CMP_BLK - 1 <= q_pos0 + iota((t_new, n_pad), 0)) & (n_c < n_cmp))
    n_o, m_o = iota((n_pad, n_blk), 0) * CMP_STRIDE, iota((n_pad, n_blk), 1) * SLC_BLK
    overlap = jnp.where((n_o < m_o + SLC_BLK) & (n_o + CMP_BLK > m_o) & (iota((n_pad, n_blk), 1) < n_slc), 1.0, 0.0)
    cur = jnp.right_shift(q_pos0 + iota((t_new, n_blk), 0), SLC_SHIFT)
    blk = iota((t_new, n_blk), 1)
    d = q_pos0 + iota((t_new, n_win), 0) - (win_pos0 + iota((t_new, n_win), 1))
    win_ok = tile((win_pos0 + iota((t_new, n_win), 1) >= 0) & (d >= 0) & (d < WINDOW))
    for g in range(C_KV):
        rs = slice(g * rows_g, (g + 1) * rows_g)
        qg = q_ref[0, rs]
        s = jnp.dot(qg, kct_ref[0, g * HEAD_DIM:(g + 1) * HEAD_DIM, :], preferred_element_type=jnp.float32) * ATTN_SCALE
        s = jnp.where(vis, s, NEG)
        e = jnp.exp(s - jnp.max(s, axis=-1, keepdims=True))
        p = jnp.where(vis, e / jnp.sum(e, axis=-1, keepdims=True), 0.0)
        ocmp_ref[0, rs] = jnp.dot(_bf16(p), vc_ref[0, :, g * HEAD_DIM:(g + 1) * HEAD_DIM], preferred_element_type=jnp.float32)
        p_grp = sum(p[r * t_new:(r + 1) * t_new] for r in range(C_REP))
        imp = jnp.dot(p_grp, overlap, preferred_element_type=jnp.float32, precision=lax.Precision.HIGHEST)
        imp = jnp.where((blk == 0) | (blk == cur), FORCE, imp)
        key_ref[g * t_new:(g + 1) * t_new, :] = _sortable_key(jnp.where(blk <= cur, imp, NEG))
        kw = _bf16(win_ref[0, :, g * HEAD_DIM:(g + 1) * HEAD_DIM])
        vw = _bf16(win_ref[0, :, nk + g * HEAD_DIM:nk + (g + 1) * HEAD_DIM])
        sw = jnp.where(win_ok, _nt_dot(qg, kw) * ATTN_SCALE, NEG)
        ew = jnp.exp(sw - jnp.max(sw, axis=-1, keepdims=True))
        pw = ew / jnp.sum(ew, axis=-1, keepdims=True)
        owin_ref[0, rs] = jnp.dot(_bf16(pw), vw, preferred_element_type=jnp.float32)
    k_sel = min(N_SEL, n_slc)
    rows = C_KV * t_new
    thr = _kth_largest_key(key_ref, 1, n_blk, k_sel, rows)
    _topk_bias(key_ref, sel_ref.at[0], thr, 1, n_blk, k_sel, rows, jnp.concatenate([cur[:, :LANES]] * C_KV, axis=0))


def _paged_attend_step(q_ref, kv_of, bias_of_group, m_ref, l_ref, acc_ref, n_groups, rows_g):
    for g in range(n_groups):
        rs = slice(g * rows_g, (g + 1) * rows_g)
        s = _nt_dot(q_ref[0, rs], _bf16(kv_of(0, g))) * ATTN_SCALE + bias_of_group(g)
        m = m_ref[rs]
        m_new = jnp.maximum(m, jnp.max(s, axis=-1, keepdims=True))
        pr = jnp.exp(s - m_new)
        alpha = jnp.exp(m - m_new)
        l_ref[rs] = alpha * l_ref[rs] + jnp.sum(pr, axis=-1, keepdims=True)
        acc_ref[rs] = alpha * acc_ref[rs] + jnp.dot(_bf16(pr), _bf16(kv_of(1, g)), preferred_element_type=jnp.float32)
        m_ref[rs] = m_new


def _init_online_softmax(m_ref, l_ref, acc_ref):
    m_ref[...] = jnp.full(m_ref.shape, NEG, jnp.float32)
    l_ref[...] = jnp.zeros(l_ref.shape, jnp.float32)
    acc_ref[...] = jnp.zeros(acc_ref.shape, jnp.float32)


def _nsa_sample_slc_kernel(pt_ref, q_ref, sel_ref, gate_ref, ocmp_ref, owin_ref, *rest, n_pages, t_new):
    pool_refs, (new_ref, o_ref, m_ref, l_ref, acc_ref) = rest[:PAGES_PER_STEP], rest[PAGES_PER_STEP:]
    p = pl.program_id(1)
    n_blk = sel_ref.shape[2]

    @pl.when(p == 0)
    def _():
        _init_online_softmax(m_ref, l_ref, acc_ref)

    def attend(kv_of):
        iota = lambda shape, ax: lax.broadcasted_iota(jnp.int32, shape, ax)
        causal = p * STEP_KEYS + iota((t_new, STEP_KEYS), 1) <= n_pages * PAGE_SIZE + iota((t_new, STEP_KEYS), 0)
        key_blk = (STEP_KEYS // SLC_BLK) * p + jnp.right_shift(iota((n_blk, STEP_KEYS), 1), SLC_SHIFT)
        expand = _bf16(jnp.where(iota((n_blk, STEP_KEYS), 0) == key_blk, 1.0, 0.0))

        def bias_of_group(g):
            sel = _bf16(jnp.where(sel_ref[0, g * t_new:(g + 1) * t_new, :] == 0.0, 1.0, 0.0))
            hit = jnp.dot(sel, expand, preferred_element_type=jnp.float32)
            return jnp.concatenate([jnp.where((hit > 0.5) & causal, 0.0, NEG)] * C_REP, axis=0)

        _paged_attend_step(q_ref, kv_of, bias_of_group, m_ref, l_ref, acc_ref, C_KV, C_REP * t_new)

    @pl.when(p < n_pages // PAGES_PER_STEP)
    def _():
        attend(lambda e, g: jnp.concatenate([r[:, e, g, :] for r in pool_refs], axis=0))

    @pl.when(p == n_pages // PAGES_PER_STEP)
    def _():
        attend(lambda e, g: new_ref[0, :, e, g, :])
        gate = jax.nn.sigmoid(gate_ref[0])
        o = gate[:, 0:1] * ocmp_ref[0] + gate[:, 1:2] * (acc_ref[...] / l_ref[...]) + gate[:, 2:3] * owin_ref[0]
        o_ref[0] = o.astype(o_ref.dtype)


def nsa_sample(z, pe, w1, w2, pool_cmp, pool_slc, win_buf, page_table, layer):
    B, T, _ = z.shape
    n_pages = page_table.shape[1]
    past = n_pages * PAGE_SIZE
    nq, nk = C_HEADS * HEAD_DIM, C_KV * HEAD_DIM
    head_major = lambda a, d: a.reshape(B, T, -1, d).transpose(0, 2, 1, 3).reshape(B, -1, d)
    as_kv = lambda a: a.reshape(B, -1, 2, C_KV, HEAD_DIM)
    q = head_major(_bf16(z[..., :nq]), HEAD_DIM)
    branch = [z[..., nq + 2 * i * nk:nq + 2 * (i + 1) * nk] for i in range(3)]
    gl = head_major(z[..., nq + 6 * nk:], 3)
    cmp_rows = gather_pages_pallas(pool_cmp, page_table, layer)
    n_cmp = (past + T - CMP_BLK) // CMP_STRIDE + 1
    if (n_cmp + 1) * CMP_STRIDE > past:
        cmp_rows = jnp.concatenate([cmp_rows, as_kv(branch[0])], axis=1)
    k_cmp, v_cmp = nsa_compress(cmp_rows, n_cmp, pe, w1, w2, layer)
    n_pad = -(-n_cmp // LANES) * LANES
    pad = ((0, 0), (0, 0), (0, n_pad - n_cmp), (0, 0))
    kct = _bf16(jnp.pad(k_cmp, pad)).transpose(0, 1, 3, 2).reshape(B, nk, n_pad)
    vc = _bf16(jnp.pad(v_cmp, pad)).transpose(0, 2, 1, 3).reshape(B, n_pad, nk)
    n_slc = -(-(past + T) // SLC_BLK)
    n_blk = -(-n_slc // LANES) * LANES
    win_all = jnp.concatenate([win_buf.reshape(B, WINDOW, 2 * nk), branch[2]], axis=1)
    n_win = -(-(WINDOW + T) // LANES) * LANES
    win_pad = jnp.pad(win_all, ((0, 0), (0, n_win - WINDOW - T), (0, 0)))
    per_b = lambda shape: pl.BlockSpec(shape, lambda b: (b, 0, 0))
    hm_rows = C_HEADS * T
    o_cmp, o_win, sel = pl.pallas_call(
        functools.partial(_nsa_sample_cmpwin_kernel, n_cmp=n_cmp, n_slc=n_slc, q_pos0=past, win_pos0=past - WINDOW, t_new=T),
        grid=(B,),
        in_specs=[per_b((1, hm_rows, HEAD_DIM)), per_b((1, nk, n_pad)), per_b((1, n_pad, nk)), per_b((1, n_win, 2 * nk))],
        out_specs=[per_b((1, hm_rows, HEAD_DIM)), per_b((1, hm_rows, HEAD_DIM)), per_b((1, C_KV * T, n_blk))],
        out_shape=[jax.ShapeDtypeStruct((B, hm_rows, HEAD_DIM), jnp.float32)] * 2
        + [jax.ShapeDtypeStruct((B, C_KV * T, n_blk), jnp.float32)],
        scratch_shapes=[pltpu.VMEM((C_KV * T, n_blk), jnp.int32)],
        compiler_params=_params("parallel"))(q, kct, vc, win_pad)
    per_b3 = lambda shape: pl.BlockSpec(shape, lambda b, p, pt: (b, 0, 0))
    assert n_pages % PAGES_PER_STEP == 0 and T <= STEP_KEYS
    new_pad = jnp.pad(as_kv(branch[1]), ((0, 0), (0, STEP_KEYS - T), (0, 0), (0, 0), (0, 0)))
    kv_row = (2, C_KV, HEAD_DIM)
    page_of = lambda b, p, pt, i: pt[b, jnp.minimum(p * PAGES_PER_STEP + i, n_pages - 1)]
    o = pl.pallas_call(
        functools.partial(_nsa_sample_slc_kernel, n_pages=n_pages, t_new=T),
        grid_spec=pltpu.PrefetchScalarGridSpec(
            num_scalar_prefetch=1, grid=(B, n_pages // PAGES_PER_STEP + 1),
            in_specs=[per_b3((1, hm_rows, HEAD_DIM)), per_b3((1, C_KV * T, n_blk)), per_b3((1, hm_rows, 3)),
                      per_b3((1, hm_rows, HEAD_DIM)), per_b3((1, hm_rows, HEAD_DIM))]
            + [pl.BlockSpec((None, PAGE_SIZE, None) + kv_row, lambda b, p, pt, i=i: (page_of(b, p, pt, i), 0, layer, 0, 0, 0))
               for i in range(PAGES_PER_STEP)]
            + [pl.BlockSpec((1, STEP_KEYS) + kv_row, lambda b, p, pt: (b, 0, 0, 0, 0))],
            out_specs=per_b3((1, hm_rows, HEAD_DIM)),
            scratch_shapes=[pltpu.VMEM((hm_rows, 1), jnp.float32), pltpu.VMEM((hm_rows, 1), jnp.float32),
                            pltpu.VMEM((hm_rows, HEAD_DIM), jnp.float32)]),
        out_shape=jax.ShapeDtypeStruct((B, hm_rows, HEAD_DIM), jnp.bfloat16),
        compiler_params=_params("parallel", "arbitrary"))(
            page_table, q, sel, gl, o_cmp, o_win, *[pool_slc] * PAGES_PER_STEP, new_pad)
    o = o.reshape(B, C_HEADS, T, HEAD_DIM).transpose(0, 2, 1, 3).reshape(B, T, nq)
    return o, as_kv(branch[0]), as_kv(branch[1]), as_kv(win_all)[:, -WINDOW:]


def kernel(x_prompt, x_sample, cache_dsa_kv, cache_dsa_idx, state_gdn_conv, state_gdn_rec, cache_nsa_cmp, cache_nsa_slc, cache_nsa_win, page_table, c_prompt, c_sample, ada_w, ada_b, ln_g, ln_b, dsa_w_in, dsa_w_out, gdn_w_in, gdn_conv_w, gdn_a_log, gdn_dt_bias, gdn_norm_g, gdn_w_out, nsa_w_in, nsa_cmp_pe, nsa_cmp_w1, nsa_cmp_w2, nsa_w_out, moe_w_router, moe_b_router, moe_w_in, moe_b_in, moe_w_out, moe_b_out):
    xp, xs = x_prompt, x_sample
    bp, tp, _ = xp.shape
    bs, ts, _ = xs.shape
    n_p, n_s = bp * tp, bs * ts
    outs = {k: [] for k in ("dsa_kv_p", "dsa_kv_s", "dsa_idx_p", "dsa_idx_s", "gdn_conv_p", "gdn_conv_s", "gdn_rec_p",
                            "gdn_rec_s", "nsa_cmp_p", "nsa_cmp_s", "nsa_slc_p", "nsa_slc_s", "nsa_win_p", "nsa_win_s")}
    moe_w = (moe_w_router, moe_b_router, moe_w_in, moe_b_in, moe_w_out, moe_b_out)
    c_all = jax.nn.silu(jnp.concatenate([c_prompt, c_sample], axis=0))
    for i in range(DEPTH):
        kind, j = i % N_MIXERS, i // N_MIXERS
        mod = matmul(c_all, ada_w, i, ada_b).reshape(bp + bs, 6, D_MODEL)
        mp, ms = mod[:bp], mod[bp:]
        if kind == 0:
            w_in, w_out = dsa_w_in, dsa_w_out
        elif kind == 1:
            w_in, w_out = gdn_w_in, gdn_w_out
        else:
            w_in, w_out = nsa_w_in, nsa_w_out
        zp = matmul_modulated(xp, mp[:, 1], mp[:, 0], w_in, j)
        hs = xs * (1.0 + ms[:, 1][:, None, :]) + ms[:, 0][:, None, :]
        zs = matmul(hs.reshape(n_s, D_MODEL), w_in, j).reshape(bs, ts, -1)
        if kind == 0:
            op, kv_p, ki_p = dsa_prompt(zp)
            os_, kv_s, ki_s = dsa_sample(zs, cache_dsa_kv, cache_dsa_idx, page_table, j)
            outs["dsa_kv_p"].append(kv_p)
            outs["dsa_kv_s"].append(kv_s)
            outs["dsa_idx_p"].append(ki_p)
            outs["dsa_idx_s"].append(ki_s)
        elif kind == 1:
            gdn_args = (gdn_conv_w[j], gdn_a_log[j], gdn_dt_bias[j], gdn_norm_g[j])
            zero_conv = jnp.zeros((bp, B_CONV - 1, B_CONV_CH), xp.dtype)
            zero_rec = jnp.zeros((bp, B_V_HEADS, B_DK, B_DV), xp.dtype)
            op, conv_p, rec_p = gdn_mixer(zp, *gdn_args, zero_conv, zero_rec)
            os_, conv_s, rec_s = gdn_mixer(zs, *gdn_args, state_gdn_conv[:, j], state_gdn_rec[:, j])
            outs["gdn_conv_p"].append(conv_p)
            outs["gdn_conv_s"].append(conv_s)
            outs["gdn_rec_p"].append(rec_p)
            outs["gdn_rec_s"].append(rec_s)
        else:
            nsa_args = (nsa_cmp_pe, nsa_cmp_w1, nsa_cmp_w2)
            op, cmp_p, slc_p, win_p = nsa_prompt(zp, *nsa_args, j)
            os_, cmp_s, slc_s, win_s = nsa_sample(zs, *nsa_args, cache_nsa_cmp, cache_nsa_slc, cache_nsa_win[:, j],
                                                  page_table, j)
            outs["nsa_cmp_p"].append(cmp_p)
            outs["nsa_cmp_s"].append(cmp_s)
            outs["nsa_slc_p"].append(slc_p)
            outs["nsa_slc_s"].append(slc_s)
            outs["nsa_win_p"].append(win_p)
            outs["nsa_win_s"].append(win_s)
        yp = matmul(op.reshape(n_p, -1), w_out, j).reshape(bp, tp, D_MODEL)
        ys = matmul(os_.reshape(n_s, -1), w_out, j).reshape(bs, ts, D_MODEL)
        xp, hp = post_norm_modulate(xp, yp, mp[:, 2], ln_g[i, 0], ln_b[i, 0], mp[:, 4], mp[:, 3])
        xs, hs = post_norm_modulate(xs, ys, ms[:, 2], ln_g[i, 0], ln_b[i, 0], ms[:, 4], ms[:, 3])
        h_all = jnp.concatenate([hp.reshape(n_p, D_MODEL), hs.reshape(n_s, D_MODEL)], axis=0)
        y_k, w_k = moe(h_all, i, *moe_w)
        xp = post_norm_moe(xp, y_k, w_k, 0, mp[:, 5], ln_g[i, 1], ln_b[i, 1])
        xs = post_norm_moe(xs, y_k, w_k, n_p, ms[:, 5], ln_g[i, 1], ln_b[i, 1])
    st = lambda k, ax: jnp.stack(outs[k], axis=ax)
    return (xp, xs,
            st("dsa_kv_p", 2), st("dsa_kv_s", 2), st("dsa_idx_p", 2), st("dsa_idx_s", 2),
            st("gdn_conv_p", 1), st("gdn_conv_s", 1), st("gdn_rec_p", 1), st("gdn_rec_s", 1),
            st("nsa_cmp_p", 2), st("nsa_cmp_s", 2), st("nsa_slc_p", 2), st("nsa_slc_s", 2),
            st("nsa_win_p", 1), st("nsa_win_s", 1))
```

```python
import functools
import math

import jax
import jax.numpy as jnp
import numpy as np
from jax import lax
from jax.experimental import pallas as pl
from jax.experimental.pallas import tpu as pltpu

D_MODEL = 2048
DEPTH = 4
PAGE_SIZE = 128
N_MIXERS = 3
ALPHA = (2.0 * DEPTH) ** 0.25
LN_EPS = 1e-5
NORM_EPS = 1e-6
NEG = -1e30
FORCE = 1e9
HEAD_DIM = 128
ATTN_SCALE = HEAD_DIM ** -0.5

A_HEADS = D_MODEL // HEAD_DIM
A_KV = 4
A_REP = A_HEADS // A_KV
A_IDX_HEADS = 16
A_IDX_DIM = 64
A_TOPK = 256

B_QK_HEADS = 16
B_V_HEADS = 32
B_DK = 128
B_DV = 128
B_CONV = 4
B_CHUNK = 64
B_CONV_CH = 2 * B_QK_HEADS * B_DK + B_V_HEADS * B_DV

C_HEADS = D_MODEL // HEAD_DIM
C_KV = 4
C_REP = C_HEADS // C_KV
CMP_BLK = 32
CMP_STRIDE = 16
SLC_BLK = 64
N_SEL = 16
WINDOW = 512

N_EXPERTS = 32
TOP_K = 4
D_FF = 2048
SWIGLU_LIMIT = 7.0
SWIGLU_ALPHA = 1.702

VMEM_LIMIT_BYTES = 56 * 1024 * 1024
MOE_ROW_BLK = 256
MOE_FF_TILE = 1024
MOE_OUT_TILE = 2048


def _params(*sem):
    return pltpu.CompilerParams(dimension_semantics=sem, vmem_limit_bytes=VMEM_LIMIT_BYTES)


def _bf16(x):
    return x.astype(jnp.bfloat16)


def _mm_kernel(x_ref, w_ref, o_ref):
    o_ref[...] = jnp.dot(_bf16(x_ref[...]), _bf16(w_ref[...]), preferred_element_type=jnp.float32)


def _mm_bias_kernel(x_ref, w_ref, b_ref, o_ref):
    o_ref[...] = jnp.dot(_bf16(x_ref[...]), _bf16(w_ref[...]), preferred_element_type=jnp.float32) + b_ref[...]


def _mm_mod_kernel(x_ref, sc_ref, sh_ref, w_ref, o_ref):
    h = x_ref[0] * (1.0 + sc_ref[0]) + sh_ref[0]
    o_ref[0] = jnp.dot(_bf16(h), _bf16(w_ref[...]), preferred_element_type=jnp.float32)


def matmul(x, w, layer, b=None, tm=512, tn=512):
    m, k = x.shape
    n = w.shape[2]
    tm = min(tm, m)
    tn = min(tn, n)
    grid = (pl.cdiv(m, tm), pl.cdiv(n, tn))
    in_specs = [pl.BlockSpec((tm, k), lambda i, j: (i, 0)), pl.BlockSpec((None, k, tn), lambda i, j: (layer, 0, j))]
    args = [x, w]
    body = _mm_kernel
    if b is not None:
        in_specs.append(pl.BlockSpec((None, 1, tn), lambda i, j: (layer, 0, j)))
        args.append(b.reshape(b.shape[0], 1, n))
        body = _mm_bias_kernel
    return pl.pallas_call(
        body, grid=grid, in_specs=in_specs,
        out_specs=pl.BlockSpec((tm, tn), lambda i, j: (i, j)),
        out_shape=jax.ShapeDtypeStruct((m, n), jnp.float32),
        compiler_params=_params("parallel", "parallel"))(*args)


def matmul_modulated(x, scale, shift, w, layer, tm=512, tn=512):
    bsz, t, k = x.shape
    n = w.shape[2]
    tm = min(tm, t)
    tn = min(tn, n)
    grid = (bsz, pl.cdiv(t, tm), pl.cdiv(n, tn))
    mod_spec = pl.BlockSpec((1, 1, k), lambda b, i, j: (b, 0, 0))
    return pl.pallas_call(
        _mm_mod_kernel, grid=grid,
        in_specs=[pl.BlockSpec((1, tm, k), lambda b, i, j: (b, i, 0)), mod_spec, mod_spec,
                  pl.BlockSpec((None, k, tn), lambda b, i, j: (layer, 0, j))],
        out_specs=pl.BlockSpec((1, tm, tn), lambda b, i, j: (b, i, j)),
        out_shape=jax.ShapeDtypeStruct((bsz, t, n), jnp.float32),
        compiler_params=_params("parallel", "parallel", "parallel"))(
            x, scale.reshape(bsz, 1, k), shift.reshape(bsz, 1, k), w)


def _layer_norm(z, g, b):
    mu = jnp.mean(z, axis=-1, keepdims=True)
    zc = z - mu
    var = jnp.mean(zc * zc, axis=-1, keepdims=True)
    return zc * lax.rsqrt(var + LN_EPS) * g + b


def _postnorm_kernel(x_ref, y_ref, gate_ref, g_ref, b_ref, sc_ref, sh_ref, xo_ref, ho_ref):
    xn = _layer_norm(ALPHA * x_ref[0] + gate_ref[0] * y_ref[0], g_ref[...], b_ref[...])
    xo_ref[0] = xn
    ho_ref[0] = (xn * (1.0 + sc_ref[0]) + sh_ref[0]).astype(ho_ref.dtype)


def post_norm_modulate(x, y, gate, g, b, scale, shift, tm=256):
    bsz, t, d = x.shape
    tm = min(tm, t)
    row = pl.BlockSpec((1, tm, d), lambda bi, i: (bi, i, 0))
    per_seq = pl.BlockSpec((1, 1, d), lambda bi, i: (bi, 0, 0))
    shared = pl.BlockSpec((1, d), lambda bi, i: (0, 0))
    return pl.pallas_call(
        _postnorm_kernel, grid=(bsz, t // tm),
        in_specs=[row, row, per_seq, shared, shared, per_seq, per_seq],
        out_specs=[row, row],
        out_shape=[jax.ShapeDtypeStruct(x.shape, jnp.float32), jax.ShapeDtypeStruct(x.shape, jnp.bfloat16)],
        compiler_params=_params("parallel", "parallel"))(
            x, y, gate.reshape(bsz, 1, d), g.reshape(1, d), b.reshape(1, d),
            scale.reshape(bsz, 1, d), shift.reshape(bsz, 1, d))


def _postnorm_moe_kernel(x_ref, y0_ref, y1_ref, y2_ref, y3_ref, w_ref, gate_ref, g_ref, b_ref, xo_ref):
    w = w_ref[...]
    y = sum(w[:, k:k + 1] * r[...].astype(jnp.float32) for k, r in enumerate((y0_ref, y1_ref, y2_ref, y3_ref)))
    xo_ref[0] = _layer_norm(ALPHA * x_ref[0] + gate_ref[0] * y, g_ref[...], b_ref[...])


def post_norm_moe(x, ys, w, row0, gate, g, b, tm=256):
    bsz, t, d = x.shape
    tm = min(tm, t)
    per_seq_blocks = t // tm
    assert row0 % tm == 0
    row = pl.BlockSpec((1, tm, d), lambda bi, i: (bi, i, 0))
    tok = lambda width: pl.BlockSpec((tm, width), lambda bi, i: (row0 // tm + bi * per_seq_blocks + i, 0))
    per_seq = pl.BlockSpec((1, 1, d), lambda bi, i: (bi, 0, 0))
    shared = pl.BlockSpec((1, d), lambda bi, i: (0, 0))
    return pl.pallas_call(
        _postnorm_moe_kernel, grid=(bsz, per_seq_blocks),
        in_specs=[row] + [tok(d)] * TOP_K + [tok(TOP_K), per_seq, shared, shared],
        out_specs=row,
        out_shape=jax.ShapeDtypeStruct(x.shape, jnp.float32),
        compiler_params=_params("parallel", "parallel"))(x, *ys, w, gate.reshape(bsz, 1, d), g.reshape(1, d), b.reshape(1, d))


def _moe_in_kernel(blk_e_ref, n_used_ref, x_ref, wg_ref, wl_ref, bg_ref, bl_ref, o_ref):
    used = pl.program_id(1) < n_used_ref[0]

    @pl.when(used)
    def _():
        x = x_ref[...]
        glu = jnp.dot(x, _bf16(wg_ref[...]), preferred_element_type=jnp.float32) + bg_ref[...]
        lin = jnp.dot(x, _bf16(wl_ref[...]), preferred_element_type=jnp.float32) + bl_ref[...]
        glu = jnp.minimum(glu, SWIGLU_LIMIT)
        lin = jnp.clip(lin, -SWIGLU_LIMIT, SWIGLU_LIMIT)
        o_ref[...] = (glu * jax.nn.sigmoid(SWIGLU_ALPHA * glu) * (lin + 1.0)).astype(o_ref.dtype)

    @pl.when(jnp.logical_not(used))
    def _():
        o_ref[...] = jnp.zeros(o_ref.shape, o_ref.dtype)


def _moe_out_kernel(blk_e_ref, n_used_ref, h_ref, w_ref, b_ref, o_ref):
    used = pl.program_id(1) < n_used_ref[0]

    @pl.when(used)
    def _():
        y = jnp.dot(h_ref[...], _bf16(w_ref[...]), preferred_element_type=jnp.float32) + b_ref[...]
        o_ref[...] = y.astype(o_ref.dtype)

    @pl.when(jnp.logical_not(used))
    def _():
        o_ref[...] = jnp.zeros(o_ref.shape, o_ref.dtype)


def moe_ffn(xs, blk_expert, n_used, layer, w_in, b_in, w_out, b_out):
    r = xs.shape[0]
    n_blk = r // MOE_ROW_BLK
    n_ff = D_FF // MOE_FF_TILE
    tm, tf, tn = MOE_ROW_BLK, MOE_FF_TILE, MOE_OUT_TILE
    b_in4 = b_in.reshape(DEPTH, N_EXPERTS, 1, 2 * D_FF)
    b_out4 = b_out.reshape(DEPTH, N_EXPERTS, 1, D_MODEL)
    row_blk = lambda j, m, be, nu: (jnp.minimum(m, nu[0] - 1), 0)
    act = pl.pallas_call(
        _moe_in_kernel,
        grid_spec=pltpu.PrefetchScalarGridSpec(
            num_scalar_prefetch=2, grid=(n_ff, n_blk),
            in_specs=[
                pl.BlockSpec((tm, D_MODEL), row_blk),
                pl.BlockSpec((None, None, D_MODEL, tf), lambda j, m, be, nu: (layer, be[m], 0, j)),
                pl.BlockSpec((None, None, D_MODEL, tf), lambda j, m, be, nu: (layer, be[m], 0, n_ff + j)),
                pl.BlockSpec((None, None, 1, tf), lambda j, m, be, nu: (layer, be[m], 0, j)),
                pl.BlockSpec((None, None, 1, tf), lambda j, m, be, nu: (layer, be[m], 0, n_ff + j)),
            ],
            out_specs=pl.BlockSpec((tm, tf), lambda j, m, be, nu: (m, j))),
        out_shape=jax.ShapeDtypeStruct((r, D_FF), jnp.bfloat16),
        compiler_params=_params("arbitrary", "arbitrary"))(blk_expert, n_used, xs, w_in, w_in, b_in4, b_in4)
    return pl.pallas_call(
        _moe_out_kernel,
        grid_spec=pltpu.PrefetchScalarGridSpec(
            num_scalar_prefetch=2, grid=(D_MODEL // tn, n_blk),
            in_specs=[
                pl.BlockSpec((tm, D_FF), row_blk),
                pl.BlockSpec((None, None, D_FF, tn), lambda j, m, be, nu: (layer, be[m], 0, j)),
                pl.BlockSpec((None, None, 1, tn), lambda j, m, be, nu: (layer, be[m], 0, j)),
            ],
            out_specs=pl.BlockSpec((tm, tn), lambda j, m, be, nu: (m, j))),
        out_shape=jax.ShapeDtypeStruct((r, D_MODEL), jnp.bfloat16),
        compiler_params=_params("arbitrary", "arbitrary"))(blk_expert, n_used, act, w_out, b_out4)


def moe(x, layer, w_router, b_router, w_in, b_in, w_out, b_out):
    n_tok = x.shape[0]
    n_asg = n_tok * TOP_K
    logits = jnp.dot(x.astype(jnp.float32), w_router[layer], precision=lax.Precision.HIGHEST) + b_router[layer]
    top_val, top_idx = lax.top_k(logits, TOP_K)
    gate = jax.nn.softmax(top_val, axis=-1)
    assert n_asg % MOE_ROW_BLK == 0
    onehot = (top_idx.reshape(-1, MOE_ROW_BLK, 1) == jnp.arange(N_EXPERTS)).astype(jnp.float32)
    tri = jnp.tril(jnp.ones((MOE_ROW_BLK, MOE_ROW_BLK), jnp.float32), -1)
    in_blk = jnp.einsum("ij,bje->bie", tri, onehot)
    blk_tot = jnp.sum(onehot, axis=1)
    before = jnp.cumsum(blk_tot, axis=0) - blk_tot
    rank = jnp.sum(onehot * (in_blk + before[:, None, :]), axis=-1).reshape(-1).astype(jnp.int32)
    counts = jnp.sum(blk_tot, axis=0).astype(jnp.int32)
    blocks = (counts + MOE_ROW_BLK - 1) // MOE_ROW_BLK
    blk_end = jnp.cumsum(blocks)
    dest = (blk_end - blocks)[top_idx.reshape(-1)] * MOE_ROW_BLK + rank
    n_blk = n_asg // MOE_ROW_BLK + N_EXPERTS
    row_tok = jnp.zeros((n_blk * MOE_ROW_BLK,), jnp.int32).at[dest].set(jnp.arange(n_asg, dtype=jnp.int32) // TOP_K)
    n_used = blk_end[-1:].astype(jnp.int32)
    blk_of = jnp.minimum(jnp.arange(n_blk, dtype=jnp.int32), n_used - 1)
    blk_expert = jnp.sum((blk_end[None, :] <= blk_of[:, None]).astype(jnp.int32), axis=1)
    xs = x[row_tok]
    ys = moe_ffn(xs, blk_expert, n_used, layer, w_in, b_in, w_out, b_out)
    dest = dest.reshape(n_tok, TOP_K)
    return [ys[dest[:, k]] for k in range(TOP_K)], gate


def split_cols(z, sizes):
    return jnp.split(z, np.cumsum(sizes)[:-1].tolist(), axis=-1)


def rms_norm(x, g):
    return x * lax.rsqrt(jnp.mean(jnp.square(x), axis=-1, keepdims=True) + NORM_EPS) * g


def l2_normalize(x):
    return x * lax.rsqrt(jnp.sum(jnp.square(x), axis=-1, keepdims=True) + NORM_EPS)


INT32_MIN = -2 ** 31
LANES = 128
PAGES_PER_STEP = 4
STEP_KEYS = PAGES_PER_STEP * PAGE_SIZE


def _sortable_key(x):
    bits = pltpu.bitcast(x, jnp.int32)
    return jnp.where(bits < 0, bits ^ 0x7FFFFFFF, bits)


def _kth_largest_key(key_ref, n_chunks, chunk, k, rows):
    def count_ge(cand):
        def body(c, acc):
            off = pl.multiple_of(c * chunk, chunk)
            for j in range(chunk // LANES):
                keys = key_ref[:, pl.ds(off + j * LANES, LANES)]
                acc = acc + jnp.where(keys >= cand, 1.0, 0.0)
            return acc

        acc = lax.fori_loop(0, n_chunks, body, jnp.zeros((rows, LANES), jnp.float32))
        return jnp.sum(acc, axis=-1, keepdims=True)

    def bit_step(b, ans):
        cand = ans + lax.shift_left(jnp.int32(1), 31 - b)
        return jnp.where(count_ge(cand) >= k, cand, ans)

    return lax.fori_loop(0, 32, bit_step, jnp.full((rows, 1), INT32_MIN, jnp.int32))


def _topk_bias(key_ref, bias_ref, thr, n_chunks, chunk, k, rows, row_pos):
    tri = (lax.broadcasted_iota(jnp.int32, (LANES, LANES), 0)
           <= lax.broadcasted_iota(jnp.int32, (LANES, LANES), 1)).astype(jnp.bfloat16)
    col = lax.broadcasted_iota(jnp.int32, (rows, LANES), 1)

    def count_gt(c, acc):
        off = pl.multiple_of(c * chunk, chunk)
        for j in range(chunk // LANES):
            acc = acc + jnp.where(key_ref[:, pl.ds(off + j * LANES, LANES)] > thr, 1.0, 0.0)
        return acc

    n_gt = jnp.sum(lax.fori_loop(0, n_chunks, count_gt, jnp.zeros((rows, LANES), jnp.float32)), axis=-1, keepdims=True)
    need = k - n_gt

    def body(c, run):
        off = pl.multiple_of(c * chunk, chunk)
        for j in range(chunk // LANES):
            keys = key_ref[:, pl.ds(off + j * LANES, LANES)]
            eq = keys == thr
            eq_f = jnp.where(eq, 1.0, 0.0)
            incl = jnp.dot(eq_f.astype(jnp.bfloat16), tri, preferred_element_type=jnp.float32)
            sel = (keys > thr) | (eq & (run + incl - eq_f < need))
            ok = sel & (off + j * LANES + col <= row_pos)
            bias_ref[:, pl.ds(off + j * LANES, LANES)] = jnp.where(ok, 0.0, NEG)
            run = run + incl[:, LANES - 1:LANES]
        return run

    lax.fori_loop(0, n_chunks, body, jnp.zeros((rows, 1), jnp.float32))


def _stack_heads(q_ref, g, rep):
    return jnp.concatenate([q_ref[0, :, (g * rep + r) * HEAD_DIM:(g * rep + r + 1) * HEAD_DIM] for r in range(rep)], axis=0)


def _flash_group(qg, kt_ref, v_ref, g, lo, hi, chunk, bias_fn):
    rows = qg.shape[0]

    def body(c, carry):
        m, l, acc = carry
        off = pl.multiple_of(c * chunk, chunk)
        kt = kt_ref[0, g * HEAD_DIM:(g + 1) * HEAD_DIM, pl.ds(off, chunk)]
        s = jnp.dot(qg, kt, preferred_element_type=jnp.float32) * ATTN_SCALE + bias_fn(off)
        m_new = jnp.maximum(m, jnp.max(s, axis=-1, keepdims=True))
        p = jnp.exp(s - m_new)
        alpha = jnp.exp(m - m_new)
        l = alpha * l + jnp.sum(p, axis=-1, keepdims=True)
        v = v_ref[0, pl.ds(off, chunk), g * HEAD_DIM:(g + 1) * HEAD_DIM]
        acc = alpha * acc + jnp.dot(_bf16(p), v, preferred_element_type=jnp.float32)
        return m_new, l, acc

    init = (jnp.full((rows, 1), NEG, jnp.float32), jnp.zeros((rows, 1), jnp.float32),
            jnp.zeros((rows, HEAD_DIM), jnp.float32))
    _, l, acc = lax.fori_loop(lo, hi, body, init)
    return acc / l


DSA_TQ = 128
DSA_TK = 512


def _dsa_prompt_kernel(q_ref, qi_ref, wi_ref, kit_ref, kt_ref, v_ref, o_ref, key_ref, bias_ref, *, topk):
    tq, tk = DSA_TQ, DSA_TK
    q_start = pl.program_id(1) * tq
    n_chunks = (q_start + tq + tk - 1) // tk
    row_pos = q_start + lax.broadcasted_iota(jnp.int32, (tq, tk), 0)
    col = lax.broadcasted_iota(jnp.int32, (tq, tk), 1)
    wi = wi_ref[0]

    def score_chunk(c, carry):
        off = pl.multiple_of(c * tk, tk)
        kit = kit_ref[0, :, pl.ds(off, tk)]
        acc = jnp.zeros((tq, tk), jnp.float32)
        for h in range(A_IDX_HEADS):
            s = jnp.dot(qi_ref[0, h], kit, preferred_element_type=jnp.float32)
            acc = acc + jnp.maximum(s, 0.0) * wi[:, h:h + 1]
        key_ref[:, pl.ds(off, tk)] = _sortable_key(jnp.where(off + col <= row_pos, acc, NEG))
        return carry

    lax.fori_loop(0, n_chunks, score_chunk, 0)
    thr = _kth_largest_key(key_ref, n_chunks, tk, topk, tq)
    _topk_bias(key_ref, bias_ref, thr, n_chunks, tk, topk, tq, row_pos[:, :LANES])
    bias_fn = lambda off: jnp.concatenate([bias_ref[:, pl.ds(off, tk)]] * A_REP, axis=0)
    for g in range(A_KV):
        out = _flash_group(_stack_heads(q_ref, g, A_REP), kt_ref, v_ref, g, 0, n_chunks, tk, bias_fn)
        for r in range(A_REP):
            h = g * A_REP + r
            o_ref[0, :, h * HEAD_DIM:(h + 1) * HEAD_DIM] = out[r * tq:(r + 1) * tq].astype(o_ref.dtype)


def dsa_prompt(z):
    B, T, _ = z.shape
    topk = min(A_TOPK, T // 4)
    nq, nk = A_HEADS * HEAD_DIM, A_KV * HEAD_DIM
    q = _bf16(z[..., :nq])
    k, v = z[..., nq:nq + nk], z[..., nq + nk:nq + 2 * nk]
    off = nq + 2 * nk
    qi = _bf16(z[..., off:off + A_IDX_HEADS * A_IDX_DIM]).reshape(B, T, A_IDX_HEADS, A_IDX_DIM).transpose(0, 2, 1, 3)
    off += A_IDX_HEADS * A_IDX_DIM
    ki = z[..., off:off + A_IDX_DIM]
    wi = z[..., off + A_IDX_DIM:] * (A_IDX_HEADS ** -0.5 * A_IDX_DIM ** -0.5)
    kit = _bf16(ki).transpose(0, 2, 1)
    kt = _bf16(k).transpose(0, 2, 1)
    tq = DSA_TQ
    o = pl.pallas_call(
        functools.partial(_dsa_prompt_kernel, topk=topk),
        grid=(B, T // tq),
        in_specs=[
            pl.BlockSpec((1, tq, nq), lambda b, i: (b, i, 0)),
            pl.BlockSpec((1, A_IDX_HEADS, tq, A_IDX_DIM), lambda b, i: (b, 0, i, 0)),
            pl.BlockSpec((1, tq, A_IDX_HEADS), lambda b, i: (b, i, 0)),
            pl.BlockSpec((1, A_IDX_DIM, T), lambda b, i: (b, 0, 0)),
            pl.BlockSpec((1, nk, T), lambda b, i: (b, 0, 0)),
            pl.BlockSpec((1, T, nk), lambda b, i: (b, 0, 0)),
        ],
        out_specs=pl.BlockSpec((1, tq, nq), lambda b, i: (b, i, 0)),
        out_shape=jax.ShapeDtypeStruct((B, T, nq), jnp.bfloat16),
        scratch_shapes=[pltpu.VMEM((tq, T), jnp.int32), pltpu.VMEM((tq, T), jnp.float32)],
        compiler_params=_params("parallel", "arbitrary"))(q, qi, wi, kit, kt, _bf16(v))
    kv = jnp.stack([k.reshape(B, T, A_KV, HEAD_DIM), v.reshape(B, T, A_KV, HEAD_DIM)], axis=2)
    return o, kv, ki


def _nt_dot(a, b):
    return lax.dot_general(a, b, (((1,), (1,)), ((), ())), preferred_element_type=jnp.float32)


def _dsa_sample_score_kernel(pt_ref, qi_ref, wi_ref, *rest, layer, n_pages, t_new):
    pool_refs, (new_ref, o_ref) = rest[:PAGES_PER_STEP], rest[PAGES_PER_STEP:]
    p = pl.program_id(1)

    def scores(ki):
        s = jnp.maximum(_nt_dot(qi_ref[0], _bf16(ki)), 0.0) * wi_ref[0]
        acc = s[0:t_new]
        for h in range(1, A_IDX_HEADS):
            acc = acc + s[h * t_new:(h + 1) * t_new]
        k_pos = p * STEP_KEYS + lax.broadcasted_iota(jnp.int32, (t_new, STEP_KEYS), 1)
        q_pos = n_pages * PAGE_SIZE + lax.broadcasted_iota(jnp.int32, (t_new, STEP_KEYS), 0)
        o_ref[0] = jnp.where(k_pos <= q_pos, acc, NEG)

    @pl.when(p < n_pages // PAGES_PER_STEP)
    def _():
        scores(jnp.concatenate([r[:, layer, :] for r in pool_refs], axis=0))

    @pl.when(p == n_pages // PAGES_PER_STEP)
    def _():
        scores(new_ref[0])


def _dsa_sample_select_kernel(s_ref, bias_ref, key_ref, *, topk, chunk, q_pos0):
    rows, n_keys = key_ref.shape
    n_chunks = n_keys // chunk
    for c in range(n_chunks):
        key_ref[:, c * chunk:(c + 1) * chunk] = _sortable_key(s_ref[0, :, c * chunk:(c + 1) * chunk])
    thr = _kth_largest_key(key_ref, n_chunks, chunk, topk, rows)
    q_pos = q_pos0 + lax.broadcasted_iota(jnp.int32, (rows, LANES), 0)
    _topk_bias(key_ref, bias_ref.at[0], thr, n_chunks, chunk, topk, rows, q_pos)


def _dsa_sample_attend_kernel(pt_ref, q_ref, bias_ref, *rest, n_pages, t_new):
    pool_refs, (new_ref, o_ref, m_ref, l_ref, acc_ref) = rest[:PAGES_PER_STEP], rest[PAGES_PER_STEP:]
    p = pl.program_id(1)

    @pl.when(p == 0)
    def _():
        _init_online_softmax(m_ref, l_ref, acc_ref)

    def attend(kv_of):
        bias = jnp.concatenate([bias_ref[0]] * A_REP, axis=0)
        _paged_attend_step(q_ref, kv_of, lambda g: bias, m_ref, l_ref, acc_ref, A_KV, A_REP * t_new)

    @pl.when(p < n_pages // PAGES_PER_STEP)
    def _():
        attend(lambda e, g: jnp.concatenate([r[:, e, g, :] for r in pool_refs], axis=0))

    @pl.when(p == n_pages // PAGES_PER_STEP)
    def _():
        attend(lambda e, g: new_ref[0, :, e, g, :])
        o_ref[0] = (acc_ref[...] / l_ref[...]).astype(o_ref.dtype)


def dsa_sample(z, pool_kv, pool_idx, page_table, layer):
    B, T, _ = z.shape
    n_pages = page_table.shape[1]
    past = n_pages * PAGE_SIZE
    topk = min(A_TOPK, (past + T) // 4)
    nq, nk = A_HEADS * HEAD_DIM, A_KV * HEAD_DIM
    n_layers = pool_kv.shape[2]
    k, v = z[..., nq:nq + nk], z[..., nq + nk:nq + 2 * nk]
    off = nq + 2 * nk
    head_major = lambda a, d: a.reshape(B, T, -1, d).transpose(0, 2, 1, 3).reshape(B, -1, d)
    q = head_major(_bf16(z[..., :nq]), HEAD_DIM)
    qi = head_major(_bf16(z[..., off:off + A_IDX_HEADS * A_IDX_DIM]), A_IDX_DIM)
    off += A_IDX_HEADS * A_IDX_DIM
    ki = z[..., off:off + A_IDX_DIM]
    wi = head_major(z[..., off + A_IDX_DIM:] * (A_IDX_HEADS ** -0.5 * A_IDX_DIM ** -0.5), 1)
    assert n_pages % PAGES_PER_STEP == 0 and T <= STEP_KEYS
    n_steps = n_pages // PAGES_PER_STEP + 1
    n_keys = n_steps * STEP_KEYS
    pad_rows = lambda a: jnp.pad(a, ((0, 0), (0, STEP_KEYS - T)) + ((0, 0),) * (a.ndim - 2))
    page_of = lambda b, p, pt, i: pt[b, jnp.minimum(p * PAGES_PER_STEP + i, n_pages - 1)]
    per_b = lambda b, p, pt: (b, 0, 0)
    scores = pl.pallas_call(
        functools.partial(_dsa_sample_score_kernel, layer=layer, n_pages=n_pages, t_new=T),
        grid_spec=pltpu.PrefetchScalarGridSpec(
            num_scalar_prefetch=1, grid=(B, n_steps),
            in_specs=[pl.BlockSpec((1, A_IDX_HEADS * T, A_IDX_DIM), per_b),
                      pl.BlockSpec((1, A_IDX_HEADS * T, 1), per_b)]
            + [pl.BlockSpec((None, PAGE_SIZE, n_layers, A_IDX_DIM), lambda b, p, pt, i=i: (page_of(b, p, pt, i), 0, 0, 0))
               for i in range(PAGES_PER_STEP)]
            + [pl.BlockSpec((1, STEP_KEYS, A_IDX_DIM), per_b)],
            out_specs=pl.BlockSpec((1, T, STEP_KEYS), lambda b, p, pt: (b, 0, p))),
        out_shape=jax.ShapeDtypeStruct((B, T, n_keys), jnp.float32),
        compiler_params=_params("parallel", "arbitrary"))(page_table, qi, wi, *[pool_idx] * PAGES_PER_STEP, pad_rows(ki))
    chunk = STEP_KEYS
    bias = pl.pallas_call(
        functools.partial(_dsa_sample_select_kernel, topk=topk, chunk=chunk, q_pos0=past),
        grid=(B,),
        in_specs=[pl.BlockSpec((1, T, n_keys), lambda b: (b, 0, 0))],
        out_specs=pl.BlockSpec((1, T, n_keys), lambda b: (b, 0, 0)),
        out_shape=jax.ShapeDtypeStruct((B, T, n_keys), jnp.float32),
        scratch_shapes=[pltpu.VMEM((T, n_keys), jnp.int32)],
        compiler_params=_params("parallel"))(scores)
    kv_new = jnp.stack([k.reshape(B, T, A_KV, HEAD_DIM), v.reshape(B, T, A_KV, HEAD_DIM)], axis=2)
    o = pl.pallas_call(
        functools.partial(_dsa_sample_attend_kernel, n_pages=n_pages, t_new=T),
        grid_spec=pltpu.PrefetchScalarGridSpec(
            num_scalar_prefetch=1, grid=(B, n_steps),
            in_specs=[pl.BlockSpec((1, A_HEADS * T, HEAD_DIM), per_b),
                      pl.BlockSpec((1, T, STEP_KEYS), lambda b, p, pt: (b, 0, p))]
            + [pl.BlockSpec((None, PAGE_SIZE, None, 2, A_KV, HEAD_DIM),
                            lambda b, p, pt, i=i: (page_of(b, p, pt, i), 0, layer, 0, 0, 0)) for i in range(PAGES_PER_STEP)]
            + [pl.BlockSpec((1, STEP_KEYS, 2, A_KV, HEAD_DIM), lambda b, p, pt: (b, 0, 0, 0, 0))],
            out_specs=pl.BlockSpec((1, A_HEADS * T, HEAD_DIM), per_b),
            scratch_shapes=[pltpu.VMEM((A_HEADS * T, 1), jnp.float32), pltpu.VMEM((A_HEADS * T, 1), jnp.float32),
                            pltpu.VMEM((A_HEADS * T, HEAD_DIM), jnp.float32)]),
        out_shape=jax.ShapeDtypeStruct((B, A_HEADS * T, HEAD_DIM), jnp.bfloat16),
        compiler_params=_params("parallel", "arbitrary"))(
            page_table, q, bias, *[pool_kv] * PAGES_PER_STEP, pad_rows(kv_new))
    o = o.reshape(B, A_HEADS, T, HEAD_DIM).transpose(0, 2, 1, 3).reshape(B, T, nq)
    return o, kv_new, ki


TRI_PACK = 4
TRI_GROUPS = 4


def _dot_bf16x3(a, b):
    a_hi, b_hi = _bf16(a), _bf16(b)
    a_lo, b_lo = _bf16(a - a_hi.astype(jnp.float32)), _bf16(b - b_hi.astype(jnp.float32))
    dot = functools.partial(jnp.dot, preferred_element_type=jnp.float32)
    return dot(a_hi, b_hi) + (dot(a_hi, b_lo) + dot(a_lo, b_hi))


def _unit_lower_inverse_kernel(a_ref, o_ref):
    c = a_ref.shape[-1]
    size = TRI_PACK * c
    eye = jnp.where(lax.broadcasted_iota(jnp.int32, (size, size), 0) == lax.broadcasted_iota(jnp.int32, (size, size), 1), 1.0, 0.0)
    zero = jnp.zeros((c, c), jnp.float32)
    for i in range(TRI_GROUPS):
        n = jnp.concatenate([jnp.concatenate([-a_ref[i, j] if jj == j else zero for jj in range(TRI_PACK)], axis=1)
                             for j in range(TRI_PACK)], axis=0)
        t, p = eye + n, n
        for _ in range((c - 1).bit_length() - 1):
            p = _dot_bf16x3(p, p)
            t = t + _dot_bf16x3(t, p)
        for j in range(TRI_PACK):
            o_ref[i, j] = t[j * c:(j + 1) * c, j * c:(j + 1) * c]


def unit_lower_inverse(a):
    c = a.shape[-1]
    n = math.prod(a.shape[:-2])
    per_step = TRI_GROUPS * TRI_PACK
    assert n % per_step == 0
    blk = (TRI_GROUPS, TRI_PACK, c, c)
    out = pl.pallas_call(
        _unit_lower_inverse_kernel, grid=(n // per_step,),
        in_specs=[pl.BlockSpec(blk, lambda i: (i, 0, 0, 0))],
        out_specs=pl.BlockSpec(blk, lambda i: (i, 0, 0, 0)),
        out_shape=jax.ShapeDtypeStruct((n // TRI_PACK, TRI_PACK, c, c), jnp.float32),
        compiler_params=_params("parallel"))(a.reshape(n // TRI_PACK, TRI_PACK, c, c))
    return out.reshape(a.shape)
def chunk_gated_delta(q, k, v, g, beta, s0):
    B, T, H, DK = q.shape
    C = B_CHUNK
    n = -(-T // C)
    pad = n * C - T

    def chunks(a):
        a = jnp.pad(a, [(0, 0), (0, pad)] + [(0, 0)] * (a.ndim - 2))
        a = a.reshape(B, n, C, *a.shape[2:])
        return jnp.moveaxis(a, (1, 3), (0, 2))

    qc, kc, vc, bc = chunks(q) * DK ** -0.5, chunks(k), chunks(v), chunks(beta)
    gc = jnp.cumsum(chunks(g), axis=-1)
    incl = jnp.tril(jnp.ones((C, C), bool))
    strict = jnp.tril(jnp.ones((C, C), bool), -1)
    decay = jnp.exp(jnp.where(incl, gc[..., :, None] - gc[..., None, :], NEG))
    kb = kc * bc[..., None]
    a_mat = jnp.where(strict, jnp.einsum("...id,...jd->...ij", kb, kc) * decay, 0.0)
    t_mat = unit_lower_inverse(a_mat)
    u = t_mat @ (vc * bc[..., None])
    w = t_mat @ (kb * jnp.exp(gc)[..., None])
    attn = jnp.einsum("...id,...jd->...ij", qc, kc) * decay

    def step(s, inp):
        q_i, k_i, u_i, w_i, g_i, attn_i = inp
        v_new = u_i - w_i @ s
        o = (q_i * jnp.exp(g_i)[..., None]) @ s + attn_i @ v_new
        g_last = g_i[..., -1]
        s = s * jnp.exp(g_last)[..., None, None] + jnp.einsum(
            "bhcd,bhce->bhde", k_i * jnp.exp(g_last[..., None] - g_i)[..., None], v_new)
        return s, o

    s_final, o = lax.scan(step, s0, (qc, kc, u, w, gc, attn))
    o = jnp.moveaxis(o, (0, 2), (1, 3)).reshape(B, n * C, H, -1)[:, :T]
    return o, s_final


def gdn_mixer(zin, conv_w, a_log, dt_bias, norm_g, conv_state, rec_state):
    B, T, _ = zin.shape
    qkv, z, b_raw, a_raw = split_cols(zin, [B_CONV_CH, B_V_HEADS * B_DV, B_V_HEADS, B_V_HEADS])
    x_ext = jnp.concatenate([conv_state, qkv], axis=1)
    conv = sum(x_ext[:, i:i + T] * conv_w[i][None, None, :] for i in range(B_CONV))
    q, k, v = split_cols(jax.nn.silu(conv), [B_QK_HEADS * B_DK, B_QK_HEADS * B_DK, B_V_HEADS * B_DV])
    rep = B_V_HEADS // B_QK_HEADS
    q = jnp.repeat(l2_normalize(q.reshape(B, T, B_QK_HEADS, B_DK)), rep, axis=2)
    k = jnp.repeat(l2_normalize(k.reshape(B, T, B_QK_HEADS, B_DK)), rep, axis=2)
    v = v.reshape(B, T, B_V_HEADS, B_DV)
    beta = jax.nn.sigmoid(b_raw)
    g = -jnp.exp(a_log) * jax.nn.softplus(a_raw + dt_bias)
    o, s_new = chunk_gated_delta(q, k, v, g, beta, rec_state)
    o = rms_norm(o, norm_g) * jax.nn.silu(z.reshape(B, T, B_V_HEADS, B_DV))
    return o.reshape(B, T, B_V_HEADS * B_DV), x_ext[:, -(B_CONV - 1):], s_new


def nsa_compress(kv, n_cmp, pe, w1, w2, layer):
    B = kv.shape[0]
    n_row = n_cmp + 1
    x = kv[:, :n_row * CMP_STRIDE].reshape(B, n_row, CMP_STRIDE, 2, C_KV, HEAD_DIM)
    return nsa_compress_rows(x.transpose(3, 0, 4, 1, 2, 5).reshape(2, B, C_KV, n_row, CMP_STRIDE * HEAD_DIM),
                             n_cmp, pe, w1, w2, layer)


def nsa_compress_rows(x, n_cmp, pe, w1, w2, layer):
    _, B, _, n_row, half = x.shape
    x = x.reshape(2, B * C_KV * n_row, half)
    w1l = w1[layer]
    w_halves = jnp.concatenate([w1l[:, :half], w1l[:, half:]], axis=-1)
    pe_rows = jnp.broadcast_to(pe[layer].transpose(1, 0, 2).reshape(2, 1, CMP_BLK * HEAD_DIM), (2, 8, CMP_BLK * HEAD_DIM))
    outs = []
    for e in range(2):
        ab = matmul(x[e], w_halves, e).reshape(B, C_KV, n_row, 2, HEAD_DIM)
        c = matmul(pe_rows[e], w1l, e)[0]
        hid = jax.nn.gelu(ab[:, :, :n_cmp, 0] + ab[:, :, 1:n_cmp + 1, 1] + c)
        outs.append(matmul(hid.reshape(B * C_KV * n_cmp, HEAD_DIM), w2[layer], e).reshape(B, C_KV, n_cmp, HEAD_DIM))
    return outs


NSA_TQ = 128
NSA_TK = 512
SLC_SHIFT = SLC_BLK.bit_length() - 1


def _nsa_prompt_kernel(q_ref, gl_ref, kct_ref, vc_ref, kst_ref, vs_ref, kwt_ref, vw_ref, o_ref, key_ref, bias_ref, *,
                       n_cmp, n_slc):
    tq, tk = NSA_TQ, NSA_TK
    rows = C_REP * tq
    q_start = pl.program_id(1) * tq
    n_chunks = (q_start + tq + tk - 1) // tk
    win_lo = jnp.maximum(q_start - (WINDOW - 1), 0) // tk
    n_pad = kct_ref.shape[2]
    iota = lambda shape, ax: lax.broadcasted_iota(jnp.int32, shape, ax)
    tile = lambda a: jnp.concatenate([a] * C_REP, axis=0)

    n_c = iota((tq, n_pad), 1)
    vis = tile((n_c * CMP_STRIDE + CMP_BLK - 1 <= q_start + iota((tq, n_pad), 0)) & (n_c < n_cmp))
    n_o, m_o = iota((n_pad, LANES), 0) * CMP_STRIDE, iota((n_pad, LANES), 1) * SLC_BLK
    overlap = jnp.where((n_o < m_o + SLC_BLK) & (n_o + CMP_BLK > m_o) & (iota((n_pad, LANES), 1) < n_slc), 1.0, 0.0)
    cur = jnp.right_shift(q_start + iota((tq, LANES), 0), SLC_SHIFT)
    blk = iota((tq, LANES), 1)
    o_cmp = []
    for g in range(C_KV):
        qg = _stack_heads(q_ref, g, C_REP)
        s = jnp.dot(qg, kct_ref[0, g * HEAD_DIM:(g + 1) * HEAD_DIM, :], preferred_element_type=jnp.float32) * ATTN_SCALE
        s = jnp.where(vis, s, NEG)
        e = jnp.exp(s - jnp.max(s, axis=-1, keepdims=True))
        p = jnp.where(vis, e / jnp.sum(e, axis=-1, keepdims=True), 0.0)
        o_cmp.append(jnp.dot(_bf16(p), vc_ref[0, :, g * HEAD_DIM:(g + 1) * HEAD_DIM], preferred_element_type=jnp.float32))
        p_grp = sum(p[r * tq:(r + 1) * tq] for r in range(C_REP))
        imp = jnp.dot(p_grp, overlap, preferred_element_type=jnp.float32, precision=lax.Precision.HIGHEST)
        imp = jnp.where((blk == 0) | (blk == cur), FORCE, imp)
        key_ref[g * tq:(g + 1) * tq, :] = _sortable_key(jnp.where(blk <= cur, imp, NEG))
    k_sel = min(N_SEL, n_slc)
    thr = _kth_largest_key(key_ref, 1, LANES, k_sel, rows)
    _topk_bias(key_ref, bias_ref, thr, 1, LANES, k_sel, rows, tile(cur))

    t_k = q_start + iota((tq, tk), 0)
    s_k = iota((tq, tk), 1)
    blk_e, s_e = iota((LANES, tk), 0), iota((LANES, tk), 1)
    gate = jax.nn.sigmoid(gl_ref[0])

    def win_bias(off):
        d = t_k - (off + s_k)
        return tile(jnp.where((d >= 0) & (d < WINDOW), 0.0, NEG))

    for g in range(C_KV):
        qg = _stack_heads(q_ref, g, C_REP)
        sel = _bf16(jnp.where(bias_ref[g * tq:(g + 1) * tq, :] == 0.0, 1.0, 0.0))

        def slc_bias(off, sel=sel):
            expand = _bf16(jnp.where(blk_e == jnp.right_shift(off + s_e, SLC_SHIFT), 1.0, 0.0))
            hit = jnp.dot(sel, expand, preferred_element_type=jnp.float32)
            return tile(jnp.where((hit > 0.5) & (off + s_k <= t_k), 0.0, NEG))

        o_slc = _flash_group(qg, kst_ref, vs_ref, g, 0, n_chunks, tk, slc_bias)
        o_win = _flash_group(qg, kwt_ref, vw_ref, g, win_lo, n_chunks, tk, win_bias)
        for r in range(C_REP):
            h = g * C_REP + r
            rs = slice(r * tq, (r + 1) * tq)
            o = (gate[:, 3 * h:3 * h + 1] * o_cmp[g][rs] + gate[:, 3 * h + 1:3 * h + 2] * o_slc[rs]
                 + gate[:, 3 * h + 2:3 * h + 3] * o_win[rs])
            o_ref[0, :, h * HEAD_DIM:(h + 1) * HEAD_DIM] = o.astype(o_ref.dtype)


def nsa_prompt(z, pe, w1, w2, layer):
    B, T, _ = z.shape
    nq, nk = C_HEADS * HEAD_DIM, C_KV * HEAD_DIM
    q = _bf16(z[..., :nq])
    branch = [z[..., nq + 2 * i * nk:nq + 2 * (i + 1) * nk] for i in range(3)]
    gl = z[..., nq + 6 * nk:]
    as_kv = lambda a: a.reshape(B, T, 2, C_KV, HEAD_DIM)
    n_cmp = (T - CMP_BLK) // CMP_STRIDE + 1
    k_cmp, v_cmp = nsa_compress(as_kv(branch[0]), n_cmp, pe, w1, w2, layer)
    n_pad = -(-n_cmp // LANES) * LANES
    pad = ((0, 0), (0, 0), (0, n_pad - n_cmp), (0, 0))
    kct = _bf16(jnp.pad(k_cmp, pad)).transpose(0, 1, 3, 2).reshape(B, nk, n_pad)
    vc = _bf16(jnp.pad(v_cmp, pad)).transpose(0, 2, 1, 3).reshape(B, n_pad, nk)
    kt = lambda a: _bf16(a[..., :nk]).transpose(0, 2, 1)
    tq = NSA_TQ
    full_t = lambda shape: pl.BlockSpec(shape, lambda b, i: (b, 0, 0))
    o = pl.pallas_call(
        functools.partial(_nsa_prompt_kernel, n_cmp=n_cmp, n_slc=-(-T // SLC_BLK)),
        grid=(B, T // tq),
        in_specs=[
            pl.BlockSpec((1, tq, nq), lambda b, i: (b, i, 0)),
            pl.BlockSpec((1, tq, 3 * C_HEADS), lambda b, i: (b, i, 0)),
            full_t((1, nk, n_pad)), full_t((1, n_pad, nk)),
            full_t((1, nk, T)), full_t((1, T, nk)), full_t((1, nk, T)), full_t((1, T, nk)),
        ],
        out_specs=pl.BlockSpec((1, tq, nq), lambda b, i: (b, i, 0)),
        out_shape=jax.ShapeDtypeStruct((B, T, nq), jnp.bfloat16),
        scratch_shapes=[pltpu.VMEM((C_REP * tq, LANES), jnp.int32), pltpu.VMEM((C_REP * tq, LANES), jnp.float32)],
        compiler_params=_params("parallel", "arbitrary"))(
            q, gl, kct, vc, kt(branch[1]), _bf16(branch[1][..., nk:]), kt(branch[2]), _bf16(branch[2][..., nk:]))
    return o, as_kv(branch[0]), as_kv(branch[1]), as_kv(branch[2])[:, -WINDOW:]


def _page_copy_kernel(pt_ref, pool_ref, o_ref):
    o_ref[0] = pool_ref[...]


def gather_pages_pallas(pool, page_table, layer):
    B, n_pages = page_table.shape
    row = pool.shape[3:]
    return pl.pallas_call(
        _page_copy_kernel,
        grid_spec=pltpu.PrefetchScalarGridSpec(
            num_scalar_prefetch=1, grid=(B, n_pages),
            in_specs=[pl.BlockSpec((None, PAGE_SIZE, None) + row, lambda b, p, pt: (pt[b, p], 0, layer, 0, 0, 0))],
            out_specs=pl.BlockSpec((1, PAGE_SIZE) + row, lambda b, p, pt: (b, p, 0, 0, 0))),
        out_shape=jax.ShapeDtypeStruct((B, n_pages * PAGE_SIZE) + row, pool.dtype),
        compiler_params=_params("parallel", "arbitrary"))(page_table, pool)


def _page_rowgroup_kernel(pt_ref, pool_ref, o_ref):
    n_row = PAGE_SIZE // CMP_STRIDE
    for e in range(2):
        for g in range(C_KV):
            row = jnp.concatenate([pool_ref[pl.ds(j, n_row, stride=CMP_STRIDE), e, g, :] for j in range(CMP_STRIDE)], axis=1)
            o_ref[e, 0, g] = row.astype(o_ref.dtype)


def gather_page_rowgroups(pool, page_table, layer):
    B, n_pages = page_table.shape
    n_row = PAGE_SIZE // CMP_STRIDE
    blk = (2, 1, C_KV, n_row, CMP_STRIDE * HEAD_DIM)
    return pl.pallas_call(
        _page_rowgroup_kernel,
        grid_spec=pltpu.PrefetchScalarGridSpec(
            num_scalar_prefetch=1, grid=(B, n_pages),
            in_specs=[pl.BlockSpec((None, PAGE_SIZE, None, 2, C_KV, HEAD_DIM), lambda b, p, pt: (pt[b, p], 0, layer, 0, 0, 0))],
            out_specs=pl.BlockSpec(blk, lambda b, p, pt: (0, b, 0, p, 0))),
        out_shape=jax.ShapeDtypeStruct((2, B, C_KV, n_pages * n_row, CMP_STRIDE * HEAD_DIM), jnp.bfloat16),
        compiler_params=_params("parallel", "arbitrary"))(page_table, pool)


def _nsa_sample_cmpwin_kernel(q_ref, kct_ref, vc_ref, win_ref, ocmp_ref, owin_ref, sel_ref, key_ref, *,
                              n_cmp, n_slc, q_pos0, win_pos0, t_new):
    rows_g = C_REP * t_new
    nk = C_KV * HEAD_DIM
    n_pad, n_blk, n_win = kct_ref.shape[2], key_ref.shape[1], win_ref.shape[1]
    iota = lambda shape, ax: lax.broadcasted_iota(jnp.int32, shape, ax)
    tile = lambda a: jnp.concatenate([a] * C_REP, axis=0)
    n_c = iota((t_new, n_pad), 1)
    vis = tile((n_c * CMP_STRIDE + CMP_BLK - 1 <= q_pos0 + iota((t_new, n_pad), 0)) & (n_c < n_cmp))
    n_o, m_o = iota((n_pad, n_blk), 0) * CMP_STRIDE, iota((n_pad, n_blk), 1) * SLC_BLK
    overlap = jnp.where((n_o < m_o + SLC_BLK) & (n_o + CMP_BLK > m_o) & (iota((n_pad, n_blk), 1) < n_slc), 1.0, 0.0)
    cur = jnp.right_shift(q_pos0 + iota((t_new, n_blk), 0), SLC_SHIFT)
    blk = iota((t_new, n_blk), 1)
    d = q_pos0 + iota((t_new, n_win), 0) - (win_pos0 + iota((t_new, n_win), 1))
    win_ok = tile((win_pos0 + iota((t_new, n_win), 1) >= 0) & (d >= 0) & (d < WINDOW))
    for g in range(C_KV):
        rs = slice(g * rows_g, (g + 1) * rows_g)
        qg = q_ref[0, rs]
        s = jnp.dot(qg, kct_ref[0, g * HEAD_DIM:(g + 1) * HEAD_DIM, :], preferred_element_type=jnp.float32) * ATTN_SCALE
        s = jnp.where(vis, s, NEG)
        e = jnp.exp(s - jnp.max(s, axis=-1, keepdims=True))
        p = jnp.where(vis, e / jnp.sum(e, axis=-1, keepdims=True), 0.0)
        ocmp_ref[0, rs] = jnp.dot(_bf16(p), vc_ref[0, :, g * HEAD_DIM:(g + 1) * HEAD_DIM], preferred_element_type=jnp.float32)
        p_grp = sum(p[r * t_new:(r + 1) * t_new] for r in range(C_REP))
        imp = jnp.dot(p_grp, overlap, preferred_element_type=jnp.float32, precision=lax.Precision.HIGHEST)
        imp = jnp.where((blk == 0) | (blk == cur), FORCE, imp)
        key_ref[g * t_new:(g + 1) * t_new, :] = _sortable_key(jnp.where(blk <= cur, imp, NEG))
        kw = _bf16(win_ref[0, :, g * HEAD_DIM:(g + 1) * HEAD_DIM])
        vw = _bf16(win_ref[0, :, nk + g * HEAD_DIM:nk + (g + 1) * HEAD_DIM])
        sw = jnp.where(win_ok, _nt_dot(qg, kw) * ATTN_SCALE, NEG)
        ew = jnp.exp(sw - jnp.max(sw, axis=-1, keepdims=True))
        pw = ew / jnp.sum(ew, axis=-1, keepdims=True)
        owin_ref[0, rs] = jnp.dot(_bf16(pw), vw, preferred_element_type=jnp.float32)
    k_sel = min(N_SEL, n_slc)
    rows = C_KV * t_new
    thr = _kth_largest_key(key_ref, 1, n_blk, k_sel, rows)
    _topk_bias(key_ref, sel_ref.at[0], thr, 1, n_blk, k_sel, rows, jnp.concatenate([cur[:, :LANES]] * C_KV, axis=0))


def _paged_attend_step(q_ref, kv_of, bias_of_group, m_ref, l_ref, acc_ref, n_groups, rows_g):
    for g in range(n_groups):
        rs = slice(g * rows_g, (g + 1) * rows_g)
        s = _nt_dot(q_ref[0, rs], _bf16(kv_of(0, g))) * ATTN_SCALE + bias_of_group(g)
        m = m_ref[rs]
        m_new = jnp.maximum(m, jnp.max(s, axis=-1, keepdims=True))
        pr = jnp.exp(s - m_new)
        alpha = jnp.exp(m - m_new)
        l_ref[rs] = alpha * l_ref[rs] + jnp.sum(pr, axis=-1, keepdims=True)
        acc_ref[rs] = alpha * acc_ref[rs] + jnp.dot(_bf16(pr), _bf16(kv_of(1, g)), preferred_element_type=jnp.float32)
        m_ref[rs] = m_new


def _init_online_softmax(m_ref, l_ref, acc_ref):
    m_ref[...] = jnp.full(m_ref.shape, NEG, jnp.float32)
    l_ref[...] = jnp.zeros(l_ref.shape, jnp.float32)
    acc_ref[...] = jnp.zeros(acc_ref.shape, jnp.float32)


def _nsa_sample_slc_kernel(pt_ref, q_ref, sel_ref, gate_ref, ocmp_ref, owin_ref, *rest, n_pages, t_new):
    pool_refs, (new_ref, o_ref, m_ref, l_ref, acc_ref) = rest[:PAGES_PER_STEP], rest[PAGES_PER_STEP:]
    p = pl.program_id(1)
    n_blk = sel_ref.shape[2]

    @pl.when(p == 0)
    def _():
        _init_online_softmax(m_ref, l_ref, acc_ref)

    def attend(kv_of):
        iota = lambda shape, ax: lax.broadcasted_iota(jnp.int32, shape, ax)
        causal = p * STEP_KEYS + iota((t_new, STEP_KEYS), 1) <= n_pages * PAGE_SIZE + iota((t_new, STEP_KEYS), 0)
        key_blk = (STEP_KEYS // SLC_BLK) * p + jnp.right_shift(iota((n_blk, STEP_KEYS), 1), SLC_SHIFT)
        expand = _bf16(jnp.where(iota((n_blk, STEP_KEYS), 0) == key_blk, 1.0, 0.0))

        def bias_of_group(g):
            sel = _bf16(jnp.where(sel_ref[0, g * t_new:(g + 1) * t_new, :] == 0.0, 1.0, 0.0))
            hit = jnp.dot(sel, expand, preferred_element_type=jnp.float32)
            return jnp.concatenate([jnp.where((hit > 0.5) & causal, 0.0, NEG)] * C_REP, axis=0)

        _paged_attend_step(q_ref, kv_of, bias_of_group, m_ref, l_ref, acc_ref, C_KV, C_REP * t_new)

    @pl.when(p < n_pages // PAGES_PER_STEP)
    def _():
        attend(lambda e, g: jnp.concatenate([r[:, e, g, :] for r in pool_refs], axis=0))

    @pl.when(p == n_pages // PAGES_PER_STEP)
    def _():
        attend(lambda e, g: new_ref[0, :, e, g, :])
        gate = jax.nn.sigmoid(gate_ref[0])
        o = gate[:, 0:1] * ocmp_ref[0] + gate[:, 1:2] * (acc_ref[...] / l_ref[...]) + gate[:, 2:3] * owin_ref[0]
        o_ref[0] = o.astype(o_ref.dtype)


def nsa_sample(z, pe, w1, w2, pool_cmp, pool_slc, win_buf, page_table, layer):
    B, T, _ = z.shape
    n_pages = page_table.shape[1]
    past = n_pages * PAGE_SIZE
    nq, nk = C_HEADS * HEAD_DIM, C_KV * HEAD_DIM
    head_major = lambda a, d: a.reshape(B, T, -1, d).transpose(0, 2, 1, 3).reshape(B, -1, d)
    as_kv = lambda a: a.reshape(B, -1, 2, C_KV, HEAD_DIM)
    q = head_major(_bf16(z[..., :nq]), HEAD_DIM)
    branch = [z[..., nq + 2 * i * nk:nq + 2 * (i + 1) * nk] for i in range(3)]
    gl = head_major(z[..., nq + 6 * nk:], 3)
    n_cmp = (past + T - CMP_BLK) // CMP_STRIDE + 1
    if (n_cmp + 1) * CMP_STRIDE <= past:
        k_cmp, v_cmp = nsa_compress_rows(gather_page_rowgroups(pool_cmp, page_table, layer), n_cmp, pe, w1, w2, layer)
    else:
        cmp_rows = jnp.concatenate([gather_pages_pallas(pool_cmp, page_table, layer), as_kv(branch[0])], axis=1)
        k_cmp, v_cmp = nsa_compress(cmp_rows, n_cmp, pe, w1, w2, layer)
    n_pad = -(-n_cmp // LANES) * LANES
    pad = ((0, 0), (0, 0), (0, n_pad - n_cmp), (0, 0))
    kct = _bf16(jnp.pad(k_cmp, pad)).transpose(0, 1, 3, 2).reshape(B, nk, n_pad)
    vc = _bf16(jnp.pad(v_cmp, pad)).transpose(0, 2, 1, 3).reshape(B, n_pad, nk)
    n_slc = -(-(past + T) // SLC_BLK)
    n_blk = -(-n_slc // LANES) * LANES
    win_all = jnp.concatenate([win_buf.reshape(B, WINDOW, 2 * nk), branch[2]], axis=1)
    n_win = -(-(WINDOW + T) // LANES) * LANES
    win_pad = jnp.pad(win_all, ((0, 0), (0, n_win - WINDOW - T), (0, 0)))
    per_b = lambda shape: pl.BlockSpec(shape, lambda b: (b, 0, 0))
    hm_rows = C_HEADS * T
    o_cmp, o_win, sel = pl.pallas_call(
        functools.partial(_nsa_sample_cmpwin_kernel, n_cmp=n_cmp, n_slc=n_slc, q_pos0=past, win_pos0=past - WINDOW, t_new=T),
        grid=(B,),
        in_specs=[per_b((1, hm_rows, HEAD_DIM)), per_b((1, nk, n_pad)), per_b((1, n_pad, nk)), per_b((1, n_win, 2 * nk))],
        out_specs=[per_b((1, hm_rows, HEAD_DIM)), per_b((1, hm_rows, HEAD_DIM)), per_b((1, C_KV * T, n_blk))],
        out_shape=[jax.ShapeDtypeStruct((B, hm_rows, HEAD_DIM), jnp.float32)] * 2
        + [jax.ShapeDtypeStruct((B, C_KV * T, n_blk), jnp.float32)],
        scratch_shapes=[pltpu.VMEM((C_KV * T, n_blk), jnp.int32)],
        compiler_params=_params("parallel"))(q, kct, vc, win_pad)
    per_b3 = lambda shape: pl.BlockSpec(shape, lambda b, p, pt: (b, 0, 0))
    assert n_pages % PAGES_PER_STEP == 0 and T <= STEP_KEYS
    new_pad = jnp.pad(as_kv(branch[1]), ((0, 0), (0, STEP_KEYS - T), (0, 0), (0, 0), (0, 0)))
    kv_row = (2, C_KV, HEAD_DIM)
    page_of = lambda b, p, pt, i: pt[b, jnp.minimum(p * PAGES_PER_STEP + i, n_pages - 1)]
    o = pl.pallas_call(
        functools.partial(_nsa_sample_slc_kernel, n_pages=n_pages, t_new=T),
        grid_spec=pltpu.PrefetchScalarGridSpec(
            num_scalar_prefetch=1, grid=(B, n_pages // PAGES_PER_STEP + 1),
            in_specs=[per_b3((1, hm_rows, HEAD_DIM)), per_b3((1, C_KV * T, n_blk)), per_b3((1, hm_rows, 3)),
                      per_b3((1, hm_rows, HEAD_DIM)), per_b3((1, hm_rows, HEAD_DIM))]
            + [pl.BlockSpec((None, PAGE_SIZE, None) + kv_row, lambda b, p, pt, i=i: (page_of(b, p, pt, i), 0, layer, 0, 0, 0))
               for i in range(PAGES_PER_STEP)]
            + [pl.BlockSpec((1, STEP_KEYS) + kv_row, lambda b, p, pt: (b, 0, 0, 0, 0))],
            out_specs=per_b3((1, hm_rows, HEAD_DIM)),
            scratch_shapes=[pltpu.VMEM((hm_rows, 1), jnp.float32), pltpu.VMEM((hm_rows, 1), jnp.float32),
                            pltpu.VMEM((hm_rows, HEAD_DIM), jnp.float32)]),
        out_shape=jax.ShapeDtypeStruct((B, hm_rows, HEAD_DIM), jnp.bfloat16),
        compiler_params=_params("parallel", "arbitrary"))(
            page_table, q, sel, gl, o_cmp, o_win, *[pool_slc] * PAGES_PER_STEP, new_pad)
    o = o.reshape(B, C_HEADS, T, HEAD_DIM).transpose(0, 2, 1, 3).reshape(B, T, nq)
    return o, as_kv(branch[0]), as_kv(branch[1]), as_kv(win_all)[:, -WINDOW:]


def kernel(x_prompt, x_sample, cache_dsa_kv, cache_dsa_idx, state_gdn_conv, state_gdn_rec, cache_nsa_cmp, cache_nsa_slc, cache_nsa_win, page_table, c_prompt, c_sample, ada_w, ada_b, ln_g, ln_b, dsa_w_in, dsa_w_out, gdn_w_in, gdn_conv_w, gdn_a_log, gdn_dt_bias, gdn_norm_g, gdn_w_out, nsa_w_in, nsa_cmp_pe, nsa_cmp_w1, nsa_cmp_w2, nsa_w_out, moe_w_router, moe_b_router, moe_w_in, moe_b_in, moe_w_out, moe_b_out):
    xp, xs = x_prompt, x_sample
    bp, tp, _ = xp.shape
    bs, ts, _ = xs.shape
    n_p, n_s = bp * tp, bs * ts
    outs = {k: [] for k in ("dsa_kv_p", "dsa_kv_s", "dsa_idx_p", "dsa_idx_s", "gdn_conv_p", "gdn_conv_s", "gdn_rec_p",
                            "gdn_rec_s", "nsa_cmp_p", "nsa_cmp_s", "nsa_slc_p", "nsa_slc_s", "nsa_win_p", "nsa_win_s")}
    moe_w = (moe_w_router, moe_b_router, moe_w_in, moe_b_in, moe_w_out, moe_b_out)
    c_all = jax.nn.silu(jnp.concatenate([c_prompt, c_sample], axis=0))
    for i in range(DEPTH):
        kind, j = i % N_MIXERS, i // N_MIXERS
        mod = matmul(c_all, ada_w, i, ada_b).reshape(bp + bs, 6, D_MODEL)
        mp, ms = mod[:bp], mod[bp:]
        if kind == 0:
            w_in, w_out = dsa_w_in, dsa_w_out
        elif kind == 1:
            w_in, w_out = gdn_w_in, gdn_w_out
        else:
            w_in, w_out = nsa_w_in, nsa_w_out
        zp = matmul_modulated(xp, mp[:, 1], mp[:, 0], w_in, j)
        hs = xs * (1.0 + ms[:, 1][:, None, :]) + ms[:, 0][:, None, :]
        zs = matmul(hs.reshape(n_s, D_MODEL), w_in, j).reshape(bs, ts, -1)
        if kind == 0:
            op, kv_p, ki_p = dsa_prompt(zp)
            os_, kv_s, ki_s = dsa_sample(zs, cache_dsa_kv, cache_dsa_idx, page_table, j)
            outs["dsa_kv_p"].append(kv_p)
            outs["dsa_kv_s"].append(kv_s)
            outs["dsa_idx_p"].append(ki_p)
            outs["dsa_idx_s"].append(ki_s)
        elif kind == 1:
            gdn_args = (gdn_conv_w[j], gdn_a_log[j], gdn_dt_bias[j], gdn_norm_g[j])
            zero_conv = jnp.zeros((bp, B_CONV - 1, B_CONV_CH), xp.dtype)
            zero_rec = jnp.zeros((bp, B_V_HEADS, B_DK, B_DV), xp.dtype)
            op, conv_p, rec_p = gdn_mixer(zp, *gdn_args, zero_conv, zero_rec)
            os_, conv_s, rec_s = gdn_mixer(zs, *gdn_args, state_gdn_conv[:, j], state_gdn_rec[:, j])
            outs["gdn_conv_p"].append(conv_p)
            outs["gdn_conv_s"].append(conv_s)
            outs["gdn_rec_p"].append(rec_p)
            outs["gdn_rec_s"].append(rec_s)
        else:
            nsa_args = (nsa_cmp_pe, nsa_cmp_w1, nsa_cmp_w2)
            op, cmp_p, slc_p, win_p = nsa_prompt(zp, *nsa_args, j)
            os_, cmp_s, slc_s, win_s = nsa_sample(zs, *nsa_args, cache_nsa_cmp, cache_nsa_slc, cache_nsa_win[:, j],
                                                  page_table, j)
            outs["nsa_cmp_p"].append(cmp_p)
            outs["nsa_cmp_s"].append(cmp_s)
            outs["nsa_slc_p"].append(slc_p)
            outs["nsa_slc_s"].append(slc_s)
            outs["nsa_win_p"].append(win_p)
            outs["nsa_win_s"].append(win_s)
        yp = matmul(op.reshape(n_p, -1), w_out, j).reshape(bp, tp, D_MODEL)
        ys = matmul(os_.reshape(n_s, -1), w_out, j).reshape(bs, ts, D_MODEL)
        xp, hp = post_norm_modulate(xp, yp, mp[:, 2], ln_g[i, 0], ln_b[i, 0], mp[:, 4], mp[:, 3])
        xs, hs = post_norm_modulate(xs, ys, ms[:, 2], ln_g[i, 0], ln_b[i, 0], ms[:, 4], ms[:, 3])
        h_all = jnp.concatenate([hp.reshape(n_p, D_MODEL), hs.reshape(n_s, D_MODEL)], axis=0)
        y_k, w_k = moe(h_all, i, *moe_w)
        xp = post_norm_moe(xp, y_k, w_k, 0, mp[:, 5], ln_g[i, 1], ln_b[i, 1])
        xs = post_norm_moe(xs, y_k, w_k, n_p, ms[:, 5], ln_g[i, 1], ln_b[i, 1])
    st = lambda k, ax: jnp.stack(outs[k], axis=ax)
    return (xp, xs,
            st("dsa_kv_p", 2), st("dsa_kv_s", 2), st("dsa_idx_p", 2), st("dsa_idx_s", 2),
            st("gdn_conv_p", 1), st("gdn_conv_s", 1), st("gdn_rec_p", 1), st("gdn_rec_s", 1),
            st("nsa_cmp_p", 2), st("nsa_cmp_s", 2), st("nsa_slc_p", 2), st("nsa_slc_s", 2),
            st("nsa_win_p", 1), st("nsa_win_s", 1))
```

```python
import functools
import math

import jax
import jax.numpy as jnp
import numpy as np
from jax import lax
from jax.experimental import pallas as pl
from jax.experimental.pallas import tpu as pltpu

D_MODEL = 2048
DEPTH = 4
PAGE_SIZE = 128
N_MIXERS = 3
ALPHA = (2.0 * DEPTH) ** 0.25
LN_EPS = 1e-5
NORM_EPS = 1e-6
NEG = -1e30
FORCE = 1e9
HEAD_DIM = 128
ATTN_SCALE = HEAD_DIM ** -0.5

A_HEADS = D_MODEL // HEAD_DIM
A_KV = 4
A_REP = A_HEADS // A_KV
A_IDX_HEADS = 16
A_IDX_DIM = 64
A_TOPK = 256

B_QK_HEADS = 16
B_V_HEADS = 32
B_DK = 128
B_DV = 128
B_CONV = 4
B_CHUNK = 64
B_CONV_CH = 2 * B_QK_HEADS * B_DK + B_V_HEADS * B_DV

C_HEADS = D_MODEL // HEAD_DIM
C_KV = 4
C_REP = C_HEADS // C_KV
CMP_BLK = 32
CMP_STRIDE = 16
SLC_BLK = 64
N_SEL = 16
WINDOW = 512

N_EXPERTS = 32
TOP_K = 4
D_FF = 2048
SWIGLU_LIMIT = 7.0
SWIGLU_ALPHA = 1.702

VMEM_LIMIT_BYTES = 56 * 1024 * 1024
MOE_ROW_BLK = 256
MOE_FF_TILE = 1024
MOE_OUT_TILE = 2048


def _params(*sem):
    return pltpu.CompilerParams(dimension_semantics=sem, vmem_limit_bytes=VMEM_LIMIT_BYTES)


def _bf16(x):
    return x.astype(jnp.bfloat16)


def _mm_kernel(x_ref, w_ref, o_ref):
    o_ref[...] = jnp.dot(_bf16(x_ref[...]), _bf16(w_ref[...]), preferred_element_type=jnp.float32)


def _mm_bias_kernel(x_ref, w_ref, b_ref, o_ref):
    o_ref[...] = jnp.dot(_bf16(x_ref[...]), _bf16(w_ref[...]), preferred_element_type=jnp.float32) + b_ref[...]


def _mm_mod_kernel(x_ref, sc_ref, sh_ref, w_ref, o_ref):
    h = x_ref[0] * (1.0 + sc_ref[0]) + sh_ref[0]
    o_ref[0] = jnp.dot(_bf16(h), _bf16(w_ref[...]), preferred_element_type=jnp.float32)


def matmul(x, w, layer, b=None, tm=512, tn=512):
    m, k = x.shape
    n = w.shape[2]
    tm = min(tm, m)
    tn = min(tn, n)
    grid = (pl.cdiv(m, tm), pl.cdiv(n, tn))
    in_specs = [pl.BlockSpec((tm, k), lambda i, j: (i, 0)), pl.BlockSpec((None, k, tn), lambda i, j: (layer, 0, j))]
    args = [x, w]
    body = _mm_kernel
    if b is not None:
        in_specs.append(pl.BlockSpec((None, 1, tn), lambda i, j: (layer, 0, j)))
        args.append(b.reshape(b.shape[0], 1, n))
        body = _mm_bias_kernel
    return pl.pallas_call(
        body, grid=grid, in_specs=in_specs,
        out_specs=pl.BlockSpec((tm, tn), lambda i, j: (i, j)),
        out_shape=jax.ShapeDtypeStruct((m, n), jnp.float32),
        compiler_params=_params("parallel", "parallel"))(*args)


def matmul_modulated(x, scale, shift, w, layer, tm=1024, tn=512):
    bsz, t, k = x.shape
    n = w.shape[2]
    tm = min(tm, t)
    tn = min(tn, n)
    grid = (bsz, pl.cdiv(t, tm), pl.cdiv(n, tn))
    mod_spec = pl.BlockSpec((1, 1, k), lambda b, i, j: (b, 0, 0))
    return pl.pallas_call(
        _mm_mod_kernel, grid=grid,
        in_specs=[pl.BlockSpec((1, tm, k), lambda b, i, j: (b, i, 0)), mod_spec, mod_spec,
                  pl.BlockSpec((None, k, tn), lambda b, i, j: (layer, 0, j))],
        out_specs=pl.BlockSpec((1, tm, tn), lambda b, i, j: (b, i, j)),
        out_shape=jax.ShapeDtypeStruct((bsz, t, n), jnp.float32),
        compiler_params=_params("parallel", "parallel", "parallel"))(
            x, scale.reshape(bsz, 1, k), shift.reshape(bsz, 1, k), w)


def _layer_norm(z, g, b):
    mu = jnp.mean(z, axis=-1, keepdims=True)
    zc = z - mu
    var = jnp.mean(zc * zc, axis=-1, keepdims=True)
    return zc * lax.rsqrt(var + LN_EPS) * g + b


def _postnorm_kernel(x_ref, y_ref, gate_ref, g_ref, b_ref, sc_ref, sh_ref, xo_ref, ho_ref):
    xn = _layer_norm(ALPHA * x_ref[0] + gate_ref[0] * y_ref[0], g_ref[...], b_ref[...])
    xo_ref[0] = xn
    ho_ref[0] = (xn * (1.0 + sc_ref[0]) + sh_ref[0]).astype(ho_ref.dtype)


def post_norm_modulate(x, y, gate, g, b, scale, shift, tm=256):
    bsz, t, d = x.shape
    tm = min(tm, t)
    row = pl.BlockSpec((1, tm, d), lambda bi, i: (bi, i, 0))
    per_seq = pl.BlockSpec((1, 1, d), lambda bi, i: (bi, 0, 0))
    shared = pl.BlockSpec((1, d), lambda bi, i: (0, 0))
    return pl.pallas_call(
        _postnorm_kernel, grid=(bsz, t // tm),
        in_specs=[row, row, per_seq, shared, shared, per_seq, per_seq],
        out_specs=[row, row],
        out_shape=[jax.ShapeDtypeStruct(x.shape, jnp.float32), jax.ShapeDtypeStruct(x.shape, jnp.bfloat16)],
        compiler_params=_params("parallel", "parallel"))(
            x, y, gate.reshape(bsz, 1, d), g.reshape(1, d), b.reshape(1, d),
            scale.reshape(bsz, 1, d), shift.reshape(bsz, 1, d))


def _postnorm_moe_kernel(x_ref, y0_ref, y1_ref, y2_ref, y3_ref, w_ref, gate_ref, g_ref, b_ref, xo_ref):
    w = w_ref[...]
    y = sum(w[:, k:k + 1] * r[...].astype(jnp.float32) for k, r in enumerate((y0_ref, y1_ref, y2_ref, y3_ref)))
    xo_ref[0] = _layer_norm(ALPHA * x_ref[0] + gate_ref[0] * y, g_ref[...], b_ref[...])


def post_norm_moe(x, ys, w, row0, gate, g, b, tm=256):
    bsz, t, d = x.shape
    tm = min(tm, t)
    per_seq_blocks = t // tm
    assert row0 % tm == 0
    row = pl.BlockSpec((1, tm, d), lambda bi, i: (bi, i, 0))
    tok = lambda width: pl.BlockSpec((tm, width), lambda bi, i: (row0 // tm + bi * per_seq_blocks + i, 0))
    per_seq = pl.BlockSpec((1, 1, d), lambda bi, i: (bi, 0, 0))
    shared = pl.BlockSpec((1, d), lambda bi, i: (0, 0))
    return pl.pallas_call(
        _postnorm_moe_kernel, grid=(bsz, per_seq_blocks),
        in_specs=[row] + [tok(d)] * TOP_K + [tok(TOP_K), per_seq, shared, shared],
        out_specs=row,
        out_shape=jax.ShapeDtypeStruct(x.shape, jnp.float32),
        compiler_params=_params("parallel", "parallel"))(x, *ys, w, gate.reshape(bsz, 1, d), g.reshape(1, d), b.reshape(1, d))


def _moe_in_kernel(blk_e_ref, n_used_ref, x_ref, wg_ref, wl_ref, bg_ref, bl_ref, o_ref):
    used = pl.program_id(1) < n_used_ref[0]

    @pl.when(used)
    def _():
        x = x_ref[...]
        glu = jnp.dot(x, _bf16(wg_ref[...]), preferred_element_type=jnp.float32) + bg_ref[...]
        lin = jnp.dot(x, _bf16(wl_ref[...]), preferred_element_type=jnp.float32) + bl_ref[...]
        glu = jnp.minimum(glu, SWIGLU_LIMIT)
        lin = jnp.clip(lin, -SWIGLU_LIMIT, SWIGLU_LIMIT)
        o_ref[...] = (glu * jax.nn.sigmoid(SWIGLU_ALPHA * glu) * (lin + 1.0)).astype(o_ref.dtype)

    @pl.when(jnp.logical_not(used))
    def _():
        o_ref[...] = jnp.zeros(o_ref.shape, o_ref.dtype)


def _moe_out_kernel(blk_e_ref, n_used_ref, h_ref, w_ref, b_ref, o_ref):
    used = pl.program_id(1) < n_used_ref[0]

    @pl.when(used)
    def _():
        y = jnp.dot(h_ref[...], _bf16(w_ref[...]), preferred_element_type=jnp.float32) + b_ref[...]
        o_ref[...] = y.astype(o_ref.dtype)

    @pl.when(jnp.logical_not(used))
    def _():
        o_ref[...] = jnp.zeros(o_ref.shape, o_ref.dtype)


def moe_ffn(xs, blk_expert, n_used, layer, w_in, b_in, w_out, b_out):
    r = xs.shape[0]
    n_blk = r // MOE_ROW_BLK
    n_ff = D_FF // MOE_FF_TILE
    tm, tf, tn = MOE_ROW_BLK, MOE_FF_TILE, MOE_OUT_TILE
    b_in4 = b_in.reshape(DEPTH, N_EXPERTS, 1, 2 * D_FF)
    b_out4 = b_out.reshape(DEPTH, N_EXPERTS, 1, D_MODEL)
    row_blk = lambda j, m, be, nu: (jnp.minimum(m, nu[0] - 1), 0)
    act = pl.pallas_call(
        _moe_in_kernel,
        grid_spec=pltpu.PrefetchScalarGridSpec(
            num_scalar_prefetch=2, grid=(n_ff, n_blk),
            in_specs=[
                pl.BlockSpec((tm, D_MODEL), row_blk),
                pl.BlockSpec((None, None, D_MODEL, tf), lambda j, m, be, nu: (layer, be[m], 0, j)),
                pl.BlockSpec((None, None, D_MODEL, tf), lambda j, m, be, nu: (layer, be[m], 0, n_ff + j)),
                pl.BlockSpec((None, None, 1, tf), lambda j, m, be, nu: (layer, be[m], 0, j)),
                pl.BlockSpec((None, None, 1, tf), lambda j, m, be, nu: (layer, be[m], 0, n_ff + j)),
            ],
            out_specs=pl.BlockSpec((tm, tf), lambda j, m, be, nu: (m, j))),
        out_shape=jax.ShapeDtypeStruct((r, D_FF), jnp.bfloat16),
        compiler_params=_params("arbitrary", "arbitrary"))(blk_expert, n_used, xs, w_in, w_in, b_in4, b_in4)
    return pl.pallas_call(
        _moe_out_kernel,
        grid_spec=pltpu.PrefetchScalarGridSpec(
            num_scalar_prefetch=2, grid=(D_MODEL // tn, n_blk),
            in_specs=[
                pl.BlockSpec((tm, D_FF), row_blk),
                pl.BlockSpec((None, None, D_FF, tn), lambda j, m, be, nu: (layer, be[m], 0, j)),
                pl.BlockSpec((None, None, 1, tn), lambda j, m, be, nu: (layer, be[m], 0, j)),
            ],
            out_specs=pl.BlockSpec((tm, tn), lambda j, m, be, nu: (m, j))),
        out_shape=jax.ShapeDtypeStruct((r, D_MODEL), jnp.bfloat16),
        compiler_params=_params("arbitrary", "arbitrary"))(blk_expert, n_used, act, w_out, b_out4)


def moe(x, layer, w_router, b_router, w_in, b_in, w_out, b_out):
    n_tok = x.shape[0]
    n_asg = n_tok * TOP_K
    logits = jnp.dot(x.astype(jnp.float32), w_router[layer], precision=lax.Precision.HIGHEST) + b_router[layer]
    top_val, top_idx = lax.top_k(logits, TOP_K)
    gate = jax.nn.softmax(top_val, axis=-1)
    assert n_asg % MOE_ROW_BLK == 0
    onehot = (top_idx.reshape(-1, MOE_ROW_BLK, 1) == jnp.arange(N_EXPERTS)).astype(jnp.float32)
    tri = jnp.tril(jnp.ones((MOE_ROW_BLK, MOE_ROW_BLK), jnp.float32), -1)
    in_blk = jnp.einsum("ij,bje->bie", tri, onehot)
    blk_tot = jnp.sum(onehot, axis=1)
    before = jnp.cumsum(blk_tot, axis=0) - blk_tot
    rank = jnp.sum(onehot * (in_blk + before[:, None, :]), axis=-1).reshape(-1).astype(jnp.int32)
    counts = jnp.sum(blk_tot, axis=0).astype(jnp.int32)
    blocks = (counts + MOE_ROW_BLK - 1) // MOE_ROW_BLK
    blk_end = jnp.cumsum(blocks)
    dest = (blk_end - blocks)[top_idx.reshape(-1)] * MOE_ROW_BLK + rank
    n_blk = n_asg // MOE_ROW_BLK + N_EXPERTS
    row_tok = jnp.zeros((n_blk * MOE_ROW_BLK,), jnp.int32).at[dest].set(jnp.arange(n_asg, dtype=jnp.int32) // TOP_K)
    n_used = blk_end[-1:].astype(jnp.int32)
    blk_of = jnp.minimum(jnp.arange(n_blk, dtype=jnp.int32), n_used - 1)
    blk_expert = jnp.sum((blk_end[None, :] <= blk_of[:, None]).astype(jnp.int32), axis=1)
    xs = x[row_tok]
    ys = moe_ffn(xs, blk_expert, n_used, layer, w_in, b_in, w_out, b_out)
    dest = dest.reshape(n_tok, TOP_K)
    return [ys[dest[:, k]] for k in range(TOP_K)], gate


def split_cols(z, sizes):
    return jnp.split(z, np.cumsum(sizes)[:-1].tolist(), axis=-1)


def rms_norm(x, g):
    return x * lax.rsqrt(jnp.mean(jnp.square(x), axis=-1, keepdims=True) + NORM_EPS) * g


def l2_normalize(x):
    return x * lax.rsqrt(jnp.sum(jnp.square(x), axis=-1, keepdims=True) + NORM_EPS)


INT32_MIN = -2 ** 31
LANES = 128
PAGES_PER_STEP = 4
STEP_KEYS = PAGES_PER_STEP * PAGE_SIZE


def _sortable_key(x):
    bits = pltpu.bitcast(x, jnp.int32)
    return jnp.where(bits < 0, bits ^ 0x7FFFFFFF, bits)


def _kth_largest_key(key_ref, n_chunks, chunk, k, rows):
    def count_ge(cand):
        def body(c, acc):
            off = pl.multiple_of(c * chunk, chunk)
            for j in range(chunk // LANES):
                keys = key_ref[:, pl.ds(off + j * LANES, LANES)]
                acc = acc + jnp.where(keys >= cand, 1.0, 0.0)
            return acc

        acc = lax.fori_loop(0, n_chunks, body, jnp.zeros((rows, LANES), jnp.float32))
        return jnp.sum(acc, axis=-1, keepdims=True)

    def bit_step(b, ans):
        cand = ans + lax.shift_left(jnp.int32(1), 31 - b)
        return jnp.where(count_ge(cand) >= k, cand, ans)

    return lax.fori_loop(0, 32, bit_step, jnp.full((rows, 1), INT32_MIN, jnp.int32))


def _topk_bias(key_ref, bias_ref, thr, n_chunks, chunk, k, rows, row_pos):
    tri = (lax.broadcasted_iota(jnp.int32, (LANES, LANES), 0)
           <= lax.broadcasted_iota(jnp.int32, (LANES, LANES), 1)).astype(jnp.bfloat16)
    col = lax.broadcasted_iota(jnp.int32, (rows, LANES), 1)

    def count_gt(c, acc):
        off = pl.multiple_of(c * chunk, chunk)
        for j in range(chunk // LANES):
            acc = acc + jnp.where(key_ref[:, pl.ds(off + j * LANES, LANES)] > thr, 1.0, 0.0)
        return acc

    n_gt = jnp.sum(lax.fori_loop(0, n_chunks, count_gt, jnp.zeros((rows, LANES), jnp.float32)), axis=-1, keepdims=True)
    need = k - n_gt

    def body(c, run):
        off = pl.multiple_of(c * chunk, chunk)
        for j in range(chunk // LANES):
            keys = key_ref[:, pl.ds(off + j * LANES, LANES)]
            eq = keys == thr
            eq_f = jnp.where(eq, 1.0, 0.0)
            incl = jnp.dot(eq_f.astype(jnp.bfloat16), tri, preferred_element_type=jnp.float32)
            sel = (keys > thr) | (eq & (run + incl - eq_f < need))
            ok = sel & (off + j * LANES + col <= row_pos)
            bias_ref[:, pl.ds(off + j * LANES, LANES)] = jnp.where(ok, 0.0, NEG)
            run = run + incl[:, LANES - 1:LANES]
        return run

    lax.fori_loop(0, n_chunks, body, jnp.zeros((rows, 1), jnp.float32))


def _stack_heads(q_ref, g, rep):
    return jnp.concatenate([q_ref[0, :, (g * rep + r) * HEAD_DIM:(g * rep + r + 1) * HEAD_DIM] for r in range(rep)], axis=0)


def _flash_group(qg, kt_ref, v_ref, g, lo, hi, chunk, bias_fn):
    rows = qg.shape[0]

    def body(c, carry):
        m, l, acc = carry
        off = pl.multiple_of(c * chunk, chunk)
        kt = kt_ref[0, g * HEAD_DIM:(g + 1) * HEAD_DIM, pl.ds(off, chunk)]
        s = jnp.dot(qg, kt, preferred_element_type=jnp.float32) * ATTN_SCALE + bias_fn(off)
        m_new = jnp.maximum(m, jnp.max(s, axis=-1, keepdims=True))
        p = jnp.exp(s - m_new)
        alpha = jnp.exp(m - m_new)
        l = alpha * l + jnp.sum(p, axis=-1, keepdims=True)
        v = v_ref[0, pl.ds(off, chunk), g * HEAD_DIM:(g + 1) * HEAD_DIM]
        acc = alpha * acc + jnp.dot(_bf16(p), v, preferred_element_type=jnp.float32)
        return m_new, l, acc

    init = (jnp.full((rows, 1), NEG, jnp.float32), jnp.zeros((rows, 1), jnp.float32),
            jnp.zeros((rows, HEAD_DIM), jnp.float32))
    _, l, acc = lax.fori_loop(lo, hi, body, init)
    return acc / l


DSA_TQ = 128
DSA_TK = 512


def _dsa_prompt_kernel(q_ref, qi_ref, wi_ref, kit_ref, kt_ref, v_ref, o_ref, key_ref, bias_ref, *, topk):
    tq, tk = DSA_TQ, DSA_TK
    q_start = pl.program_id(1) * tq
    n_chunks = (q_start + tq + tk - 1) // tk
    row_pos = q_start + lax.broadcasted_iota(jnp.int32, (tq, tk), 0)
    col = lax.broadcasted_iota(jnp.int32, (tq, tk), 1)
    wi = wi_ref[0]

    def score_chunk(c, carry):
        off = pl.multiple_of(c * tk, tk)
        kit = kit_ref[0, :, pl.ds(off, tk)]
        acc = jnp.zeros((tq, tk), jnp.float32)
        for h in range(A_IDX_HEADS):
            s = jnp.dot(qi_ref[0, h], kit, preferred_element_type=jnp.float32)
            acc = acc + jnp.maximum(s, 0.0) * wi[:, h:h + 1]
        key_ref[:, pl.ds(off, tk)] = _sortable_key(jnp.where(off + col <= row_pos, acc, NEG))
        return carry

    lax.fori_loop(0, n_chunks, score_chunk, 0)
    thr = _kth_largest_key(key_ref, n_chunks, tk, topk, tq)
    _topk_bias(key_ref, bias_ref, thr, n_chunks, tk, topk, tq, row_pos[:, :LANES])
    bias_fn = lambda off: jnp.concatenate([bias_ref[:, pl.ds(off, tk)]] * A_REP, axis=0)
    for g in range(A_KV):
        out = _flash_group(_stack_heads(q_ref, g, A_REP), kt_ref, v_ref, g, 0, n_chunks, tk, bias_fn)
        for r in range(A_REP):
            h = g * A_REP + r
            o_ref[0, :, h * HEAD_DIM:(h + 1) * HEAD_DIM] = out[r * tq:(r + 1) * tq].astype(o_ref.dtype)


def dsa_prompt(z):
    B, T, _ = z.shape
    topk = min(A_TOPK, T // 4)
    nq, nk = A_HEADS * HEAD_DIM, A_KV * HEAD_DIM
    q = _bf16(z[..., :nq])
    k, v = z[..., nq:nq + nk], z[..., nq + nk:nq + 2 * nk]
    off = nq + 2 * nk
    qi = _bf16(z[..., off:off + A_IDX_HEADS * A_IDX_DIM]).reshape(B, T, A_IDX_HEADS, A_IDX_DIM).transpose(0, 2, 1, 3)
    off += A_IDX_HEADS * A_IDX_DIM
    ki = z[..., off:off + A_IDX_DIM]
    wi = z[..., off + A_IDX_DIM:] * (A_IDX_HEADS ** -0.5 * A_IDX_DIM ** -0.5)
    kit = _bf16(ki).transpose(0, 2, 1)
    kt = _bf16(k).transpose(0, 2, 1)
    tq = DSA_TQ
    o = pl.pallas_call(
        functools.partial(_dsa_prompt_kernel, topk=topk),
        grid=(B, T // tq),
        in_specs=[
            pl.BlockSpec((1, tq, nq), lambda b, i: (b, i, 0)),
            pl.BlockSpec((1, A_IDX_HEADS, tq, A_IDX_DIM), lambda b, i: (b, 0, i, 0)),
            pl.BlockSpec((1, tq, A_IDX_HEADS), lambda b, i: (b, i, 0)),
            pl.BlockSpec((1, A_IDX_DIM, T), lambda b, i: (b, 0, 0)),
            pl.BlockSpec((1, nk, T), lambda b, i: (b, 0, 0)),
            pl.BlockSpec((1, T, nk), lambda b, i: (b, 0, 0)),
        ],
        out_specs=pl.BlockSpec((1, tq, nq), lambda b, i: (b, i, 0)),
        out_shape=jax.ShapeDtypeStruct((B, T, nq), jnp.bfloat16),
        scratch_shapes=[pltpu.VMEM((tq, T), jnp.int32), pltpu.VMEM((tq, T), jnp.float32)],
        compiler_params=_params("parallel", "arbitrary"))(q, qi, wi, kit, kt, _bf16(v))
    kv = jnp.stack([k.reshape(B, T, A_KV, HEAD_DIM), v.reshape(B, T, A_KV, HEAD_DIM)], axis=2)
    return o, kv, ki


def _nt_dot(a, b):
    return lax.dot_general(a, b, (((1,), (1,)), ((), ())), preferred_element_type=jnp.float32)


def _dsa_sample_score_kernel(pt_ref, qi_ref, wi_ref, *rest, layer, n_pages, t_new):
    pool_refs, (new_ref, o_ref) = rest[:PAGES_PER_STEP], rest[PAGES_PER_STEP:]
    p = pl.program_id(1)

    def scores(ki):
        s = jnp.maximum(_nt_dot(qi_ref[0], _bf16(ki)), 0.0) * wi_ref[0]
        acc = s[0:t_new]
        for h in range(1, A_IDX_HEADS):
            acc = acc + s[h * t_new:(h + 1) * t_new]
        k_pos = p * STEP_KEYS + lax.broadcasted_iota(jnp.int32, (t_new, STEP_KEYS), 1)
        q_pos = n_pages * PAGE_SIZE + lax.broadcasted_iota(jnp.int32, (t_new, STEP_KEYS), 0)
        o_ref[0] = jnp.where(k_pos <= q_pos, acc, NEG)

    @pl.when(p < n_pages // PAGES_PER_STEP)
    def _():
        scores(jnp.concatenate([r[:, layer, :] for r in pool_refs], axis=0))

    @pl.when(p == n_pages // PAGES_PER_STEP)
    def _():
        scores(new_ref[0])


def _dsa_sample_select_kernel(s_ref, bias_ref, key_ref, *, topk, chunk, q_pos0):
    rows, n_keys = key_ref.shape
    n_chunks = n_keys // chunk
    for c in range(n_chunks):
        key_ref[:, c * chunk:(c + 1) * chunk] = _sortable_key(s_ref[0, :, c * chunk:(c + 1) * chunk])
    thr = _kth_largest_key(key_ref, n_chunks, chunk, topk, rows)
    q_pos = q_pos0 + lax.broadcasted_iota(jnp.int32, (rows, LANES), 0)
    _topk_bias(key_ref, bias_ref.at[0], thr, n_chunks, chunk, topk, rows, q_pos)


def _dsa_sample_attend_kernel(pt_ref, q_ref, bias_ref, *rest, n_pages, t_new):
    pool_refs, (new_ref, o_ref, m_ref, l_ref, acc_ref) = rest[:PAGES_PER_STEP], rest[PAGES_PER_STEP:]
    p = pl.program_id(1)

    @pl.when(p == 0)
    def _():
        _init_online_softmax(m_ref, l_ref, acc_ref)

    def attend(kv_of):
        bias = jnp.concatenate([bias_ref[0]] * A_REP, axis=0)
        _paged_attend_step(q_ref, kv_of, lambda g: bias, m_ref, l_ref, acc_ref, A_KV, A_REP * t_new)

    @pl.when(p < n_pages // PAGES_PER_STEP)
    def _():
        attend(lambda e, g: jnp.concatenate([r[:, e, g, :] for r in pool_refs], axis=0))

    @pl.when(p == n_pages // PAGES_PER_STEP)
    def _():
        attend(lambda e, g: new_ref[0, :, e, g, :])
        o_ref[0] = (acc_ref[...] / l_ref[...]).astype(o_ref.dtype)


def dsa_sample(z, pool_kv, pool_idx, page_table, layer):
    B, T, _ = z.shape
    n_pages = page_table.shape[1]
    past = n_pages * PAGE_SIZE
    topk = min(A_TOPK, (past + T) // 4)
    nq, nk = A_HEADS * HEAD_DIM, A_KV * HEAD_DIM
    n_layers = pool_kv.shape[2]
    k, v = z[..., nq:nq + nk], z[..., nq + nk:nq + 2 * nk]
    off = nq + 2 * nk
    head_major = lambda a, d: a.reshape(B, T, -1, d).transpose(0, 2, 1, 3).reshape(B, -1, d)
    q = head_major(_bf16(z[..., :nq]), HEAD_DIM)
    qi = head_major(_bf16(z[..., off:off + A_IDX_HEADS * A_IDX_DIM]), A_IDX_DIM)
    off += A_IDX_HEADS * A_IDX_DIM
    ki = z[..., off:off + A_IDX_DIM]
    wi = head_major(z[..., off + A_IDX_DIM:] * (A_IDX_HEADS ** -0.5 * A_IDX_DIM ** -0.5), 1)
    assert n_pages % PAGES_PER_STEP == 0 and T <= STEP_KEYS
    n_steps = n_pages // PAGES_PER_STEP + 1
    n_keys = n_steps * STEP_KEYS
    pad_rows = lambda a: jnp.pad(a, ((0, 0), (0, STEP_KEYS - T)) + ((0, 0),) * (a.ndim - 2))
    page_of = lambda b, p, pt, i: pt[b, jnp.minimum(p * PAGES_PER_STEP + i, n_pages - 1)]
    per_b = lambda b, p, pt: (b, 0, 0)
    scores = pl.pallas_call(
        functools.partial(_dsa_sample_score_kernel, layer=layer, n_pages=n_pages, t_new=T),
        grid_spec=pltpu.PrefetchScalarGridSpec(
            num_scalar_prefetch=1, grid=(B, n_steps),
            in_specs=[pl.BlockSpec((1, A_IDX_HEADS * T, A_IDX_DIM), per_b),
                      pl.BlockSpec((1, A_IDX_HEADS * T, 1), per_b)]
            + [pl.BlockSpec((None, PAGE_SIZE, n_layers, A_IDX_DIM), lambda b, p, pt, i=i: (page_of(b, p, pt, i), 0, 0, 0))
               for i in range(PAGES_PER_STEP)]
            + [pl.BlockSpec((1, STEP_KEYS, A_IDX_DIM), per_b)],
            out_specs=pl.BlockSpec((1, T, STEP_KEYS), lambda b, p, pt: (b, 0, p))),
        out_shape=jax.ShapeDtypeStruct((B, T, n_keys), jnp.float32),
        compiler_params=_params("parallel", "arbitrary"))(page_table, qi, wi, *[pool_idx] * PAGES_PER_STEP, pad_rows(ki))
    chunk = STEP_KEYS
    bias = pl.pallas_call(
        functools.partial(_dsa_sample_select_kernel, topk=topk, chunk=chunk, q_pos0=past),
        grid=(B,),
        in_specs=[pl.BlockSpec((1, T, n_keys), lambda b: (b, 0, 0))],
        out_specs=pl.BlockSpec((1, T, n_keys), lambda b: (b, 0, 0)),
        out_shape=jax.ShapeDtypeStruct((B, T, n_keys), jnp.float32),
        scratch_shapes=[pltpu.VMEM((T, n_keys), jnp.int32)],
        compiler_params=_params("parallel"))(scores)
    kv_new = jnp.stack([k.reshape(B, T, A_KV, HEAD_DIM), v.reshape(B, T, A_KV, HEAD_DIM)], axis=2)
    o = pl.pallas_call(
        functools.partial(_dsa_sample_attend_kernel, n_pages=n_pages, t_new=T),
        grid_spec=pltpu.PrefetchScalarGridSpec(
            num_scalar_prefetch=1, grid=(B, n_steps),
            in_specs=[pl.BlockSpec((1, A_HEADS * T, HEAD_DIM), per_b),
                      pl.BlockSpec((1, T, STEP_KEYS), lambda b, p, pt: (b, 0, p))]
            + [pl.BlockSpec((None, PAGE_SIZE, None, 2, A_KV, HEAD_DIM),
                            lambda b, p, pt, i=i: (page_of(b, p, pt, i), 0, layer, 0, 0, 0)) for i in range(PAGES_PER_STEP)]
            + [pl.BlockSpec((1, STEP_KEYS, 2, A_KV, HEAD_DIM), lambda b, p, pt: (b, 0, 0, 0, 0))],
            out_specs=pl.BlockSpec((1, A_HEADS * T, HEAD_DIM), per_b),
            scratch_shapes=[pltpu.VMEM((A_HEADS * T, 1), jnp.float32), pltpu.VMEM((A_HEADS * T, 1), jnp.float32),
                            pltpu.VMEM((A_HEADS * T, HEAD_DIM), jnp.float32)]),
        out_shape=jax.ShapeDtypeStruct((B, A_HEADS * T, HEAD_DIM), jnp.bfloat16),
        compiler_params=_params("parallel", "arbitrary"))(
            page_table, q, bias, *[pool_kv] * PAGES_PER_STEP, pad_rows(kv_new))
    o = o.reshape(B, A_HEADS, T, HEAD_DIM).transpose(0, 2, 1, 3).reshape(B, T, nq)
    return o, kv_new, ki


TRI_PACK = 4
TRI_GROUPS = 4


def _dot_bf16x3(a, b):
    a_hi, b_hi = _bf16(a), _bf16(b)
    a_lo, b_lo = _bf16(a - a_hi.astype(jnp.float32)), _bf16(b - b_hi.astype(jnp.float32))
    dot = functools.partial(jnp.dot, preferred_element_type=jnp.float32)
    return dot(a_hi, b_hi) + (dot(a_hi, b_lo) + dot(a_lo, b_hi))


def _unit_lower_inverse_kernel(a_ref, o_ref):
    c = a_ref.shape[-1]
    size = TRI_PACK * c
    eye = jnp.where(lax.broadcasted_iota(jnp.int32, (size, size), 0) == lax.broadcasted_iota(jnp.int32, (size, size), 1), 1.0, 0.0)
    zero = jnp.zeros((c, c), jnp.float32)
    for i in range(TRI_GROUPS):
        n = jnp.concatenate([jnp.concatenate([-a_ref[i, j] if jj == j else zero for jj in range(TRI_PACK)], axis=1)
                             for j in range(TRI_PACK)], axis=0)
        t, p = eye + n, n
        for _ in range((c - 1).bit_length() - 1):
            p = _dot_bf16x3(p, p)
            t = t + _dot_bf16x3(t, p)
        for j in range(TRI_PACK):
            o_ref[i, j] = t[j * c:(j + 1) * c, j * c:(j + 1) * c]


def unit_lower_inverse(a):
    c = a.shape[-1]
    n = math.prod(a.shape[:-2])
    per_step = TRI_GROUPS * TRI_PACK
    assert n % per_step == 0
    blk = (TRI_GROUPS, TRI_PACK, c, c)
    out = pl.pallas_call(
        _unit_lower_inverse_kernel, grid=(n // per_step,),
        in_specs=[pl.BlockSpec(blk, lambda i: (i, 0, 0, 0))],
        out_specs=pl.BlockSpec(blk, lambda i: (i, 0, 0, 0)),
        out_shape=jax.ShapeDtypeStruct((n // TRI_PACK, TRI_PACK, c, c), jnp.float32),
        compiler_params=_params("parallel"))(a.reshape(n // TRI_PACK, TRI_PACK, c, c))
    return out.reshape(a.shape)
def chunk_gated_delta(q, k, v, g, beta, s0):
    B, T, H, DK = q.shape
    C = B_CHUNK
    n = -(-T // C)
    pad = n * C - T

    def chunks(a):
        a = jnp.pad(a, [(0, 0), (0, pad)] + [(0, 0)] * (a.ndim - 2))
        a = a.reshape(B, n, C, *a.shape[2:])
        return jnp.moveaxis(a, (1, 3), (0, 2))

    qc, kc, vc, bc = chunks(q) * DK ** -0.5, chunks(k), chunks(v), chunks(beta)
    gc = jnp.cumsum(chunks(g), axis=-1)
    incl = jnp.tril(jnp.ones((C, C), bool))
    strict = jnp.tril(jnp.ones((C, C), bool), -1)
    decay = jnp.exp(jnp.where(incl, gc[..., :, None] - gc[..., None, :], NEG))
    kb = kc * bc[..., None]
    a_mat = jnp.where(strict, jnp.einsum("...id,...jd->...ij", kb, kc) * decay, 0.0)
    t_mat = unit_lower_inverse(a_mat)
    u = t_mat @ (vc * bc[..., None])
    w = t_mat @ (kb * jnp.exp(gc)[..., None])
    attn = jnp.einsum("...id,...jd->...ij", qc, kc) * decay

    def step(s, inp):
        q_i, k_i, u_i, w_i, g_i, attn_i = inp
        v_new = u_i - w_i @ s
        o = (q_i * jnp.exp(g_i)[..., None]) @ s + attn_i @ v_new
        g_last = g_i[..., -1]
        s = s * jnp.exp(g_last)[..., None, None] + jnp.einsum(
            "bhcd,bhce->bhde", k_i * jnp.exp(g_last[..., None] - g_i)[..., None], v_new)
        return s, o

    s_final, o = lax.scan(step, s0, (qc, kc, u, w, gc, attn))
    o = jnp.moveaxis(o, (0, 2), (1, 3)).reshape(B, n * C, H, -1)[:, :T]
    return o, s_final


def gdn_mixer(zin, conv_w, a_log, dt_bias, norm_g, conv_state, rec_state):
    B, T, _ = zin.shape
    qkv, z, b_raw, a_raw = split_cols(zin, [B_CONV_CH, B_V_HEADS * B_DV, B_V_HEADS, B_V_HEADS])
    x_ext = jnp.concatenate([conv_state, qkv], axis=1)
    conv = sum(x_ext[:, i:i + T] * conv_w[i][None, None, :] for i in range(B_CONV))
    q, k, v = split_cols(jax.nn.silu(conv), [B_QK_HEADS * B_DK, B_QK_HEADS * B_DK, B_V_HEADS * B_DV])
    rep = B_V_HEADS // B_QK_HEADS
    q = jnp.repeat(l2_normalize(q.reshape(B, T, B_QK_HEADS, B_DK)), rep, axis=2)
    k = jnp.repeat(l2_normalize(k.reshape(B, T, B_QK_HEADS, B_DK)), rep, axis=2)
    v = v.reshape(B, T, B_V_HEADS, B_DV)
    beta = jax.nn.sigmoid(b_raw)
    g = -jnp.exp(a_log) * jax.nn.softplus(a_raw + dt_bias)
    o, s_new = chunk_gated_delta(q, k, v, g, beta, rec_state)
    o = rms_norm(o, norm_g) * jax.nn.silu(z.reshape(B, T, B_V_HEADS, B_DV))
    return o.reshape(B, T, B_V_HEADS * B_DV), x_ext[:, -(B_CONV - 1):], s_new


def nsa_compress(kv, n_cmp, pe, w1, w2, layer):
    B = kv.shape[0]
    n_row = n_cmp + 1
    x = kv[:, :n_row * CMP_STRIDE].reshape(B, n_row, CMP_STRIDE, 2, C_KV, HEAD_DIM)
    return nsa_compress_rows(x.transpose(3, 0, 4, 1, 2, 5).reshape(2, B, C_KV, n_row, CMP_STRIDE * HEAD_DIM),
                             n_cmp, pe, w1, w2, layer)


def nsa_compress_rows(x, n_cmp, pe, w1, w2, layer):
    _, B, _, n_row, half = x.shape
    x = x.reshape(2, B * C_KV * n_row, half)
    w1l = w1[layer]
    w_halves = jnp.concatenate([w1l[:, :half], w1l[:, half:]], axis=-1)
    pe_rows = jnp.broadcast_to(pe[layer].transpose(1, 0, 2).reshape(2, 1, CMP_BLK * HEAD_DIM), (2, 8, CMP_BLK * HEAD_DIM))
    outs = []
    for e in range(2):
        ab = matmul(x[e], w_halves, e).reshape(B, C_KV, n_row, 2, HEAD_DIM)
        c = matmul(pe_rows[e], w1l, e)[0]
        hid = jax.nn.gelu(ab[:, :, :n_cmp, 0] + ab[:, :, 1:n_cmp + 1, 1] + c)
        outs.append(matmul(hid.reshape(B * C_KV * n_cmp, HEAD_DIM), w2[layer], e).reshape(B, C_KV, n_cmp, HEAD_DIM))
    return outs


NSA_TQ = 128
NSA_TK = 512
SLC_SHIFT = SLC_BLK.bit_length() - 1


def _nsa_prompt_kernel(q_ref, gl_ref, kct_ref, vc_ref, kst_ref, vs_ref, kwt_ref, vw_ref, o_ref, key_ref, bias_ref, *,
                       n_cmp, n_slc):
    tq, tk = NSA_TQ, NSA_TK
    rows = C_REP * tq
    q_start = pl.program_id(1) * tq
    n_chunks = (q_start + tq + tk - 1) // tk
    win_lo = jnp.maximum(q_start - (WINDOW - 1), 0) // tk
    n_pad = kct_ref.shape[2]
    iota = lambda shape, ax: lax.broadcasted_iota(jnp.int32, shape, ax)
    tile = lambda a: jnp.concatenate([a] * C_REP, axis=0)

    n_c = iota((tq, n_pad), 1)
    vis = tile((n_c * CMP_STRIDE + CMP_BLK - 1 <= q_start + iota((tq, n_pad), 0)) & (n_c < n_cmp))
    n_o, m_o = iota((n_pad, LANES), 0) * CMP_STRIDE, iota((n_pad, LANES), 1) * SLC_BLK
    overlap = jnp.where((n_o < m_o + SLC_BLK) & (n_o + CMP_BLK > m_o) & (iota((n_pad, LANES), 1) < n_slc), 1.0, 0.0)
    cur = jnp.right_shift(q_start + iota((tq, LANES), 0), SLC_SHIFT)
    blk = iota((tq, LANES), 1)
    o_cmp = []
    for g in range(C_KV):
        qg = _stack_heads(q_ref, g, C_REP)
        s = jnp.dot(qg, kct_ref[0, g * HEAD_DIM:(g + 1) * HEAD_DIM, :], preferred_element_type=jnp.float32) * ATTN_SCALE
        s = jnp.where(vis, s, NEG)
        e = jnp.exp(s - jnp.max(s, axis=-1, keepdims=True))
        p = jnp.where(vis, e / jnp.sum(e, axis=-1, keepdims=True), 0.0)
        o_cmp.append(jnp.dot(_bf16(p), vc_ref[0, :, g * HEAD_DIM:(g + 1) * HEAD_DIM], preferred_element_type=jnp.float32))
        p_grp = sum(p[r * tq:(r + 1) * tq] for r in range(C_REP))
        imp = jnp.dot(p_grp, overlap, preferred_element_type=jnp.float32, precision=lax.Precision.HIGHEST)
        imp = jnp.where((blk == 0) | (blk == cur), FORCE, imp)
        key_ref[g * tq:(g + 1) * tq, :] = _sortable_key(jnp.where(blk <= cur, imp, NEG))
    k_sel = min(N_SEL, n_slc)
    thr = _kth_largest_key(key_ref, 1, LANES, k_sel, rows)
    _topk_bias(key_ref, bias_ref, thr, 1, LANES, k_sel, rows, tile(cur))

    t_k = q_start + iota((tq, tk), 0)
    s_k = iota((tq, tk), 1)
    blk_e, s_e = iota((LANES, tk), 0), iota((LANES, tk), 1)
    gate = jax.nn.sigmoid(gl_ref[0])

    def win_bias(off):
        d = t_k - (off + s_k)
        return tile(jnp.where((d >= 0) & (d < WINDOW), 0.0, NEG))

    for g in range(C_KV):
        qg = _stack_heads(q_ref, g, C_REP)
        sel = _bf16(jnp.where(bias_ref[g * tq:(g + 1) * tq, :] == 0.0, 1.0, 0.0))

        def slc_bias(off, sel=sel):
            expand = _bf16(jnp.where(blk_e == jnp.right_shift(off + s_e, SLC_SHIFT), 1.0, 0.0))
            hit = jnp.dot(sel, expand, preferred_element_type=jnp.float32)
            return tile(jnp.where((hit > 0.5) & (off + s_k <= t_k), 0.0, NEG))

        o_slc = _flash_group(qg, kst_ref, vs_ref, g, 0, n_chunks, tk, slc_bias)
        o_win = _flash_group(qg, kwt_ref, vw_ref, g, win_lo, n_chunks, tk, win_bias)
        for r in range(C_REP):
            h = g * C_REP + r
            rs = slice(r * tq, (r + 1) * tq)
            o = (gate[:, 3 * h:3 * h + 1] * o_cmp[g][rs] + gate[:, 3 * h + 1:3 * h + 2] * o_slc[rs]
                 + gate[:, 3 * h + 2:3 * h + 3] * o_win[rs])
            o_ref[0, :, h * HEAD_DIM:(h + 1) * HEAD_DIM] = o.astype(o_ref.dtype)


def nsa_prompt(z, pe, w1, w2, layer):
    B, T, _ = z.shape
    nq, nk = C_HEADS * HEAD_DIM, C_KV * HEAD_DIM
    q = _bf16(z[..., :nq])
    branch = [z[..., nq + 2 * i * nk:nq + 2 * (i + 1) * nk] for i in range(3)]
    gl = z[..., nq + 6 * nk:]
    as_kv = lambda a: a.reshape(B, T, 2, C_KV, HEAD_DIM)
    n_cmp = (T - CMP_BLK) // CMP_STRIDE + 1
    k_cmp, v_cmp = nsa_compress(as_kv(branch[0]), n_cmp, pe, w1, w2, layer)
    n_pad = -(-n_cmp // LANES) * LANES
    pad = ((0, 0), (0, 0), (0, n_pad - n_cmp), (0, 0))
    kct = _bf16(jnp.pad(k_cmp, pad)).transpose(0, 1, 3, 2).reshape(B, nk, n_pad)
    vc = _bf16(jnp.pad(v_cmp, pad)).transpose(0, 2, 1, 3).reshape(B, n_pad, nk)
    kt = lambda a: _bf16(a[..., :nk]).transpose(0, 2, 1)
    tq = NSA_TQ
    full_t = lambda shape: pl.BlockSpec(shape, lambda b, i: (b, 0, 0))
    o = pl.pallas_call(
        functools.partial(_nsa_prompt_kernel, n_cmp=n_cmp, n_slc=-(-T // SLC_BLK)),
        grid=(B, T // tq),
        in_specs=[
            pl.BlockSpec((1, tq, nq), lambda b, i: (b, i, 0)),
            pl.BlockSpec((1, tq, 3 * C_HEADS), lambda b, i: (b, i, 0)),
            full_t((1, nk, n_pad)), full_t((1, n_pad, nk)),
            full_t((1, nk, T)), full_t((1, T, nk)), full_t((1, nk, T)), full_t((1, T, nk)),
        ],
        out_specs=pl.BlockSpec((1, tq, nq), lambda b, i: (b, i, 0)),
        out_shape=jax.ShapeDtypeStruct((B, T, nq), jnp.bfloat16),
        scratch_shapes=[pltpu.VMEM((C_REP * tq, LANES), jnp.int32), pltpu.VMEM((C_REP * tq, LANES), jnp.float32)],
        compiler_params=_params("parallel", "arbitrary"))(
            q, gl, kct, vc, kt(branch[1]), _bf16(branch[1][..., nk:]), kt(branch[2]), _bf16(branch[2][..., nk:]))
    return o, as_kv(branch[0]), as_kv(branch[1]), as_kv(branch[2])[:, -WINDOW:]


def _page_copy_kernel(pt_ref, pool_ref, o_ref):
    o_ref[0] = pool_ref[...]


def gather_pages_pallas(pool, page_table, layer):
    B, n_pages = page_table.shape
    row = pool.shape[3:]
    return pl.pallas_call(
        _page_copy_kernel,
        grid_spec=pltpu.PrefetchScalarGridSpec(
            num_scalar_prefetch=1, grid=(B, n_pages),
            in_specs=[pl.BlockSpec((None, PAGE_SIZE, None) + row, lambda b, p, pt: (pt[b, p], 0, layer, 0, 0, 0))],
            out_specs=pl.BlockSpec((1, PAGE_SIZE) + row, lambda b, p, pt: (b, p, 0, 0, 0))),
        out_shape=jax.ShapeDtypeStruct((B, n_pages * PAGE_SIZE) + row, pool.dtype),
        compiler_params=_params("parallel", "arbitrary"))(page_table, pool)


def _page_rowgroup_kernel(pt_ref, *refs):
    pool_refs, o_ref = refs[:PAGES_PER_STEP], refs[PAGES_PER_STEP]
    n_row = PAGE_SIZE // CMP_STRIDE
    for e in range(2):
        for g in range(C_KV):
            rows = [jnp.concatenate([r[pl.ds(j, n_row, stride=CMP_STRIDE), e, g, :] for j in range(CMP_STRIDE)], axis=1)
                    for r in pool_refs]
            o_ref[e, 0, g] = jnp.concatenate(rows, axis=0).astype(o_ref.dtype)


def gather_page_rowgroups(pool, page_table, layer):
    B, n_pages = page_table.shape
    assert n_pages % PAGES_PER_STEP == 0
    n_row = PAGE_SIZE // CMP_STRIDE
    blk = (2, 1, C_KV, PAGES_PER_STEP * n_row, CMP_STRIDE * HEAD_DIM)
    return pl.pallas_call(
        _page_rowgroup_kernel,
        grid_spec=pltpu.PrefetchScalarGridSpec(
            num_scalar_prefetch=1, grid=(B, n_pages // PAGES_PER_STEP),
            in_specs=[pl.BlockSpec((None, PAGE_SIZE, None, 2, C_KV, HEAD_DIM),
                                   lambda b, p, pt, i=i: (pt[b, p * PAGES_PER_STEP + i], 0, layer, 0, 0, 0))
                      for i in range(PAGES_PER_STEP)],
            out_specs=pl.BlockSpec(blk, lambda b, p, pt: (0, b, 0, p, 0))),
        out_shape=jax.ShapeDtypeStruct((2, B, C_KV, n_pages * n_row, CMP_STRIDE * HEAD_DIM), jnp.bfloat16),
        compiler_params=_params("parallel", "arbitrary"))(page_table, *[pool] * PAGES_PER_STEP)


def _nsa_sample_cmpwin_kernel(q_ref, kct_ref, vc_ref, win_ref, ocmp_ref, owin_ref, sel_ref, key_ref, *,
                              n_cmp, n_slc, q_pos0, win_pos0, t_new):
    rows_g = C_REP * t_new
    nk = C_KV * HEAD_DIM
    n_pad, n_blk, n_win = kct_ref.shape[2], key_ref.shape[1], win_ref.shape[1]
    iota = lambda shape, ax: lax.broadcasted_iota(jnp.int32, shape, ax)
    tile = lambda a: jnp.concatenate([a] * C_REP, axis=0)
    n_c = iota((t_new, n_pad), 1)
    vis = tile((n_c * CMP_STRIDE + CMP_BLK - 1 <= q_pos0 + iota((t_new, n_pad), 0)) & (n_c < n_cmp))
    n_o, m_o = iota((n_pad, n_blk), 0) * CMP_STRIDE, iota((n_pad, n_blk), 1) * SLC_BLK
    overlap = jnp.where((n_o < m_o + SLC_BLK) & (n_o + CMP_BLK > m_o) & (iota((n_pad, n_blk), 1) < n_slc), 1.0, 0.0)
    cur = jnp.right_shift(q_pos0 + iota((t_new, n_blk), 0), SLC_SHIFT)
    blk = iota((t_new, n_blk), 1)
    d = q_pos0 + iota((t_new, n_win), 0) - (win_pos0 + iota((t_new, n_win), 1))
    win_ok = tile((win_pos0 + iota((t_new, n_win), 1) >= 0) & (d >= 0) & (d < WINDOW))
    for g in range(C_KV):
        rs = slice(g * rows_g, (g + 1) * rows_g)
        qg = q_ref[0, rs]
        s = jnp.dot(qg, kct_ref[0, g * HEAD_DIM:(g + 1) * HEAD_DIM, :], preferred_element_type=jnp.float32) * ATTN_SCALE
        s = jnp.where(vis, s, NEG)
        e = jnp.exp(s - jnp.max(s, axis=-1, keepdims=True))
        p = jnp.where(vis, e / jnp.sum(e, axis=-1, keepdims=True), 0.0)
        ocmp_ref[0, rs] = jnp.dot(_bf16(p), vc_ref[0, :, g * HEAD_DIM:(g + 1) * HEAD_DIM], preferred_element_type=jnp.float32)
        p_grp = sum(p[r * t_new:(r + 1) * t_new] for r in range(C_REP))
        imp = jnp.dot(p_grp, overlap, preferred_element_type=jnp.float32, precision=lax.Precision.HIGHEST)
        imp = jnp.where((blk == 0) | (blk == cur), FORCE, imp)
        key_ref[g * t_new:(g + 1) * t_new, :] = _sortable_key(jnp.where(blk <= cur, imp, NEG))
        kw = _bf16(win_ref[0, :, g * HEAD_DIM:(g + 1) * HEAD_DIM])
        vw = _bf16(win_ref[0, :, nk + g * HEAD_DIM:nk + (g + 1) * HEAD_DIM])
        sw = jnp.where(win_ok, _nt_dot(qg, kw) * ATTN_SCALE, NEG)
        ew = jnp.exp(sw - jnp.max(sw, axis=-1, keepdims=True))
        pw = ew / jnp.sum(ew, axis=-1, keepdims=True)
        owin_ref[0, rs] = jnp.dot(_bf16(pw), vw, preferred_element_type=jnp.float32)
    k_sel = min(N_SEL, n_slc)
    rows = C_KV * t_new
    thr = _kth_largest_key(key_ref, 1, n_blk, k_sel, rows)
    _topk_bias(key_ref, sel_ref.at[0], thr, 1, n_blk, k_sel, rows, jnp.concatenate([cur[:, :LANES]] * C_KV, axis=0))


def _paged_attend_step(q_ref, kv_of, bias_of_group, m_ref, l_ref, acc_ref, n_groups, rows_g):
    for g in range(n_groups):
        rs = slice(g * rows_g, (g + 1) * rows_g)
        s = _nt_dot(q_ref[0, rs], _bf16(kv_of(0, g))) * ATTN_SCALE + bias_of_group(g)
        m = m_ref[rs]
        m_new = jnp.maximum(m, jnp.max(s, axis=-1, keepdims=True))
        pr = jnp.exp(s - m_new)
        alpha = jnp.exp(m - m_new)
        l_ref[rs] = alpha * l_ref[rs] + jnp.sum(pr, axis=-1, keepdims=True)
        acc_ref[rs] = alpha * acc_ref[rs] + jnp.dot(_bf16(pr), _bf16(kv_of(1, g)), preferred_element_type=jnp.float32)
        m_ref[rs] = m_new


def _init_online_softmax(m_ref, l_ref, acc_ref):
    m_ref[...] = jnp.full(m_ref.shape, NEG, jnp.float32)
    l_ref[...] = jnp.zeros(l_ref.shape, jnp.float32)
    acc_ref[...] = jnp.zeros(acc_ref.shape, jnp.float32)


def _nsa_sample_slc_kernel(pt_ref, q_ref, sel_ref, gate_ref, ocmp_ref, owin_ref, *rest, n_pages, t_new):
    pool_refs, (new_ref, o_ref, m_ref, l_ref, acc_ref) = rest[:PAGES_PER_STEP], rest[PAGES_PER_STEP:]
    p = pl.program_id(1)
    n_blk = sel_ref.shape[2]

    @pl.when(p == 0)
    def _():
        _init_online_softmax(m_ref, l_ref, acc_ref)

    def attend(kv_of):
        iota = lambda shape, ax: lax.broadcasted_iota(jnp.int32, shape, ax)
        causal = p * STEP_KEYS + iota((t_new, STEP_KEYS), 1) <= n_pages * PAGE_SIZE + iota((t_new, STEP_KEYS), 0)
        key_blk = (STEP_KEYS // SLC_BLK) * p + jnp.right_shift(iota((n_blk, STEP_KEYS), 1), SLC_SHIFT)
        expand = _bf16(jnp.where(iota((n_blk, STEP_KEYS), 0) == key_blk, 1.0, 0.0))

        def bias_of_group(g):
            sel = _bf16(jnp.where(sel_ref[0, g * t_new:(g + 1) * t_new, :] == 0.0, 1.0, 0.0))
            hit = jnp.dot(sel, expand, preferred_element_type=jnp.float32)
            return jnp.concatenate([jnp.where((hit > 0.5) & causal, 0.0, NEG)] * C_REP, axis=0)

        _paged_attend_step(q_ref, kv_of, bias_of_group, m_ref, l_ref, acc_ref, C_KV, C_REP * t_new)

    @pl.when(p < n_pages // PAGES_PER_STEP)
    def _():
        attend(lambda e, g: jnp.concatenate([r[:, e, g, :] for r in pool_refs], axis=0))

    @pl.when(p == n_pages // PAGES_PER_STEP)
    def _():
        attend(lambda e, g: new_ref[0, :, e, g, :])
        gate = jax.nn.sigmoid(gate_ref[0])
        o = gate[:, 0:1] * ocmp_ref[0] + gate[:, 1:2] * (acc_ref[...] / l_ref[...]) + gate[:, 2:3] * owin_ref[0]
        o_ref[0] = o.astype(o_ref.dtype)


def nsa_sample(z, pe, w1, w2, pool_cmp, pool_slc, win_buf, page_table, layer):
    B, T, _ = z.shape
    n_pages = page_table.shape[1]
    past = n_pages * PAGE_SIZE
    nq, nk = C_HEADS * HEAD_DIM, C_KV * HEAD_DIM
    head_major = lambda a, d: a.reshape(B, T, -1, d).transpose(0, 2, 1, 3).reshape(B, -1, d)
    as_kv = lambda a: a.reshape(B, -1, 2, C_KV, HEAD_DIM)
    q = head_major(_bf16(z[..., :nq]), HEAD_DIM)
    branch = [z[..., nq + 2 * i * nk:nq + 2 * (i + 1) * nk] for i in range(3)]
    gl = head_major(z[..., nq + 6 * nk:], 3)
    n_cmp = (past + T - CMP_BLK) // CMP_STRIDE + 1
    if (n_cmp + 1) * CMP_STRIDE <= past:
        k_cmp, v_cmp = nsa_compress_rows(gather_page_rowgroups(pool_cmp, page_table, layer), n_cmp, pe, w1, w2, layer)
    else:
        cmp_rows = jnp.concatenate([gather_pages_pallas(pool_cmp, page_table, layer), as_kv(branch[0])], axis=1)
        k_cmp, v_cmp = nsa_compress(cmp_rows, n_cmp, pe, w1, w2, layer)
    n_pad = -(-n_cmp // LANES) * LANES
    pad = ((0, 0), (0, 0), (0, n_pad - n_cmp), (0, 0))
    kct = _bf16(jnp.pad(k_cmp, pad)).transpose(0, 1, 3, 2).reshape(B, nk, n_pad)
    vc = _bf16(jnp.pad(v_cmp, pad)).transpose(0, 2, 1, 3).reshape(B, n_pad, nk)
    n_slc = -(-(past + T) // SLC_BLK)
    n_blk = -(-n_slc // LANES) * LANES
    win_all = jnp.concatenate([win_buf.reshape(B, WINDOW, 2 * nk), branch[2]], axis=1)
    n_win = -(-(WINDOW + T) // LANES) * LANES
    win_pad = jnp.pad(win_all, ((0, 0), (0, n_win - WINDOW - T), (0, 0)))
    per_b = lambda shape: pl.BlockSpec(shape, lambda b: (b, 0, 0))
    hm_rows = C_HEADS * T
    o_cmp, o_win, sel = pl.pallas_call(
        functools.partial(_nsa_sample_cmpwin_kernel, n_cmp=n_cmp, n_slc=n_slc, q_pos0=past, win_pos0=past - WINDOW, t_new=T),
        grid=(B,),
        in_specs=[per_b((1, hm_rows, HEAD_DIM)), per_b((1, nk, n_pad)), per_b((1, n_pad, nk)), per_b((1, n_win, 2 * nk))],
        out_specs=[per_b((1, hm_rows, HEAD_DIM)), per_b((1, hm_rows, HEAD_DIM)), per_b((1, C_KV * T, n_blk))],
        out_shape=[jax.ShapeDtypeStruct((B, hm_rows, HEAD_DIM), jnp.float32)] * 2
        + [jax.ShapeDtypeStruct((B, C_KV * T, n_blk), jnp.float32)],
        scratch_shapes=[pltpu.VMEM((C_KV * T, n_blk), jnp.int32)],
        compiler_params=_params("parallel"))(q, kct, vc, win_pad)
    per_b3 = lambda shape: pl.BlockSpec(shape, lambda b, p, pt: (b, 0, 0))
    assert n_pages % PAGES_PER_STEP == 0 and T <= STEP_KEYS
    new_pad = jnp.pad(as_kv(branch[1]), ((0, 0), (0, STEP_KEYS - T), (0, 0), (0, 0), (0, 0)))
    kv_row = (2, C_KV, HEAD_DIM)
    page_of = lambda b, p, pt, i: pt[b, jnp.minimum(p * PAGES_PER_STEP + i, n_pages - 1)]
    o = pl.pallas_call(
        functools.partial(_nsa_sample_slc_kernel, n_pages=n_pages, t_new=T),
        grid_spec=pltpu.PrefetchScalarGridSpec(
            num_scalar_prefetch=1, grid=(B, n_pages // PAGES_PER_STEP + 1),
            in_specs=[per_b3((1, hm_rows, HEAD_DIM)), per_b3((1, C_KV * T, n_blk)), per_b3((1, hm_rows, 3)),
                      per_b3((1, hm_rows, HEAD_DIM)), per_b3((1, hm_rows, HEAD_DIM))]
            + [pl.BlockSpec((None, PAGE_SIZE, None) + kv_row, lambda b, p, pt, i=i: (page_of(b, p, pt, i), 0, layer, 0, 0, 0))
               for i in range(PAGES_PER_STEP)]
            + [pl.BlockSpec((1, STEP_KEYS) + kv_row, lambda b, p, pt: (b, 0, 0, 0, 0))],
            out_specs=per_b3((1, hm_rows, HEAD_DIM)),
            scratch_shapes=[pltpu.VMEM((hm_rows, 1), jnp.float32), pltpu.VMEM((hm_rows, 1), jnp.float32),
                            pltpu.VMEM((hm_rows, HEAD_DIM), jnp.float32)]),
        out_shape=jax.ShapeDtypeStruct((B, hm_rows, HEAD_DIM), jnp.bfloat16),
        compiler_params=_params("parallel", "arbitrary"))(
            page_table, q, sel, gl, o_cmp, o_win, *[pool_slc] * PAGES_PER_STEP, new_pad)
    o = o.reshape(B, C_HEADS, T, HEAD_DIM).transpose(0, 2, 1, 3).reshape(B, T, nq)
    return o, as_kv(branch[0]), as_kv(branch[1]), as_kv(win_all)[:, -WINDOW:]


def kernel(x_prompt, x_sample, cache_dsa_kv, cache_dsa_idx, state_gdn_conv, state_gdn_rec, cache_nsa_cmp, cache_nsa_slc, cache_nsa_win, page_table, c_prompt, c_sample, ada_w, ada_b, ln_g, ln_b, dsa_w_in, dsa_w_out, gdn_w_in, gdn_conv_w, gdn_a_log, gdn_dt_bias, gdn_norm_g, gdn_w_out, nsa_w_in, nsa_cmp_pe, nsa_cmp_w1, nsa_cmp_w2, nsa_w_out, moe_w_router, moe_b_router, moe_w_in, moe_b_in, moe_w_out, moe_b_out):
    xp, xs = x_prompt, x_sample
    bp, tp, _ = xp.shape
    bs, ts, _ = xs.shape
    n_p, n_s = bp * tp, bs * ts
    outs = {k: [] for k in ("dsa_kv_p", "dsa_kv_s", "dsa_idx_p", "dsa_idx_s", "gdn_conv_p", "gdn_conv_s", "gdn_rec_p",
                            "gdn_rec_s", "nsa_cmp_p", "nsa_cmp_s", "nsa_slc_p", "nsa_slc_s", "nsa_win_p", "nsa_win_s")}
    moe_w = (moe_w_router, moe_b_router, moe_w_in, moe_b_in, moe_w_out, moe_b_out)
    c_all = jax.nn.silu(jnp.concatenate([c_prompt, c_sample], axis=0))
    for i in range(DEPTH):
        kind, j = i % N_MIXERS, i // N_MIXERS
        mod = matmul(c_all, ada_w, i, ada_b).reshape(bp + bs, 6, D_MODEL)
        mp, ms = mod[:bp], mod[bp:]
        if kind == 0:
            w_in, w_out = dsa_w_in, dsa_w_out
        elif kind == 1:
            w_in, w_out = gdn_w_in, gdn_w_out
        else:
            w_in, w_out = nsa_w_in, nsa_w_out
        zp = matmul_modulated(xp, mp[:, 1], mp[:, 0], w_in, j)
        hs = xs * (1.0 + ms[:, 1][:, None, :]) + ms[:, 0][:, None, :]
        zs = matmul(hs.reshape(n_s, D_MODEL), w_in, j).reshape(bs, ts, -1)
        if kind == 0:
            op, kv_p, ki_p = dsa_prompt(zp)
            os_, kv_s, ki_s = dsa_sample(zs, cache_dsa_kv, cache_dsa_idx, page_table, j)
            outs["dsa_kv_p"].append(kv_p)
            outs["dsa_kv_s"].append(kv_s)
            outs["dsa_idx_p"].append(ki_p)
            outs["dsa_idx_s"].append(ki_s)
        elif kind == 1:
            gdn_args = (gdn_conv_w[j], gdn_a_log[j], gdn_dt_bias[j], gdn_norm_g[j])
            zero_conv = jnp.zeros((bp, B_CONV - 1, B_CONV_CH), xp.dtype)
            zero_rec = jnp.zeros((bp, B_V_HEADS, B_DK, B_DV), xp.dtype)
            op, conv_p, rec_p = gdn_mixer(zp, *gdn_args, zero_conv, zero_rec)
            os_, conv_s, rec_s = gdn_mixer(zs, *gdn_args, state_gdn_conv[:, j], state_gdn_rec[:, j])
            outs["gdn_conv_p"].append(conv_p)
            outs["gdn_conv_s"].append(conv_s)
            outs["gdn_rec_p"].append(rec_p)
            outs["gdn_rec_s"].append(rec_s)
        else:
            nsa_args = (nsa_cmp_pe, nsa_cmp_w1, nsa_cmp_w2)
            op, cmp_p, slc_p, win_p = nsa_prompt(zp, *nsa_args, j)
            os_, cmp_s, slc_s, win_s = nsa_sample(zs, *nsa_args, cache_nsa_cmp, cache_nsa_slc, cache_nsa_win[:, j],
                                                  page_table, j)
            outs["nsa_cmp_p"].append(cmp_p)
            outs["nsa_cmp_s"].append(cmp_s)
            outs["nsa_slc_p"].append(slc_p)
            outs["nsa_slc_s"].append(slc_s)
            outs["nsa_win_p"].append(win_p)
            outs["nsa_win_s"].append(win_s)
        yp = matmul(op.reshape(n_p, -1), w_out, j).reshape(bp, tp, D_MODEL)
        ys = matmul(os_.reshape(n_s, -1), w_out, j).reshape(bs, ts, D_MODEL)
        xp, hp = post_norm_modulate(xp, yp, mp[:, 2], ln_g[i, 0], ln_b[i, 0], mp[:, 4], mp[:, 3])
        xs, hs = post_norm_modulate(xs, ys, ms[:, 2], ln_g[i, 0], ln_b[i, 0], ms[:, 4], ms[:, 3])
        h_all = jnp.concatenate([hp.reshape(n_p, D_MODEL), hs.reshape(n_s, D_MODEL)], axis=0)
        y_k, w_k = moe(h_all, i, *moe_w)
        xp = post_norm_moe(xp, y_k, w_k, 0, mp[:, 5], ln_g[i, 1], ln_b[i, 1])
        xs = post_norm_moe(xs, y_k, w_k, n_p, ms[:, 5], ln_g[i, 1], ln_b[i, 1])
    st = lambda k, ax: jnp.stack(outs[k], axis=ax)
    return (xp, xs,
            st("dsa_kv_p", 2), st("dsa_kv_s", 2), st("dsa_idx_p", 2), st("dsa_idx_s", 2),
            st("gdn_conv_p", 1), st("gdn_conv_s", 1), st("gdn_rec_p", 1), st("gdn_rec_s", 1),
            st("nsa_cmp_p", 2), st("nsa_cmp_s", 2), st("nsa_slc_p", 2), st("nsa_slc_s", 2),
            st("nsa_win_p", 1), st("nsa_win_s", 1))
```

```python
import functools
import math

import jax
import jax.numpy as jnp
import numpy as np
from jax import lax
from jax.experimental import pallas as pl
from jax.experimental.pallas import tpu as pltpu

D_MODEL = 2048
DEPTH = 4
PAGE_SIZE = 128
N_MIXERS = 3
ALPHA = (2.0 * DEPTH) ** 0.25
LN_EPS = 1e-5
NORM_EPS = 1e-6
NEG = -1e30
FORCE = 1e9
HEAD_DIM = 128
ATTN_SCALE = HEAD_DIM ** -0.5

A_HEADS = D_MODEL // HEAD_DIM
A_KV = 4
A_REP = A_HEADS // A_KV
A_IDX_HEADS = 16
A_IDX_DIM = 64
A_TOPK = 256

B_QK_HEADS = 16
B_V_HEADS = 32
B_DK = 128
B_DV = 128
B_CONV = 4
B_CHUNK = 64
B_CONV_CH = 2 * B_QK_HEADS * B_DK + B_V_HEADS * B_DV

C_HEADS = D_MODEL // HEAD_DIM
C_KV = 4
C_REP = C_HEADS // C_KV
CMP_BLK = 32
CMP_STRIDE = 16
SLC_BLK = 64
N_SEL = 16
WINDOW = 512

N_EXPERTS = 32
TOP_K = 4
D_FF = 2048
SWIGLU_LIMIT = 7.0
SWIGLU_ALPHA = 1.702

VMEM_LIMIT_BYTES = 56 * 1024 * 1024
MOE_ROW_BLK = 256
MOE_FF_TILE = 1024
MOE_OUT_TILE = 2048


def _params(*sem):
    return pltpu.CompilerParams(dimension_semantics=sem, vmem_limit_bytes=VMEM_LIMIT_BYTES)


def _bf16(x):
    return x.astype(jnp.bfloat16)


def _mm_kernel(x_ref, w_ref, o_ref):
    o_ref[...] = jnp.dot(_bf16(x_ref[...]), _bf16(w_ref[...]), preferred_element_type=jnp.float32)


def _mm_bias_kernel(x_ref, w_ref, b_ref, o_ref):
    o_ref[...] = jnp.dot(_bf16(x_ref[...]), _bf16(w_ref[...]), preferred_element_type=jnp.float32) + b_ref[...]


def _mm_mod_kernel(x_ref, sc_ref, sh_ref, w_ref, o_ref):
    h = x_ref[0] * (1.0 + sc_ref[0]) + sh_ref[0]
    o_ref[0] = jnp.dot(_bf16(h), _bf16(w_ref[...]), preferred_element_type=jnp.float32)


def matmul(x, w, layer, b=None, tm=512, tn=512):
    m, k = x.shape
    n = w.shape[2]
    tm = min(tm, m)
    tn = min(tn, n)
    grid = (pl.cdiv(m, tm), pl.cdiv(n, tn))
    in_specs = [pl.BlockSpec((tm, k), lambda i, j: (i, 0)), pl.BlockSpec((None, k, tn), lambda i, j: (layer, 0, j))]
    args = [x, w]
    body = _mm_kernel
    if b is not None:
        in_specs.append(pl.BlockSpec((None, 1, tn), lambda i, j: (layer, 0, j)))
        args.append(b.reshape(b.shape[0], 1, n))
        body = _mm_bias_kernel
    return pl.pallas_call(
        body, grid=grid, in_specs=in_specs,
        out_specs=pl.BlockSpec((tm, tn), lambda i, j: (i, j)),
        out_shape=jax.ShapeDtypeStruct((m, n), jnp.float32),
        compiler_params=_params("parallel", "parallel"))(*args)


def matmul_modulated(x, scale, shift, w, layer, tm=1024, tn=512):
    bsz, t, k = x.shape
    n = w.shape[2]
    tm = min(tm, t)
    tn = min(tn, n)
    grid = (bsz, pl.cdiv(t, tm), pl.cdiv(n, tn))
    mod_spec = pl.BlockSpec((1, 1, k), lambda b, i, j: (b, 0, 0))
    return pl.pallas_call(
        _mm_mod_kernel, grid=grid,
        in_specs=[pl.BlockSpec((1, tm, k), lambda b, i, j: (b, i, 0)), mod_spec, mod_spec,
                  pl.BlockSpec((None, k, tn), lambda b, i, j: (layer, 0, j))],
        out_specs=pl.BlockSpec((1, tm, tn), lambda b, i, j: (b, i, j)),
        out_shape=jax.ShapeDtypeStruct((bsz, t, n), jnp.float32),
        compiler_params=_params("parallel", "parallel", "parallel"))(
            x, scale.reshape(bsz, 1, k), shift.reshape(bsz, 1, k), w)


def _layer_norm(z, g, b):
    mu = jnp.mean(z, axis=-1, keepdims=True)
    zc = z - mu
    var = jnp.mean(zc * zc, axis=-1, keepdims=True)
    return zc * lax.rsqrt(var + LN_EPS) * g + b


def _postnorm_kernel(x_ref, y_ref, gate_ref, g_ref, b_ref, sc_ref, sh_ref, xo_ref, ho_ref):
    xn = _layer_norm(ALPHA * x_ref[0] + gate_ref[0] * y_ref[0], g_ref[...], b_ref[...])
    xo_ref[0] = xn
    ho_ref[0] = (xn * (1.0 + sc_ref[0]) + sh_ref[0]).astype(ho_ref.dtype)


def post_norm_modulate(x, y, gate, g, b, scale, shift, tm=256):
    bsz, t, d = x.shape
    tm = min(tm, t)
    row = pl.BlockSpec((1, tm, d), lambda bi, i: (bi, i, 0))
    per_seq = pl.BlockSpec((1, 1, d), lambda bi, i: (bi, 0, 0))
    shared = pl.BlockSpec((1, d), lambda bi, i: (0, 0))
    return pl.pallas_call(
        _postnorm_kernel, grid=(bsz, t // tm),
        in_specs=[row, row, per_seq, shared, shared, per_seq, per_seq],
        out_specs=[row, row],
        out_shape=[jax.ShapeDtypeStruct(x.shape, jnp.float32), jax.ShapeDtypeStruct(x.shape, jnp.bfloat16)],
        compiler_params=_params("parallel", "parallel"))(
            x, y, gate.reshape(bsz, 1, d), g.reshape(1, d), b.reshape(1, d),
            scale.reshape(bsz, 1, d), shift.reshape(bsz, 1, d))


def _postnorm_moe_kernel(x_ref, y0_ref, y1_ref, y2_ref, y3_ref, w_ref, gate_ref, g_ref, b_ref, xo_ref):
    w = w_ref[...]
    y = sum(w[:, k:k + 1] * r[...].astype(jnp.float32) for k, r in enumerate((y0_ref, y1_ref, y2_ref, y3_ref)))
    xo_ref[0] = _layer_norm(ALPHA * x_ref[0] + gate_ref[0] * y, g_ref[...], b_ref[...])


def post_norm_moe(x, ys, w, row0, gate, g, b, tm=256):
    bsz, t, d = x.shape
    tm = min(tm, t)
    per_seq_blocks = t // tm
    assert row0 % tm == 0
    row = pl.BlockSpec((1, tm, d), lambda bi, i: (bi, i, 0))
    tok = lambda width: pl.BlockSpec((tm, width), lambda bi, i: (row0 // tm + bi * per_seq_blocks + i, 0))
    per_seq = pl.BlockSpec((1, 1, d), lambda bi, i: (bi, 0, 0))
    shared = pl.BlockSpec((1, d), lambda bi, i: (0, 0))
    return pl.pallas_call(
        _postnorm_moe_kernel, grid=(bsz, per_seq_blocks),
        in_specs=[row] + [tok(d)] * TOP_K + [tok(TOP_K), per_seq, shared, shared],
        out_specs=row,
        out_shape=jax.ShapeDtypeStruct(x.shape, jnp.float32),
        compiler_params=_params("parallel", "parallel"))(x, *ys, w, gate.reshape(bsz, 1, d), g.reshape(1, d), b.reshape(1, d))


def moe_ffn(xs, blk_expert, n_used, layer, w_in, b_in, w_out, b_out):
    r = xs.shape[0]
    n_blk = r // MOE_ROW_BLK
    n_ff = D_FF // MOE_FF_TILE
    tm, tf, tn = MOE_ROW_BLK, MOE_FF_TILE, MOE_OUT_TILE
    b_in4 = b_in.reshape(DEPTH, N_EXPERTS, 1, 2 * D_FF)
    b_out4 = b_out.reshape(DEPTH, N_EXPERTS, 1, D_MODEL)
    ahead = pl.Buffered(2, use_lookahead=True)
    hbm = pl.BlockSpec(memory_space=pl.ANY)

    def in_call(be_ref, nu_ref, x_hbm, w_hbm, b_hbm, o_hbm):
        row_blk = lambda j, m: (jnp.minimum(m, nu_ref[0] - 1), 0)

        def body(idx, x_ref, wg_ref, wl_ref, bg_ref, bl_ref, o_ref):
            used = idx[1] < nu_ref[0]

            @pl.when(used)
            def _():
                x = x_ref[...]
                glu = jnp.dot(x, _bf16(wg_ref[...]), preferred_element_type=jnp.float32) + bg_ref[...]
                lin = jnp.dot(x, _bf16(wl_ref[...]), preferred_element_type=jnp.float32) + bl_ref[...]
                glu = jnp.minimum(glu, SWIGLU_LIMIT)
                lin = jnp.clip(lin, -SWIGLU_LIMIT, SWIGLU_LIMIT)
                o_ref[...] = (glu * jax.nn.sigmoid(SWIGLU_ALPHA * glu) * (lin + 1.0)).astype(o_ref.dtype)

            @pl.when(jnp.logical_not(used))
            def _():
                o_ref[...] = jnp.zeros(o_ref.shape, o_ref.dtype)

        pltpu.emit_pipeline(
            body, grid=(n_ff, n_blk), _explicit_indices=True,
            in_specs=[
                pl.BlockSpec((tm, D_MODEL), row_blk),
                pl.BlockSpec((None, None, D_MODEL, tf), lambda j, m: (layer, be_ref[m], 0, j), pipeline_mode=ahead),
                pl.BlockSpec((None, None, D_MODEL, tf), lambda j, m: (layer, be_ref[m], 0, n_ff + j), pipeline_mode=ahead),
                pl.BlockSpec((None, None, 1, tf), lambda j, m: (layer, be_ref[m], 0, j)),
                pl.BlockSpec((None, None, 1, tf), lambda j, m: (layer, be_ref[m], 0, n_ff + j)),
            ],
            out_specs=[pl.BlockSpec((tm, tf), lambda j, m: (m, j))])(x_hbm, w_hbm, w_hbm, b_hbm, b_hbm, o_hbm)

    act = pl.pallas_call(
        in_call,
        grid_spec=pltpu.PrefetchScalarGridSpec(num_scalar_prefetch=2, grid=(1,), in_specs=[hbm, hbm, hbm], out_specs=hbm),
        out_shape=jax.ShapeDtypeStruct((r, D_FF), jnp.bfloat16),
        compiler_params=_params("arbitrary"))(blk_expert, n_used, xs, w_in, b_in4)

    def out_call(be_ref, nu_ref, h_hbm, w_hbm, b_hbm, o_hbm):
        row_blk = lambda j, m: (jnp.minimum(m, nu_ref[0] - 1), 0)

        def body(idx, h_ref, w_ref, b_ref, o_ref):
            used = idx[1] < nu_ref[0]

            @pl.when(used)
            def _():
                y = jnp.dot(h_ref[...], _bf16(w_ref[...]), preferred_element_type=jnp.float32) + b_ref[...]
                o_ref[...] = y.astype(o_ref.dtype)

            @pl.when(jnp.logical_not(used))
            def _():
                o_ref[...] = jnp.zeros(o_ref.shape, o_ref.dtype)

        pltpu.emit_pipeline(
            body, grid=(D_MODEL // tn, n_blk), _explicit_indices=True,
            in_specs=[
                pl.BlockSpec((tm, D_FF), row_blk),
                pl.BlockSpec((None, None, D_FF, tn), lambda j, m: (layer, be_ref[m], 0, j), pipeline_mode=ahead),
                pl.BlockSpec((None, None, 1, tn), lambda j, m: (layer, be_ref[m], 0, j)),
            ],
            out_specs=[pl.BlockSpec((tm, tn), lambda j, m: (m, j))])(h_hbm, w_hbm, b_hbm, o_hbm)

    return pl.pallas_call(
        out_call,
        grid_spec=pltpu.PrefetchScalarGridSpec(num_scalar_prefetch=2, grid=(1,), in_specs=[hbm, hbm, hbm], out_specs=hbm),
        out_shape=jax.ShapeDtypeStruct((r, D_MODEL), jnp.bfloat16),
        compiler_params=_params("arbitrary"))(blk_expert, n_used, act, w_out, b_out4)


def moe(x, layer, w_router, b_router, w_in, b_in, w_out, b_out):
    n_tok = x.shape[0]
    n_asg = n_tok * TOP_K
    logits = jnp.dot(x.astype(jnp.float32), w_router[layer], precision=lax.Precision.HIGHEST) + b_router[layer]
    top_val, top_idx = lax.top_k(logits, TOP_K)
    gate = jax.nn.softmax(top_val, axis=-1)
    assert n_asg % MOE_ROW_BLK == 0
    onehot = (top_idx.reshape(-1, MOE_ROW_BLK, 1) == jnp.arange(N_EXPERTS)).astype(jnp.float32)
    tri = jnp.tril(jnp.ones((MOE_ROW_BLK, MOE_ROW_BLK), jnp.float32), -1)
    in_blk = jnp.einsum("ij,bje->bie", tri, onehot)
    blk_tot = jnp.sum(onehot, axis=1)
    before = jnp.cumsum(blk_tot, axis=0) - blk_tot
    rank = jnp.sum(onehot * (in_blk + before[:, None, :]), axis=-1).reshape(-1).astype(jnp.int32)
    counts = jnp.sum(blk_tot, axis=0).astype(jnp.int32)
    blocks = (counts + MOE_ROW_BLK - 1) // MOE_ROW_BLK
    blk_end = jnp.cumsum(blocks)
    dest = (blk_end - blocks)[top_idx.reshape(-1)] * MOE_ROW_BLK + rank
    n_blk = n_asg // MOE_ROW_BLK + N_EXPERTS
    row_tok = jnp.zeros((n_blk * MOE_ROW_BLK,), jnp.int32).at[dest].set(jnp.arange(n_asg, dtype=jnp.int32) // TOP_K)
    n_used = blk_end[-1:].astype(jnp.int32)
    blk_of = jnp.minimum(jnp.arange(n_blk, dtype=jnp.int32), n_used - 1)
    blk_expert = jnp.sum((blk_end[None, :] <= blk_of[:, None]).astype(jnp.int32), axis=1)
    xs = x[row_tok]
    ys = moe_ffn(xs, blk_expert, n_used, layer, w_in, b_in, w_out, b_out)
    dest = dest.reshape(n_tok, TOP_K)
    return [ys[dest[:, k]] for k in range(TOP_K)], gate


def split_cols(z, sizes):
    return jnp.split(z, np.cumsum(sizes)[:-1].tolist(), axis=-1)


def rms_norm(x, g):
    return x * lax.rsqrt(jnp.mean(jnp.square(x), axis=-1, keepdims=True) + NORM_EPS) * g


def l2_normalize(x):
    return x * lax.rsqrt(jnp.sum(jnp.square(x), axis=-1, keepdims=True) + NORM_EPS)


INT32_MIN = -2 ** 31
LANES = 128
PAGES_PER_STEP = 4
STEP_KEYS = PAGES_PER_STEP * PAGE_SIZE


def _sortable_key(x):
    bits = pltpu.bitcast(x, jnp.int32)
    return jnp.where(bits < 0, bits ^ 0x7FFFFFFF, bits)


def _kth_largest_key(key_ref, n_chunks, chunk, k, rows):
    def count_ge(cand):
        def body(c, acc):
            off = pl.multiple_of(c * chunk, chunk)
            for j in range(chunk // LANES):
                keys = key_ref[:, pl.ds(off + j * LANES, LANES)]
                acc = acc + jnp.where(keys >= cand, 1.0, 0.0)
            return acc

        acc = lax.fori_loop(0, n_chunks, body, jnp.zeros((rows, LANES), jnp.float32))
        return jnp.sum(acc, axis=-1, keepdims=True)

    def bit_step(b, ans):
        cand = ans + lax.shift_left(jnp.int32(1), 31 - b)
        return jnp.where(count_ge(cand) >= k, cand, ans)

    return lax.fori_loop(0, 32, bit_step, jnp.full((rows, 1), INT32_MIN, jnp.int32))


def _topk_bias(key_ref, bias_ref, thr, n_chunks, chunk, k, rows, row_pos):
    tri = (lax.broadcasted_iota(jnp.int32, (LANES, LANES), 0)
           <= lax.broadcasted_iota(jnp.int32, (LANES, LANES), 1)).astype(jnp.bfloat16)
    col = lax.broadcasted_iota(jnp.int32, (rows, LANES), 1)

    def count_gt(c, acc):
        off = pl.multiple_of(c * chunk, chunk)
        for j in range(chunk // LANES):
            acc = acc + jnp.where(key_ref[:, pl.ds(off + j * LANES, LANES)] > thr, 1.0, 0.0)
        return acc

    n_gt = jnp.sum(lax.fori_loop(0, n_chunks, count_gt, jnp.zeros((rows, LANES), jnp.float32)), axis=-1, keepdims=True)
    need = k - n_gt

    def body(c, run):
        off = pl.multiple_of(c * chunk, chunk)
        for j in range(chunk // LANES):
            keys = key_ref[:, pl.ds(off + j * LANES, LANES)]
            eq = keys == thr
            eq_f = jnp.where(eq, 1.0, 0.0)
            incl = jnp.dot(eq_f.astype(jnp.bfloat16), tri, preferred_element_type=jnp.float32)
            sel = (keys > thr) | (eq & (run + incl - eq_f < need))
            ok = sel & (off + j * LANES + col <= row_pos)
            bias_ref[:, pl.ds(off + j * LANES, LANES)] = jnp.where(ok, 0.0, NEG)
            run = run + incl[:, LANES - 1:LANES]
        return run

    lax.fori_loop(0, n_chunks, body, jnp.zeros((rows, 1), jnp.float32))


def _stack_heads(q_ref, g, rep):
    return jnp.concatenate([q_ref[0, :, (g * rep + r) * HEAD_DIM:(g * rep + r + 1) * HEAD_DIM] for r in range(rep)], axis=0)


def _flash_group(qg, kt_ref, v_ref, g, lo, hi, chunk, bias_fn):
    rows = qg.shape[0]

    def body(c, carry):
        m, l, acc = carry
        off = pl.multiple_of(c * chunk, chunk)
        kt = kt_ref[0, g * HEAD_DIM:(g + 1) * HEAD_DIM, pl.ds(off, chunk)]
        s = jnp.dot(qg, kt, preferred_element_type=jnp.float32) * ATTN_SCALE + bias_fn(off)
        m_new = jnp.maximum(m, jnp.max(s, axis=-1, keepdims=True))
        p = jnp.exp(s - m_new)
        alpha = jnp.exp(m - m_new)
        l = alpha * l + jnp.sum(p, axis=-1, keepdims=True)
        v = v_ref[0, pl.ds(off, chunk), g * HEAD_DIM:(g + 1) * HEAD_DIM]
        acc = alpha * acc + jnp.dot(_bf16(p), v, preferred_element_type=jnp.float32)
        return m_new, l, acc

    init = (jnp.full((rows, 1), NEG, jnp.float32), jnp.zeros((rows, 1), jnp.float32),
            jnp.zeros((rows, HEAD_DIM), jnp.float32))
    _, l, acc = lax.fori_loop(lo, hi, body, init)
    return acc / l


DSA_TQ = 128
DSA_TK = 512


def _dsa_prompt_kernel(q_ref, qi_ref, wi_ref, kit_ref, kt_ref, v_ref, o_ref, key_ref, bias_ref, *, topk):
    tq, tk = DSA_TQ, DSA_TK
    q_start = pl.program_id(1) * tq
    n_chunks = (q_start + tq + tk - 1) // tk
    row_pos = q_start + lax.broadcasted_iota(jnp.int32, (tq, tk), 0)
    col = lax.broadcasted_iota(jnp.int32, (tq, tk), 1)
    wi = wi_ref[0]

    def score_chunk(c, carry):
        off = pl.multiple_of(c * tk, tk)
        kit = kit_ref[0, :, pl.ds(off, tk)]
        acc = jnp.zeros((tq, tk), jnp.float32)
        for h in range(A_IDX_HEADS):
            s = jnp.dot(qi_ref[0, h], kit, preferred_element_type=jnp.float32)
            acc = acc + jnp.maximum(s, 0.0) * wi[:, h:h + 1]
        key_ref[:, pl.ds(off, tk)] = _sortable_key(jnp.where(off + col <= row_pos, acc, NEG))
        return carry

    lax.fori_loop(0, n_chunks, score_chunk, 0)
    thr = _kth_largest_key(key_ref, n_chunks, tk, topk, tq)
    _topk_bias(key_ref, bias_ref, thr, n_chunks, tk, topk, tq, row_pos[:, :LANES])
    bias_fn = lambda off: jnp.concatenate([bias_ref[:, pl.ds(off, tk)]] * A_REP, axis=0)
    for g in range(A_KV):
        out = _flash_group(_stack_heads(q_ref, g, A_REP), kt_ref, v_ref, g, 0, n_chunks, tk, bias_fn)
        for r in range(A_REP):
            h = g * A_REP + r
            o_ref[0, :, h * HEAD_DIM:(h + 1) * HEAD_DIM] = out[r * tq:(r + 1) * tq].astype(o_ref.dtype)


def dsa_prompt(z):
    B, T, _ = z.shape
    topk = min(A_TOPK, T // 4)
    nq, nk = A_HEADS * HEAD_DIM, A_KV * HEAD_DIM
    q = _bf16(z[..., :nq])
    k, v = z[..., nq:nq + nk], z[..., nq + nk:nq + 2 * nk]
    off = nq + 2 * nk
    qi = _bf16(z[..., off:off + A_IDX_HEADS * A_IDX_DIM]).reshape(B, T, A_IDX_HEADS, A_IDX_DIM).transpose(0, 2, 1, 3)
    off += A_IDX_HEADS * A_IDX_DIM
    ki = z[..., off:off + A_IDX_DIM]
    wi = z[..., off + A_IDX_DIM:] * (A_IDX_HEADS ** -0.5 * A_IDX_DIM ** -0.5)
    kit = _bf16(ki).transpose(0, 2, 1)
    kt = _bf16(k).transpose(0, 2, 1)
    tq = DSA_TQ
    o = pl.pallas_call(
        functools.partial(_dsa_prompt_kernel, topk=topk),
        grid=(B, T // tq),
        in_specs=[
            pl.BlockSpec((1, tq, nq), lambda b, i: (b, i, 0)),
            pl.BlockSpec((1, A_IDX_HEADS, tq, A_IDX_DIM), lambda b, i: (b, 0, i, 0)),
            pl.BlockSpec((1, tq, A_IDX_HEADS), lambda b, i: (b, i, 0)),
            pl.BlockSpec((1, A_IDX_DIM, T), lambda b, i: (b, 0, 0)),
            pl.BlockSpec((1, nk, T), lambda b, i: (b, 0, 0)),
            pl.BlockSpec((1, T, nk), lambda b, i: (b, 0, 0)),
        ],
        out_specs=pl.BlockSpec((1, tq, nq), lambda b, i: (b, i, 0)),
        out_shape=jax.ShapeDtypeStruct((B, T, nq), jnp.bfloat16),
        scratch_shapes=[pltpu.VMEM((tq, T), jnp.int32), pltpu.VMEM((tq, T), jnp.float32)],
        compiler_params=_params("parallel", "arbitrary"))(q, qi, wi, kit, kt, _bf16(v))
    kv = jnp.stack([k.reshape(B, T, A_KV, HEAD_DIM), v.reshape(B, T, A_KV, HEAD_DIM)], axis=2)
    return o, kv, ki


def _nt_dot(a, b):
    return lax.dot_general(a, b, (((1,), (1,)), ((), ())), preferred_element_type=jnp.float32)


def _dsa_sample_score_kernel(pt_ref, qi_ref, wi_ref, *rest, layer, n_pages, t_new):
    pool_refs, (new_ref, o_ref) = rest[:PAGES_PER_STEP], rest[PAGES_PER_STEP:]
    p = pl.program_id(1)

    def scores(ki):
        s = jnp.maximum(_nt_dot(qi_ref[0], _bf16(ki)), 0.0) * wi_ref[0]
        acc = s[0:t_new]
        for h in range(1, A_IDX_HEADS):
            acc = acc + s[h * t_new:(h + 1) * t_new]
        k_pos = p * STEP_KEYS + lax.broadcasted_iota(jnp.int32, (t_new, STEP_KEYS), 1)
        q_pos = n_pages * PAGE_SIZE + lax.broadcasted_iota(jnp.int32, (t_new, STEP_KEYS), 0)
        o_ref[0] = jnp.where(k_pos <= q_pos, acc, NEG)

    @pl.when(p < n_pages // PAGES_PER_STEP)
    def _():
        scores(jnp.concatenate([r[:, layer, :] for r in pool_refs], axis=0))

    @pl.when(p == n_pages // PAGES_PER_STEP)
    def _():
        scores(new_ref[0])


def _dsa_sample_select_kernel(s_ref, bias_ref, key_ref, *, topk, chunk, q_pos0):
    rows, n_keys = key_ref.shape
    n_chunks = n_keys // chunk
    for c in range(n_chunks):
        key_ref[:, c * chunk:(c + 1) * chunk] = _sortable_key(s_ref[0, :, c * chunk:(c + 1) * chunk])
    thr = _kth_largest_key(key_ref, n_chunks, chunk, topk, rows)
    q_pos = q_pos0 + lax.broadcasted_iota(jnp.int32, (rows, LANES), 0)
    _topk_bias(key_ref, bias_ref.at[0], thr, n_chunks, chunk, topk, rows, q_pos)


def _dsa_sample_attend_kernel(pt_ref, q_ref, bias_ref, *rest, n_pages, t_new):
    pool_refs, (new_ref, o_ref, m_ref, l_ref, acc_ref) = rest[:PAGES_PER_STEP], rest[PAGES_PER_STEP:]
    p = pl.program_id(1)

    @pl.when(p == 0)
    def _():
        _init_online_softmax(m_ref, l_ref, acc_ref)

    def attend(kv_of):
        bias = jnp.concatenate([bias_ref[0]] * A_REP, axis=0)
        _paged_attend_step(q_ref, kv_of, lambda g: bias, m_ref, l_ref, acc_ref, A_KV, A_REP * t_new)

    @pl.when(p < n_pages // PAGES_PER_STEP)
    def _():
        attend(lambda e, g: jnp.concatenate([r[:, e, g, :] for r in pool_refs], axis=0))

    @pl.when(p == n_pages // PAGES_PER_STEP)
    def _():
        attend(lambda e, g: new_ref[0, :, e, g, :])
        o_ref[0] = (acc_ref[...] / l_ref[...]).astype(o_ref.dtype)


def dsa_sample(z, pool_kv, pool_idx, page_table, layer):
    B, T, _ = z.shape
    n_pages = page_table.shape[1]
    past = n_pages * PAGE_SIZE
    topk = min(A_TOPK, (past + T) // 4)
    nq, nk = A_HEADS * HEAD_DIM, A_KV * HEAD_DIM
    n_layers = pool_kv.shape[2]
    k, v = z[..., nq:nq + nk], z[..., nq + nk:nq + 2 * nk]
    off = nq + 2 * nk
    head_major = lambda a, d: a.reshape(B, T, -1, d).transpose(0, 2, 1, 3).reshape(B, -1, d)
    q = head_major(_bf16(z[..., :nq]), HEAD_DIM)
    qi = head_major(_bf16(z[..., off:off + A_IDX_HEADS * A_IDX_DIM]), A_IDX_DIM)
    off += A_IDX_HEADS * A_IDX_DIM
    ki = z[..., off:off + A_IDX_DIM]
    wi = head_major(z[..., off + A_IDX_DIM:] * (A_IDX_HEADS ** -0.5 * A_IDX_DIM ** -0.5), 1)
    assert n_pages % PAGES_PER_STEP == 0 and T <= STEP_KEYS
    n_steps = n_pages // PAGES_PER_STEP + 1
    n_keys = n_steps * STEP_KEYS
    pad_rows = lambda a: jnp.pad(a, ((0, 0), (0, STEP_KEYS - T)) + ((0, 0),) * (a.ndim - 2))
    page_of = lambda b, p, pt, i: pt[b, jnp.minimum(p * PAGES_PER_STEP + i, n_pages - 1)]
    per_b = lambda b, p, pt: (b, 0, 0)
    scores = pl.pallas_call(
        functools.partial(_dsa_sample_score_kernel, layer=layer, n_pages=n_pages, t_new=T),
        grid_spec=pltpu.PrefetchScalarGridSpec(
            num_scalar_prefetch=1, grid=(B, n_steps),
            in_specs=[pl.BlockSpec((1, A_IDX_HEADS * T, A_IDX_DIM), per_b),
                      pl.BlockSpec((1, A_IDX_HEADS * T, 1), per_b)]
            + [pl.BlockSpec((None, PAGE_SIZE, n_layers, A_IDX_DIM), lambda b, p, pt, i=i: (page_of(b, p, pt, i), 0, 0, 0))
               for i in range(PAGES_PER_STEP)]
            + [pl.BlockSpec((1, STEP_KEYS, A_IDX_DIM), per_b)],
            out_specs=pl.BlockSpec((1, T, STEP_KEYS), lambda b, p, pt: (b, 0, p))),
        out_shape=jax.ShapeDtypeStruct((B, T, n_keys), jnp.float32),
        compiler_params=_params("parallel", "arbitrary"))(page_table, qi, wi, *[pool_idx] * PAGES_PER_STEP, pad_rows(ki))
    chunk = STEP_KEYS
    bias = pl.pallas_call(
        functools.partial(_dsa_sample_select_kernel, topk=topk, chunk=chunk, q_pos0=past),
        grid=(B,),
        in_specs=[pl.BlockSpec((1, T, n_keys), lambda b: (b, 0, 0))],
        out_specs=pl.BlockSpec((1, T, n_keys), lambda b: (b, 0, 0)),
        out_shape=jax.ShapeDtypeStruct((B, T, n_keys), jnp.float32),
        scratch_shapes=[pltpu.VMEM((T, n_keys), jnp.int32)],
        compiler_params=_params("parallel"))(scores)
    kv_new = jnp.stack([k.reshape(B, T, A_KV, HEAD_DIM), v.reshape(B, T, A_KV, HEAD_DIM)], axis=2)
    o = pl.pallas_call(
        functools.partial(_dsa_sample_attend_kernel, n_pages=n_pages, t_new=T),
        grid_spec=pltpu.PrefetchScalarGridSpec(
            num_scalar_prefetch=1, grid=(B, n_steps),
            in_specs=[pl.BlockSpec((1, A_HEADS * T, HEAD_DIM), per_b),
                      pl.BlockSpec((1, T, STEP_KEYS), lambda b, p, pt: (b, 0, p))]
            + [pl.BlockSpec((None, PAGE_SIZE, None, 2, A_KV, HEAD_DIM),
                            lambda b, p, pt, i=i: (page_of(b, p, pt, i), 0, layer, 0, 0, 0)) for i in range(PAGES_PER_STEP)]
            + [pl.BlockSpec((1, STEP_KEYS, 2, A_KV, HEAD_DIM), lambda b, p, pt: (b, 0, 0, 0, 0))],
            out_specs=pl.BlockSpec((1, A_HEADS * T, HEAD_DIM), per_b),
            scratch_shapes=[pltpu.VMEM((A_HEADS * T, 1), jnp.float32), pltpu.VMEM((A_HEADS * T, 1), jnp.float32),
                            pltpu.VMEM((A_HEADS * T, HEAD_DIM), jnp.float32)]),
        out_shape=jax.ShapeDtypeStruct((B, A_HEADS * T, HEAD_DIM), jnp.bfloat16),
        compiler_params=_params("parallel", "arbitrary"))(
            page_table, q, bias, *[pool_kv] * PAGES_PER_STEP, pad_rows(kv_new))
    o = o.reshape(B, A_HEADS, T, HEAD_DIM).transpose(0, 2, 1, 3).reshape(B, T, nq)
    return o, kv_new, ki


TRI_PACK = 4
TRI_GROUPS = 4


def _dot_bf16x3(a, b):
    a_hi, b_hi = _bf16(a), _bf16(b)
    a_lo, b_lo = _bf16(a - a_hi.astype(jnp.float32)), _bf16(b - b_hi.astype(jnp.float32))
    dot = functools.partial(jnp.dot, preferred_element_type=jnp.float32)
    return dot(a_hi, b_hi) + (dot(a_hi, b_lo) + dot(a_lo, b_hi))


def _unit_lower_inverse_kernel(a_ref, o_ref):
    c = a_ref.shape[-1]
    size = TRI_PACK * c
    eye = jnp.where(lax.broadcasted_iota(jnp.int32, (size, size), 0) == lax.broadcasted_iota(jnp.int32, (size, size), 1), 1.0, 0.0)
    zero = jnp.zeros((c, c), jnp.float32)
    for i in range(TRI_GROUPS):
        n = jnp.concatenate([jnp.concatenate([-a_ref[i, j] if jj == j else zero for jj in range(TRI_PACK)], axis=1)
                             for j in range(TRI_PACK)], axis=0)
        t, p = eye + n, n
        for _ in range((c - 1).bit_length() - 1):
            p = _dot_bf16x3(p, p)
            t = t + _dot_bf16x3(t, p)
        for j in range(TRI_PACK):
            o_ref[i, j] = t[j * c:(j + 1) * c, j * c:(j + 1) * c]


def unit_lower_inverse(a):
    c = a.shape[-1]
    n = math.prod(a.shape[:-2])
    per_step = TRI_GROUPS * TRI_PACK
    assert n % per_step == 0
    blk = (TRI_GROUPS, TRI_PACK, c, c)
    out = pl.pallas_call(
        _unit_lower_inverse_kernel, grid=(n // per_step,),
        in_specs=[pl.BlockSpec(blk, lambda i: (i, 0, 0, 0))],
        out_specs=pl.BlockSpec(blk, lambda i: (i, 0, 0, 0)),
        out_shape=jax.ShapeDtypeStruct((n // TRI_PACK, TRI_PACK, c, c), jnp.float32),
        compiler_params=_params("parallel"))(a.reshape(n // TRI_PACK, TRI_PACK, c, c))
    return out.reshape(a.shape)
def chunk_gated_delta(q, k, v, g, beta, s0):
    B, T, H, DK = q.shape
    C = B_CHUNK
    n = -(-T // C)
    pad = n * C - T

    def chunks(a):
        a = jnp.pad(a, [(0, 0), (0, pad)] + [(0, 0)] * (a.ndim - 2))
        a = a.reshape(B, n, C, *a.shape[2:])
        return jnp.moveaxis(a, (1, 3), (0, 2))

    qc, kc, vc, bc = chunks(q) * DK ** -0.5, chunks(k), chunks(v), chunks(beta)
    gc = jnp.cumsum(chunks(g), axis=-1)
    incl = jnp.tril(jnp.ones((C, C), bool))
    strict = jnp.tril(jnp.ones((C, C), bool), -1)
    decay = jnp.exp(jnp.where(incl, gc[..., :, None] - gc[..., None, :], NEG))
    kb = kc * bc[..., None]
    a_mat = jnp.where(strict, jnp.einsum("...id,...jd->...ij", kb, kc) * decay, 0.0)
    t_mat = unit_lower_inverse(a_mat)
    u = t_mat @ (vc * bc[..., None])
    w = t_mat @ (kb * jnp.exp(gc)[..., None])
    attn = jnp.einsum("...id,...jd->...ij", qc, kc) * decay

    def step(s, inp):
        q_i, k_i, u_i, w_i, g_i, attn_i = inp
        v_new = u_i - w_i @ s
        o = (q_i * jnp.exp(g_i)[..., None]) @ s + attn_i @ v_new
        g_last = g_i[..., -1]
        s = s * jnp.exp(g_last)[..., None, None] + jnp.einsum(
            "bhcd,bhce->bhde", k_i * jnp.exp(g_last[..., None] - g_i)[..., None], v_new)
        return s, o

    s_final, o = lax.scan(step, s0, (qc, kc, u, w, gc, attn))
    o = jnp.moveaxis(o, (0, 2), (1, 3)).reshape(B, n * C, H, -1)[:, :T]
    return o, s_final


def gdn_mixer(zin, conv_w, a_log, dt_bias, norm_g, conv_state, rec_state):
    B, T, _ = zin.shape
    qkv, z, b_raw, a_raw = split_cols(zin, [B_CONV_CH, B_V_HEADS * B_DV, B_V_HEADS, B_V_HEADS])
    x_ext = jnp.concatenate([conv_state, qkv], axis=1)
    conv = sum(x_ext[:, i:i + T] * conv_w[i][None, None, :] for i in range(B_CONV))
    q, k, v = split_cols(jax.nn.silu(conv), [B_QK_HEADS * B_DK, B_QK_HEADS * B_DK, B_V_HEADS * B_DV])
    rep = B_V_HEADS // B_QK_HEADS
    q = jnp.repeat(l2_normalize(q.reshape(B, T, B_QK_HEADS, B_DK)), rep, axis=2)
    k = jnp.repeat(l2_normalize(k.reshape(B, T, B_QK_HEADS, B_DK)), rep, axis=2)
    v = v.reshape(B, T, B_V_HEADS, B_DV)
    beta = jax.nn.sigmoid(b_raw)
    g = -jnp.exp(a_log) * jax.nn.softplus(a_raw + dt_bias)
    o, s_new = chunk_gated_delta(q, k, v, g, beta, rec_state)
    o = rms_norm(o, norm_g) * jax.nn.silu(z.reshape(B, T, B_V_HEADS, B_DV))
    return o.reshape(B, T, B_V_HEADS * B_DV), x_ext[:, -(B_CONV - 1):], s_new


def nsa_compress(kv, n_cmp, pe, w1, w2, layer):
    B = kv.shape[0]
    n_row = n_cmp + 1
    x = kv[:, :n_row * CMP_STRIDE].reshape(B, n_row, CMP_STRIDE, 2, C_KV, HEAD_DIM)
    return nsa_compress_rows(x.transpose(3, 0, 4, 1, 2, 5).reshape(2, B, C_KV, n_row, CMP_STRIDE * HEAD_DIM),
                             n_cmp, pe, w1, w2, layer)


def nsa_compress_rows(x, n_cmp, pe, w1, w2, layer):
    _, B, _, n_row, half = x.shape
    x = x.reshape(2, B * C_KV * n_row, half)
    w1l = w1[layer]
    w_halves = jnp.concatenate([w1l[:, :half], w1l[:, half:]], axis=-1)
    pe_rows = jnp.broadcast_to(pe[layer].transpose(1, 0, 2).reshape(2, 1, CMP_BLK * HEAD_DIM), (2, 8, CMP_BLK * HEAD_DIM))
    outs = []
    for e in range(2):
        ab = matmul(x[e], w_halves, e).reshape(B, C_KV, n_row, 2, HEAD_DIM)
        c = matmul(pe_rows[e], w1l, e)[0]
        hid = jax.nn.gelu(ab[:, :, :n_cmp, 0] + ab[:, :, 1:n_cmp + 1, 1] + c)
        outs.append(matmul(hid.reshape(B * C_KV * n_cmp, HEAD_DIM), w2[layer], e).reshape(B, C_KV, n_cmp, HEAD_DIM))
    return outs


NSA_TQ = 128
NSA_TK = 512
SLC_SHIFT = SLC_BLK.bit_length() - 1


def _nsa_prompt_kernel(q_ref, gl_ref, kct_ref, vc_ref, kst_ref, vs_ref, kwt_ref, vw_ref, o_ref, key_ref, bias_ref, *,
                       n_cmp, n_slc):
    tq, tk = NSA_TQ, NSA_TK
    rows = C_REP * tq
    q_start = pl.program_id(1) * tq
    n_chunks = (q_start + tq + tk - 1) // tk
    win_lo = jnp.maximum(q_start - (WINDOW - 1), 0) // tk
    n_pad = kct_ref.shape[2]
    iota = lambda shape, ax: lax.broadcasted_iota(jnp.int32, shape, ax)
    tile = lambda a: jnp.concatenate([a] * C_REP, axis=0)

    n_c = iota((tq, n_pad), 1)
    vis = tile((n_c * CMP_STRIDE + CMP_BLK - 1 <= q_start + iota((tq, n_pad), 0)) & (n_c < n_cmp))
    n_o, m_o = iota((n_pad, LANES), 0) * CMP_STRIDE, iota((n_pad, LANES), 1) * SLC_BLK
    overlap = jnp.where((n_o < m_o + SLC_BLK) & (n_o + CMP_BLK > m_o) & (iota((n_pad, LANES), 1) < n_slc), 1.0, 0.0)
    cur = jnp.right_shift(q_start + iota((tq, LANES), 0), SLC_SHIFT)
    blk = iota((tq, LANES), 1)
    o_cmp = []
    for g in range(C_KV):
        qg = _stack_heads(q_ref, g, C_REP)
        s = jnp.dot(qg, kct_ref[0, g * HEAD_DIM:(g + 1) * HEAD_DIM, :], preferred_element_type=jnp.float32) * ATTN_SCALE
        s = jnp.where(vis, s, NEG)
        e = jnp.exp(s - jnp.max(s, axis=-1, keepdims=True))
        p = jnp.where(vis, e / jnp.sum(e, axis=-1, keepdims=True), 0.0)
        o_cmp.append(jnp.dot(_bf16(p), vc_ref[0, :, g * HEAD_DIM:(g + 1) * HEAD_DIM], preferred_element_type=jnp.float32))
        p_grp = sum(p[r * tq:(r + 1) * tq] for r in range(C_REP))
        imp = jnp.dot(p_grp, overlap, preferred_element_type=jnp.float32, precision=lax.Precision.HIGHEST)
        imp = jnp.where((blk == 0) | (blk == cur), FORCE, imp)
        key_ref[g * tq:(g + 1) * tq, :] = _sortable_key(jnp.where(blk <= cur, imp, NEG))
    k_sel = min(N_SEL, n_slc)
    thr = _kth_largest_key(key_ref, 1, LANES, k_sel, rows)
    _topk_bias(key_ref, bias_ref, thr, 1, LANES, k_sel, rows, tile(cur))

    t_k = q_start + iota((tq, tk), 0)
    s_k = iota((tq, tk), 1)
    blk_e, s_e = iota((LANES, tk), 0), iota((LANES, tk), 1)
    gate = jax.nn.sigmoid(gl_ref[0])

    def win_bias(off):
        d = t_k - (off + s_k)
        return tile(jnp.where((d >= 0) & (d < WINDOW), 0.0, NEG))

    for g in range(C_KV):
        qg = _stack_heads(q_ref, g, C_REP)
        sel = _bf16(jnp.where(bias_ref[g * tq:(g + 1) * tq, :] == 0.0, 1.0, 0.0))

        def slc_bias(off, sel=sel):
            expand = _bf16(jnp.where(blk_e == jnp.right_shift(off + s_e, SLC_SHIFT), 1.0, 0.0))
            hit = jnp.dot(sel, expand, preferred_element_type=jnp.float32)
            return tile(jnp.where((hit > 0.5) & (off + s_k <= t_k), 0.0, NEG))

        o_slc = _flash_group(qg, kst_ref, vs_ref, g, 0, n_chunks, tk, slc_bias)
        o_win = _flash_group(qg, kwt_ref, vw_ref, g, win_lo, n_chunks, tk, win_bias)
        for r in range(C_REP):
            h = g * C_REP + r
            rs = slice(r * tq, (r + 1) * tq)
            o = (gate[:, 3 * h:3 * h + 1] * o_cmp[g][rs] + gate[:, 3 * h + 1:3 * h + 2] * o_slc[rs]
                 + gate[:, 3 * h + 2:3 * h + 3] * o_win[rs])
            o_ref[0, :, h * HEAD_DIM:(h + 1) * HEAD_DIM] = o.astype(o_ref.dtype)


def nsa_prompt(z, pe, w1, w2, layer):
    B, T, _ = z.shape
    nq, nk = C_HEADS * HEAD_DIM, C_KV * HEAD_DIM
    q = _bf16(z[..., :nq])
    branch = [z[..., nq + 2 * i * nk:nq + 2 * (i + 1) * nk] for i in range(3)]
    gl = z[..., nq + 6 * nk:]
    as_kv = lambda a: a.reshape(B, T, 2, C_KV, HEAD_DIM)
    n_cmp = (T - CMP_BLK) // CMP_STRIDE + 1
    k_cmp, v_cmp = nsa_compress(as_kv(branch[0]), n_cmp, pe, w1, w2, layer)
    n_pad = -(-n_cmp // LANES) * LANES
    pad = ((0, 0), (0, 0), (0, n_pad - n_cmp), (0, 0))
    kct = _bf16(jnp.pad(k_cmp, pad)).transpose(0, 1, 3, 2).reshape(B, nk, n_pad)
    vc = _bf16(jnp.pad(v_cmp, pad)).transpose(0, 2, 1, 3).reshape(B, n_pad, nk)
    kt = lambda a: _bf16(a[..., :nk]).transpose(0, 2, 1)
    tq = NSA_TQ
    full_t = lambda shape: pl.BlockSpec(shape, lambda b, i: (b, 0, 0))
    o = pl.pallas_call(
        functools.partial(_nsa_prompt_kernel, n_cmp=n_cmp, n_slc=-(-T // SLC_BLK)),
        grid=(B, T // tq),
        in_specs=[
            pl.BlockSpec((1, tq, nq), lambda b, i: (b, i, 0)),
            pl.BlockSpec((1, tq, 3 * C_HEADS), lambda b, i: (b, i, 0)),
            full_t((1, nk, n_pad)), full_t((1, n_pad, nk)),
            full_t((1, nk, T)), full_t((1, T, nk)), full_t((1, nk, T)), full_t((1, T, nk)),
        ],
        out_specs=pl.BlockSpec((1, tq, nq), lambda b, i: (b, i, 0)),
        out_shape=jax.ShapeDtypeStruct((B, T, nq), jnp.bfloat16),
        scratch_shapes=[pltpu.VMEM((C_REP * tq, LANES), jnp.int32), pltpu.VMEM((C_REP * tq, LANES), jnp.float32)],
        compiler_params=_params("parallel", "arbitrary"))(
            q, gl, kct, vc, kt(branch[1]), _bf16(branch[1][..., nk:]), kt(branch[2]), _bf16(branch[2][..., nk:]))
    return o, as_kv(branch[0]), as_kv(branch[1]), as_kv(branch[2])[:, -WINDOW:]


def _page_copy_kernel(pt_ref, pool_ref, o_ref):
    o_ref[0] = pool_ref[...]


def gather_pages_pallas(pool, page_table, layer):
    B, n_pages = page_table.shape
    row = pool.shape[3:]
    return pl.pallas_call(
        _page_copy_kernel,
        grid_spec=pltpu.PrefetchScalarGridSpec(
            num_scalar_prefetch=1, grid=(B, n_pages),
            in_specs=[pl.BlockSpec((None, PAGE_SIZE, None) + row, lambda b, p, pt: (pt[b, p], 0, layer, 0, 0, 0))],
            out_specs=pl.BlockSpec((1, PAGE_SIZE) + row, lambda b, p, pt: (b, p, 0, 0, 0))),
        out_shape=jax.ShapeDtypeStruct((B, n_pages * PAGE_SIZE) + row, pool.dtype),
        compiler_params=_params("parallel", "arbitrary"))(page_table, pool)


def _page_rowgroup_kernel(pt_ref, *refs):
    pool_refs, o_ref = refs[:PAGES_PER_STEP], refs[PAGES_PER_STEP]
    n_row = PAGE_SIZE // CMP_STRIDE
    for e in range(2):
        for g in range(C_KV):
            rows = [jnp.concatenate([r[pl.ds(j, n_row, stride=CMP_STRIDE), e, g, :] for j in range(CMP_STRIDE)], axis=1)
                    for r in pool_refs]
            o_ref[e, 0, g] = jnp.concatenate(rows, axis=0).astype(o_ref.dtype)


def gather_page_rowgroups(pool, page_table, layer):
    B, n_pages = page_table.shape
    assert n_pages % PAGES_PER_STEP == 0
    n_row = PAGE_SIZE // CMP_STRIDE
    blk = (2, 1, C_KV, PAGES_PER_STEP * n_row, CMP_STRIDE * HEAD_DIM)
    return pl.pallas_call(
        _page_rowgroup_kernel,
        grid_spec=pltpu.PrefetchScalarGridSpec(
            num_scalar_prefetch=1, grid=(B, n_pages // PAGES_PER_STEP),
            in_specs=[pl.BlockSpec((None, PAGE_SIZE, None, 2, C_KV, HEAD_DIM),
                                   lambda b, p, pt, i=i: (pt[b, p * PAGES_PER_STEP + i], 0, layer, 0, 0, 0))
                      for i in range(PAGES_PER_STEP)],
            out_specs=pl.BlockSpec(blk, lambda b, p, pt: (0, b, 0, p, 0))),
        out_shape=jax.ShapeDtypeStruct((2, B, C_KV, n_pages * n_row, CMP_STRIDE * HEAD_DIM), jnp.bfloat16),
        compiler_params=_params("parallel", "arbitrary"))(page_table, *[pool] * PAGES_PER_STEP)


def _nsa_sample_cmpwin_kernel(q_ref, kct_ref, vc_ref, win_ref, ocmp_ref, owin_ref, sel_ref, key_ref, *,
                              n_cmp, n_slc, q_pos0, win_pos0, t_new):
    rows_g = C_REP * t_new
    nk = C_KV * HEAD_DIM
    n_pad, n_blk, n_win = kct_ref.shape[2], key_ref.shape[1], win_ref.shape[1]
    iota = lambda shape, ax: lax.broadcasted_iota(jnp.int32, shape, ax)
    tile = lambda a: jnp.concatenate([a] * C_REP, axis=0)
    n_c = iota((t_new, n_pad), 1)
    vis = tile((n_c * CMP_STRIDE + CMP_BLK - 1 <= q_pos0 + iota((t_new, n_pad), 0)) & (n_c < n_cmp))
    n_o, m_o = iota((n_pad, n_blk), 0) * CMP_STRIDE, iota((n_pad, n_blk), 1) * SLC_BLK
    overlap = jnp.where((n_o < m_o + SLC_BLK) & (n_o + CMP_BLK > m_o) & (iota((n_pad, n_blk), 1) < n_slc), 1.0, 0.0)
    cur = jnp.right_shift(q_pos0 + iota((t_new, n_blk), 0), SLC_SHIFT)
    blk = iota((t_new, n_blk), 1)
    d = q_pos0 + iota((t_new, n_win), 0) - (win_pos0 + iota((t_new, n_win), 1))
    win_ok = tile((win_pos0 + iota((t_new, n_win), 1) >= 0) & (d >= 0) & (d < WINDOW))
    for g in range(C_KV):
        rs = slice(g * rows_g, (g + 1) * rows_g)
        qg = q_ref[0, rs]
        s = jnp.dot(qg, kct_ref[0, g * HEAD_DIM:(g + 1) * HEAD_DIM, :], preferred_element_type=jnp.float32) * ATTN_SCALE
        s = jnp.where(vis, s, NEG)
        e = jnp.exp(s - jnp.max(s, axis=-1, keepdims=True))
        p = jnp.where(vis, e / jnp.sum(e, axis=-1, keepdims=True), 0.0)
        ocmp_ref[0, rs] = jnp.dot(_bf16(p), vc_ref[0, :, g * HEAD_DIM:(g + 1) * HEAD_DIM], preferred_element_type=jnp.float32)
        p_grp = sum(p[r * t_new:(r + 1) * t_new] for r in range(C_REP))
        imp = jnp.dot(p_grp, overlap, preferred_element_type=jnp.float32, precision=lax.Precision.HIGHEST)
        imp = jnp.where((blk == 0) | (blk == cur), FORCE, imp)
        key_ref[g * t_new:(g + 1) * t_new, :] = _sortable_key(jnp.where(blk <= cur, imp, NEG))
        kw = _bf16(win_ref[0, :, g * HEAD_DIM:(g + 1) * HEAD_DIM])
        vw = _bf16(win_ref[0, :, nk + g * HEAD_DIM:nk + (g + 1) * HEAD_DIM])
        sw = jnp.where(win_ok, _nt_dot(qg, kw) * ATTN_SCALE, NEG)
        ew = jnp.exp(sw - jnp.max(sw, axis=-1, keepdims=True))
        pw = ew / jnp.sum(ew, axis=-1, keepdims=True)
        owin_ref[0, rs] = jnp.dot(_bf16(pw), vw, preferred_element_type=jnp.float32)
    k_sel = min(N_SEL, n_slc)
    rows = C_KV * t_new
    thr = _kth_largest_key(key_ref, 1, n_blk, k_sel, rows)
    _topk_bias(key_ref, sel_ref.at[0], thr, 1, n_blk, k_sel, rows, jnp.concatenate([cur[:, :LANES]] * C_KV, axis=0))


def _paged_attend_step(q_ref, kv_of, bias_of_group, m_ref, l_ref, acc_ref, n_groups, rows_g):
    for g in range(n_groups):
        rs = slice(g * rows_g, (g + 1) * rows_g)
        s = _nt_dot(q_ref[0, rs], _bf16(kv_of(0, g))) * ATTN_SCALE + bias_of_group(g)
        m = m_ref[rs]
        m_new = jnp.maximum(m, jnp.max(s, axis=-1, keepdims=True))
        pr = jnp.exp(s - m_new)
        alpha = jnp.exp(m - m_new)
        l_ref[rs] = alpha * l_ref[rs] + jnp.sum(pr, axis=-1, keepdims=True)
        acc_ref[rs] = alpha * acc_ref[rs] + jnp.dot(_bf16(pr), _bf16(kv_of(1, g)), preferred_element_type=jnp.float32)
        m_ref[rs] = m_new


def _init_online_softmax(m_ref, l_ref, acc_ref):
    m_ref[...] = jnp.full(m_ref.shape, NEG, jnp.float32)
    l_ref[...] = jnp.zeros(l_ref.shape, jnp.float32)
    acc_ref[...] = jnp.zeros(acc_ref.shape, jnp.float32)


def _nsa_sample_slc_kernel(pt_ref, q_ref, sel_ref, gate_ref, ocmp_ref, owin_ref, *rest, n_pages, t_new):
    pool_refs, (new_ref, o_ref, m_ref, l_ref, acc_ref) = rest[:PAGES_PER_STEP], rest[PAGES_PER_STEP:]
    p = pl.program_id(1)
    n_blk = sel_ref.shape[2]

    @pl.when(p == 0)
    def _():
        _init_online_softmax(m_ref, l_ref, acc_ref)

    def attend(kv_of):
        iota = lambda shape, ax: lax.broadcasted_iota(jnp.int32, shape, ax)
        causal = p * STEP_KEYS + iota((t_new, STEP_KEYS), 1) <= n_pages * PAGE_SIZE + iota((t_new, STEP_KEYS), 0)
        key_blk = (STEP_KEYS // SLC_BLK) * p + jnp.right_shift(iota((n_blk, STEP_KEYS), 1), SLC_SHIFT)
        expand = _bf16(jnp.where(iota((n_blk, STEP_KEYS), 0) == key_blk, 1.0, 0.0))

        def bias_of_group(g):
            sel = _bf16(jnp.where(sel_ref[0, g * t_new:(g + 1) * t_new, :] == 0.0, 1.0, 0.0))
            hit = jnp.dot(sel, expand, preferred_element_type=jnp.float32)
            return jnp.concatenate([jnp.where((hit > 0.5) & causal, 0.0, NEG)] * C_REP, axis=0)

        _paged_attend_step(q_ref, kv_of, bias_of_group, m_ref, l_ref, acc_ref, C_KV, C_REP * t_new)

    @pl.when(p < n_pages // PAGES_PER_STEP)
    def _():
        attend(lambda e, g: jnp.concatenate([r[:, e, g, :] for r in pool_refs], axis=0))

    @pl.when(p == n_pages // PAGES_PER_STEP)
    def _():
        attend(lambda e, g: new_ref[0, :, e, g, :])
        gate = jax.nn.sigmoid(gate_ref[0])
        o = gate[:, 0:1] * ocmp_ref[0] + gate[:, 1:2] * (acc_ref[...] / l_ref[...]) + gate[:, 2:3] * owin_ref[0]
        o_ref[0] = o.astype(o_ref.dtype)


def nsa_sample(z, pe, w1, w2, pool_cmp, pool_slc, win_buf, page_table, layer):
    B, T, _ = z.shape
    n_pages = page_table.shape[1]
    past = n_pages * PAGE_SIZE
    nq, nk = C_HEADS * HEAD_DIM, C_KV * HEAD_DIM
    head_major = lambda a, d: a.reshape(B, T, -1, d).transpose(0, 2, 1, 3).reshape(B, -1, d)
    as_kv = lambda a: a.reshape(B, -1, 2, C_KV, HEAD_DIM)
    q = head_major(_bf16(z[..., :nq]), HEAD_DIM)
    branch = [z[..., nq + 2 * i * nk:nq + 2 * (i + 1) * nk] for i in range(3)]
    gl = head_major(z[..., nq + 6 * nk:], 3)
    n_cmp = (past + T - CMP_BLK) // CMP_STRIDE + 1
    if (n_cmp + 1) * CMP_STRIDE <= past:
        k_cmp, v_cmp = nsa_compress_rows(gather_page_rowgroups(pool_cmp, page_table, layer), n_cmp, pe, w1, w2, layer)
    else:
        cmp_rows = jnp.concatenate([gather_pages_pallas(pool_cmp, page_table, layer), as_kv(branch[0])], axis=1)
        k_cmp, v_cmp = nsa_compress(cmp_rows, n_cmp, pe, w1, w2, layer)
    n_pad = -(-n_cmp // LANES) * LANES
    pad = ((0, 0), (0, 0), (0, n_pad - n_cmp), (0, 0))
    kct = _bf16(jnp.pad(k_cmp, pad)).transpose(0, 1, 3, 2).reshape(B, nk, n_pad)
    vc = _bf16(jnp.pad(v_cmp, pad)).transpose(0, 2, 1, 3).reshape(B, n_pad, nk)
    n_slc = -(-(past + T) // SLC_BLK)
    n_blk = -(-n_slc // LANES) * LANES
    win_all = jnp.concatenate([win_buf.reshape(B, WINDOW, 2 * nk), branch[2]], axis=1)
    n_win = -(-(WINDOW + T) // LANES) * LANES
    win_pad = jnp.pad(win_all, ((0, 0), (0, n_win - WINDOW - T), (0, 0)))
    per_b = lambda shape: pl.BlockSpec(shape, lambda b: (b, 0, 0))
    hm_rows = C_HEADS * T
    o_cmp, o_win, sel = pl.pallas_call(
        functools.partial(_nsa_sample_cmpwin_kernel, n_cmp=n_cmp, n_slc=n_slc, q_pos0=past, win_pos0=past - WINDOW, t_new=T),
        grid=(B,),
        in_specs=[per_b((1, hm_rows, HEAD_DIM)), per_b((1, nk, n_pad)), per_b((1, n_pad, nk)), per_b((1, n_win, 2 * nk))],
        out_specs=[per_b((1, hm_rows, HEAD_DIM)), per_b((1, hm_rows, HEAD_DIM)), per_b((1, C_KV * T, n_blk))],
        out_shape=[jax.ShapeDtypeStruct((B, hm_rows, HEAD_DIM), jnp.float32)] * 2
        + [jax.ShapeDtypeStruct((B, C_KV * T, n_blk), jnp.float32)],
        scratch_shapes=[pltpu.VMEM((C_KV * T, n_blk), jnp.int32)],
        compiler_params=_params("parallel"))(q, kct, vc, win_pad)
    per_b3 = lambda shape: pl.BlockSpec(shape, lambda b, p, pt: (b, 0, 0))
    assert n_pages % PAGES_PER_STEP == 0 and T <= STEP_KEYS
    new_pad = jnp.pad(as_kv(branch[1]), ((0, 0), (0, STEP_KEYS - T), (0, 0), (0, 0), (0, 0)))
    kv_row = (2, C_KV, HEAD_DIM)
    page_of = lambda b, p, pt, i: pt[b, jnp.minimum(p * PAGES_PER_STEP + i, n_pages - 1)]
    o = pl.pallas_call(
        functools.partial(_nsa_sample_slc_kernel, n_pages=n_pages, t_new=T),
        grid_spec=pltpu.PrefetchScalarGridSpec(
            num_scalar_prefetch=1, grid=(B, n_pages // PAGES_PER_STEP + 1),
            in_specs=[per_b3((1, hm_rows, HEAD_DIM)), per_b3((1, C_KV * T, n_blk)), per_b3((1, hm_rows, 3)),
                      per_b3((1, hm_rows, HEAD_DIM)), per_b3((1, hm_rows, HEAD_DIM))]
            + [pl.BlockSpec((None, PAGE_SIZE, None) + kv_row, lambda b, p, pt, i=i: (page_of(b, p, pt, i), 0, layer, 0, 0, 0))
               for i in range(PAGES_PER_STEP)]
            + [pl.BlockSpec((1, STEP_KEYS) + kv_row, lambda b, p, pt: (b, 0, 0, 0, 0))],
            out_specs=per_b3((1, hm_rows, HEAD_DIM)),
            scratch_shapes=[pltpu.VMEM((hm_rows, 1), jnp.float32), pltpu.VMEM((hm_rows, 1), jnp.float32),
                            pltpu.VMEM((hm_rows, HEAD_DIM), jnp.float32)]),
        out_shape=jax.ShapeDtypeStruct((B, hm_rows, HEAD_DIM), jnp.bfloat16),
        compiler_params=_params("parallel", "arbitrary"))(
            page_table, q, sel, gl, o_cmp, o_win, *[pool_slc] * PAGES_PER_STEP, new_pad)
    o = o.reshape(B, C_HEADS, T, HEAD_DIM).transpose(0, 2, 1, 3).reshape(B, T, nq)
    return o, as_kv(branch[0]), as_kv(branch[1]), as_kv(win_all)[:, -WINDOW:]


def kernel(x_prompt, x_sample, cache_dsa_kv, cache_dsa_idx, state_gdn_conv, state_gdn_rec, cache_nsa_cmp, cache_nsa_slc, cache_nsa_win, page_table, c_prompt, c_sample, ada_w, ada_b, ln_g, ln_b, dsa_w_in, dsa_w_out, gdn_w_in, gdn_conv_w, gdn_a_log, gdn_dt_bias, gdn_norm_g, gdn_w_out, nsa_w_in, nsa_cmp_pe, nsa_cmp_w1, nsa_cmp_w2, nsa_w_out, moe_w_router, moe_b_router, moe_w_in, moe_b_in, moe_w_out, moe_b_out):
    xp, xs = x_prompt, x_sample
    bp, tp, _ = xp.shape
    bs, ts, _ = xs.shape
    n_p, n_s = bp * tp, bs * ts
    outs = {k: [] for k in ("dsa_kv_p", "dsa_kv_s", "dsa_idx_p", "dsa_idx_s", "gdn_conv_p", "gdn_conv_s", "gdn_rec_p",
                            "gdn_rec_s", "nsa_cmp_p", "nsa_cmp_s", "nsa_slc_p", "nsa_slc_s", "nsa_win_p", "nsa_win_s")}
    moe_w = (moe_w_router, moe_b_router, moe_w_in, moe_b_in, moe_w_out, moe_b_out)
    c_all = jax.nn.silu(jnp.concatenate([c_prompt, c_sample], axis=0))
    for i in range(DEPTH):
        kind, j = i % N_MIXERS, i // N_MIXERS
        mod = matmul(c_all, ada_w, i, ada_b).reshape(bp + bs, 6, D_MODEL)
        mp, ms = mod[:bp], mod[bp:]
        if kind == 0:
            w_in, w_out = dsa_w_in, dsa_w_out
        elif kind == 1:
            w_in, w_out = gdn_w_in, gdn_w_out
        else:
            w_in, w_out = nsa_w_in, nsa_w_out
        zp = matmul_modulated(xp, mp[:, 1], mp[:, 0], w_in, j)
        hs = xs * (1.0 + ms[:, 1][:, None, :]) + ms[:, 0][:, None, :]
        zs = matmul(hs.reshape(n_s, D_MODEL), w_in, j).reshape(bs, ts, -1)
        if kind == 0:
            op, kv_p, ki_p = dsa_prompt(zp)
            os_, kv_s, ki_s = dsa_sample(zs, cache_dsa_kv, cache_dsa_idx, page_table, j)
            outs["dsa_kv_p"].append(kv_p)
            outs["dsa_kv_s"].append(kv_s)
            outs["dsa_idx_p"].append(ki_p)
            outs["dsa_idx_s"].append(ki_s)
        elif kind == 1:
            gdn_args = (gdn_conv_w[j], gdn_a_log[j], gdn_dt_bias[j], gdn_norm_g[j])
            zero_conv = jnp.zeros((bp, B_CONV - 1, B_CONV_CH), xp.dtype)
            zero_rec = jnp.zeros((bp, B_V_HEADS, B_DK, B_DV), xp.dtype)
            op, conv_p, rec_p = gdn_mixer(zp, *gdn_args, zero_conv, zero_rec)
            os_, conv_s, rec_s = gdn_mixer(zs, *gdn_args, state_gdn_conv[:, j], state_gdn_rec[:, j])
            outs["gdn_conv_p"].append(conv_p)
            outs["gdn_conv_s"].append(conv_s)
            outs["gdn_rec_p"].append(rec_p)
            outs["gdn_rec_s"].append(rec_s)
        else:
            nsa_args = (nsa_cmp_pe, nsa_cmp_w1, nsa_cmp_w2)
            op, cmp_p, slc_p, win_p = nsa_prompt(zp, *nsa_args, j)
            os_, cmp_s, slc_s, win_s = nsa_sample(zs, *nsa_args, cache_nsa_cmp, cache_nsa_slc, cache_nsa_win[:, j],
                                                  page_table, j)
            outs["nsa_cmp_p"].append(cmp_p)
            outs["nsa_cmp_s"].append(cmp_s)
            outs["nsa_slc_p"].append(slc_p)
            outs["nsa_slc_s"].append(slc_s)
            outs["nsa_win_p"].append(win_p)
            outs["nsa_win_s"].append(win_s)
        yp = matmul(op.reshape(n_p, -1), w_out, j).reshape(bp, tp, D_MODEL)
        ys = matmul(os_.reshape(n_s, -1), w_out, j).reshape(bs, ts, D_MODEL)
        xp, hp = post_norm_modulate(xp, yp, mp[:, 2], ln_g[i, 0], ln_b[i, 0], mp[:, 4], mp[:, 3])
        xs, hs = post_norm_modulate(xs, ys, ms[:, 2], ln_g[i, 0], ln_b[i, 0], ms[:, 4], ms[:, 3])
        h_all = jnp.concatenate([hp.reshape(n_p, D_MODEL), hs.reshape(n_s, D_MODEL)], axis=0)
        y_k, w_k = moe(h_all, i, *moe_w)
        xp = post_norm_moe(xp, y_k, w_k, 0, mp[:, 5], ln_g[i, 1], ln_b[i, 1])
        xs = post_norm_moe(xs, y_k, w_k, n_p, ms[:, 5], ln_g[i, 1], ln_b[i, 1])
    st = lambda k, ax: jnp.stack(outs[k], axis=ax)
    return (xp, xs,
            st("dsa_kv_p", 2), st("dsa_kv_s", 2), st("dsa_idx_p", 2), st("dsa_idx_s", 2),
            st("gdn_conv_p", 1), st("gdn_conv_s", 1), st("gdn_rec_p", 1), st("gdn_rec_s", 1),
            st("nsa_cmp_p", 2), st("nsa_cmp_s", 2), st("nsa_slc_p", 2), st("nsa_slc_s", 2),
            st("nsa_win_p", 1), st("nsa_win_s", 1))
```
